```python
import math
import jax, jax.numpy as jnp
from jax import lax
import numpy as np


D_MODEL = 2048
BATCH = 4
SEQ = 4096
DEPTH = 2

GRID_W = 64
CTX_LEN = 256
HEAD_DIM = 128
EPS = 1e-6
NEG_INF = -1e30

FOURIER_GROUPS = 4
FOURIER_WIDTH = D_MODEL // 4
FOURIER_GROUP_DIM = FOURIER_WIDTH // FOURIER_GROUPS
NA_HEADS = (D_MODEL - FOURIER_WIDTH) // HEAD_DIM
NA_WIDTH = NA_HEADS * HEAD_DIM
NA_WIN_H = 8
NA_WIN_W = 16
NA_KEY_W = 2 * NA_WIN_W
EVEN_IN = FOURIER_WIDTH + 3 * NA_WIDTH

POOL_WINDOWS = (2, 4, 8, 16)
POOL_GROUPS = 4
POOL_WIDTH = D_MODEL // 4
POOL_GROUP_DIM = POOL_WIDTH // POOL_GROUPS
MLA_HEADS = (D_MODEL - POOL_WIDTH) // HEAD_DIM
Q_LORA_RANK = 768
KV_LORA_RANK = 512
QK_NOPE_DIM = 128
QK_ROPE_DIM = 64
V_HEAD_DIM = 128
ODD_IN = POOL_WIDTH + Q_LORA_RANK + KV_LORA_RANK + QK_ROPE_DIM
ROPE_BASE = 10000.0
ATTN_BLOCK = 128

D_FF = -(-8 * D_MODEL // (3 * 256)) * 256

N_EVEN = (DEPTH + 1) // 2
N_ODD = DEPTH // 2

kernel_name = 'hybrid_fourier_natten_pool_mla_dit'


def rmsnorm(x, g):
    xf = x.astype(jnp.float32)
    y = xf * lax.rsqrt(jnp.mean(xf * xf, axis=-1, keepdims=True) + EPS)
    return (y * g.astype(jnp.float32)).astype(x.dtype)


def modulate(h, shift, scale):
    return h * (1.0 + scale) + shift


def swiglu(h, wg, wu, wd):
    return (jax.nn.silu(h @ wg) * (h @ wu)) @ wd


def rope_2d(x):
    n = x.shape[1]
    t = jnp.arange(n)
    half = QK_ROPE_DIM // 2
    nf = half // 2
    inv = ROPE_BASE ** (-jnp.arange(nf, dtype=jnp.float32) / nf)

    def rot(xa, pos):
        ang = pos.astype(jnp.float32)[:, None] * inv[None, :]
        cos = jnp.cos(ang)[None, :, None, :]
        sin = jnp.sin(ang)[None, :, None, :]
        x1 = xa[..., :nf].astype(jnp.float32)
        x2 = xa[..., nf:].astype(jnp.float32)
        return jnp.concatenate([x1 * cos - x2 * sin, x1 * sin + x2 * cos], axis=-1)

    out = jnp.concatenate([rot(x[..., :half], t // GRID_W), rot(x[..., half:], t % GRID_W)], axis=-1)
    return out.astype(x.dtype)


def fourier_mix(u, w_four):
    b, n, _ = u.shape
    ug = u.reshape(b, n, FOURIER_GROUPS, FOURIER_GROUP_DIM).astype(jnp.float32)
    f = jnp.fft.fft2(ug, axes=(1, 3), norm='ortho').real.astype(u.dtype)
    y = jnp.einsum('bngc,gcd->bngd', f, w_four)
    return y.reshape(b, n, FOURIER_WIDTH)


def pool_mix(u, w_pool, pool_scale):
    b, n, _ = u.shape
    ug = u.reshape(b, n, POOL_GROUPS, POOL_GROUP_DIM)
    cs = jnp.concatenate([jnp.zeros((b, 1, POOL_GROUPS, POOL_GROUP_DIM), jnp.float32),
                          jnp.cumsum(ug.astype(jnp.float32), axis=1)], axis=1)
    t = jnp.arange(n)
    outs = []
    for g, w in enumerate(POOL_WINDOWS):
        lo = jnp.maximum(t - w // 2, 0)
        hi = jnp.minimum(t + w // 2, n)
        s = cs[:, hi, g] - cs[:, lo, g]
        cnt = (hi - lo).astype(jnp.float32)[None, :, None]
        outs.append(s / cnt - ug[:, :, g].astype(jnp.float32))
    p = jnp.stack(outs, axis=2).astype(u.dtype)
    y = jnp.einsum('bngc,gcd->bngd', p, w_pool).reshape(b, n, POOL_WIDTH)
    return y * pool_scale


def dense_ctx_attention(q, k, v):
    b, l, h, _ = q.shape
    s = jnp.einsum('bqhd,bkhd->bhqk', q, k).astype(jnp.float32) * (q.shape[-1] ** -0.5)
    p = jax.nn.softmax(s, axis=-1).astype(v.dtype)
    return jnp.einsum('bhqk,bkhd->bqhd', p, v).reshape(b, l, h * v.shape[-1])


def dense_block_attention(q, k_all, v_all):
    b, n, h, dq = q.shape
    dv = v_all.shape[-1]
    scale = dq ** -0.5
    qb = jnp.moveaxis(q.reshape(b, n // ATTN_BLOCK, ATTN_BLOCK, h, dq), 1, 0)

    def blk(qi):
        s = jnp.einsum('bqhd,bkhd->bhqk', qi, k_all).astype(jnp.float32) * scale
        p = jax.nn.softmax(s, axis=-1).astype(v_all.dtype)
        return jnp.einsum('bhqk,bkhd->bqhd', p, v_all)

    o = lax.map(blk, qb)
    return jnp.moveaxis(o, 0, 1).reshape(b, n, h * dv)


def neighbourhood_attention(q, k, v, kc, vc, rpb):
    b, n, h, d = q.shape
    rows = n // GRID_W
    kh = min(NA_WIN_H, rows)
    n_cb = GRID_W // NA_WIN_W
    scale = d ** -0.5
    qg = q.reshape(b, rows, GRID_W, h, d)
    kg = k.reshape(b, rows, GRID_W, h, d)
    vg = v.reshape(b, rows, GRID_W, h, d)
    qcol = np.arange(n_cb)[:, None] * NA_WIN_W + np.arange(NA_WIN_W)[None, :]
    kstart = np.clip(np.arange(n_cb) * NA_WIN_W - NA_WIN_W // 2, 0, GRID_W - NA_KEY_W)
    kcol = kstart[:, None] + np.arange(NA_KEY_W)[None, :]
    cstart = np.clip(qcol - NA_WIN_W // 2, 0, GRID_W - NA_WIN_W)
    col_mask = jnp.asarray((kcol[:, None, :] >= cstart[..., None]) & (kcol[:, None, :] < cstart[..., None] + NA_WIN_W))
    col_idx = jnp.asarray(np.clip(kcol[:, None, :] - qcol[..., None] + NA_WIN_W - 1, 0, 2 * NA_WIN_W - 2))
    kcol_flat = kcol.reshape(-1)

    def row_block(r):
        rs = jnp.clip(r - kh // 2, 0, rows - kh)
        q_r = lax.dynamic_index_in_dim(qg, r, axis=1, keepdims=False).reshape(b, n_cb, NA_WIN_W, h, d)
        k_r = lax.dynamic_slice_in_dim(kg, rs, kh, axis=1)[:, :, kcol_flat].reshape(b, kh, n_cb, NA_KEY_W, h, d)
        v_r = lax.dynamic_slice_in_dim(vg, rs, kh, axis=1)[:, :, kcol_flat].reshape(b, kh, n_cb, NA_KEY_W, h, d)
        s_loc = jnp.einsum('bjqhd,brjkhd->bhjqrk', q_r, k_r).astype(jnp.float32) * scale
        row_idx = rs + jnp.arange(kh) - r + NA_WIN_H - 1
        bias = rpb[:, row_idx[:, None, None, None], col_idx[None]]
        bias = jnp.transpose(bias, (0, 2, 3, 1, 4)).astype(jnp.float32)
        s_loc = jnp.where(col_mask[:, :, None, :], s_loc + bias, NEG_INF)
        s_loc = s_loc.reshape(b, h, n_cb, NA_WIN_W, kh * NA_KEY_W)
        s_ctx = jnp.einsum('bjqhd,blhd->bhjql', q_r, kc).astype(jnp.float32) * scale
        p = jax.nn.softmax(jnp.concatenate([s_loc, s_ctx], axis=-1), axis=-1)
        p_loc = p[..., :kh * NA_KEY_W].reshape(b, h, n_cb, NA_WIN_W, kh, NA_KEY_W).astype(v.dtype)
        p_ctx = p[..., kh * NA_KEY_W:].astype(v.dtype)
        o = jnp.einsum('bhjqrk,brjkhd->bjqhd', p_loc, v_r) + jnp.einsum('bhjql,blhd->bjqhd', p_ctx, vc)
        return o.reshape(b, GRID_W, h, d)

    o = lax.map(row_block, jnp.arange(rows))
    return jnp.moveaxis(o, 0, 1).reshape(b, n, h * d)


def mla_queries(cq_raw, g_q, w_uq, rotate):
    b, n, _ = cq_raw.shape
    q = (rmsnorm(cq_raw, g_q) @ w_uq).reshape(b, n, MLA_HEADS, QK_NOPE_DIM + QK_ROPE_DIM)
    if rotate:
        q = jnp.concatenate([q[..., :QK_NOPE_DIM], rope_2d(q[..., QK_NOPE_DIM:])], axis=-1)
    return q


def mla_keys_values(ckv_raw, k_rope, g_kv, w_ukv, rotate):
    b, n, _ = ckv_raw.shape
    kv = (rmsnorm(ckv_raw, g_kv) @ w_ukv).reshape(b, n, MLA_HEADS, QK_NOPE_DIM + V_HEAD_DIM)
    k_nope, v = kv[..., :QK_NOPE_DIM], kv[..., QK_NOPE_DIM:]
    kr = k_rope[:, :, None, :]
    if rotate:
        kr = rope_2d(kr)
    k = jnp.concatenate([k_nope, jnp.broadcast_to(kr, (b, n, MLA_HEADS, QK_ROPE_DIM))], axis=-1)
    return k, v


def even_mixer(h, hc, w_in, w_four, rpb, w_out, with_ctx_out):
    b, n, _ = h.shape
    lc = hc.shape[1]
    p = h @ w_in
    u = p[..., :FOURIER_WIDTH]
    q, k, v = jnp.split(p[..., FOURIER_WIDTH:], 3, axis=-1)
    q = q.reshape(b, n, NA_HEADS, HEAD_DIM)
    k = k.reshape(b, n, NA_HEADS, HEAD_DIM)
    v = v.reshape(b, n, NA_HEADS, HEAD_DIM)
    if with_ctx_out:
        pc = hc @ w_in
        uc = pc[..., :FOURIER_WIDTH]
        qc, kc, vc = jnp.split(pc[..., FOURIER_WIDTH:], 3, axis=-1)
        qc = qc.reshape(b, lc, NA_HEADS, HEAD_DIM)
    else:
        kc, vc = jnp.split(hc @ w_in[:, FOURIER_WIDTH + NA_WIDTH:], 2, axis=-1)
    kc = kc.reshape(b, lc, NA_HEADS, HEAD_DIM)
    vc = vc.reshape(b, lc, NA_HEADS, HEAD_DIM)
    y = jnp.concatenate([fourier_mix(u, w_four), neighbourhood_attention(q, k, v, kc, vc, rpb)], axis=-1) @ w_out
    yc = None
    if with_ctx_out:
        yc = jnp.concatenate([fourier_mix(uc, w_four), dense_ctx_attention(qc, kc, vc)], axis=-1) @ w_out
    return y, yc


def odd_mixer(h, hc, w_in, w_pool, pool_scale, g_q, g_kv, w_uq, w_ukv, w_out, with_ctx_out):
    o_q = POOL_WIDTH
    o_kv = o_q + Q_LORA_RANK
    o_kr = o_kv + KV_LORA_RANK
    p = h @ w_in
    q = mla_queries(p[..., o_q:o_kv], g_q, w_uq, True)
    k, v = mla_keys_values(p[..., o_kv:o_kr], p[..., o_kr:], g_kv, w_ukv, True)
    if with_ctx_out:
        pc = hc @ w_in
        pc_kv = pc[..., o_kv:]
    else:
        pc_kv = hc @ w_in[:, o_kv:]
    kc, vc = mla_keys_values(pc_kv[..., :KV_LORA_RANK], pc_kv[..., KV_LORA_RANK:], g_kv, w_ukv, False)
    attn = dense_block_attention(q, jnp.concatenate([kc, k], axis=1), jnp.concatenate([vc, v], axis=1))
    y = jnp.concatenate([pool_mix(p[..., :o_q], w_pool, pool_scale), attn], axis=-1) @ w_out
    yc = None
    if with_ctx_out:
        qc = mla_queries(pc[..., o_q:o_kv], g_q, w_uq, False)
        yc = jnp.concatenate([pool_mix(pc[..., :o_q], w_pool, pool_scale), dense_ctx_attention(qc, kc, vc)], axis=-1) @ w_out
    return y, yc


def setup_inputs(seed: int = 0) -> dict:
    key = jax.random.key(seed)
    ks = jax.random.split(key, 24)
    f32 = jnp.float32

    def nrm(k, shape, scale):
        return jax.random.normal(k, shape, f32) * scale

    def gain(k, shape):
        return 1.0 + 0.02 * jax.random.normal(k, shape, f32)

    return {
        'x': nrm(ks[0], (BATCH, SEQ, D_MODEL), 1.0),
        'c': nrm(ks[1], (BATCH, D_MODEL), 1.0),
        'ctx': nrm(ks[2], (BATCH, CTX_LEN, D_MODEL), 1.0),
        'c_ctx': nrm(ks[3], (D_MODEL,), 1.0),
        'w_mod': nrm(ks[4], (DEPTH, D_MODEL, 6 * D_MODEL), 0.5 * D_MODEL ** -0.5),
        'b_mod': nrm(ks[5], (DEPTH, 6 * D_MODEL), 0.01),
        'norm1_g': gain(ks[6], (DEPTH, D_MODEL)),
        'norm2_g': gain(ks[7], (DEPTH, D_MODEL)),
        'w_in_ab': nrm(ks[8], (N_EVEN, D_MODEL, EVEN_IN), D_MODEL ** -0.5),
        'w_four': nrm(ks[9], (N_EVEN, FOURIER_GROUPS, FOURIER_GROUP_DIM, FOURIER_GROUP_DIM), FOURIER_GROUP_DIM ** -0.5),
        'na_rpb': nrm(ks[10], (N_EVEN, NA_HEADS, 2 * NA_WIN_H - 1, 2 * NA_WIN_W - 1), 0.1),
        'w_out_ab': nrm(ks[11], (N_EVEN, D_MODEL, D_MODEL), D_MODEL ** -0.5),
        'w_in_cd': nrm(ks[12], (N_ODD, D_MODEL, ODD_IN), D_MODEL ** -0.5),
        'w_pool': nrm(ks[13], (N_ODD, POOL_GROUPS, POOL_GROUP_DIM, POOL_GROUP_DIM), POOL_GROUP_DIM ** -0.5),
        'pool_scale': 1.0 + 0.1 * jax.random.normal(ks[14], (N_ODD, POOL_WIDTH), f32),
        'mla_gq': gain(ks[15], (N_ODD, Q_LORA_RANK)),
        'mla_gkv': gain(ks[16], (N_ODD, KV_LORA_RANK)),
        'w_uq': nrm(ks[17], (N_ODD, Q_LORA_RANK, MLA_HEADS * (QK_NOPE_DIM + QK_ROPE_DIM)), Q_LORA_RANK ** -0.5),
        'w_ukv': nrm(ks[18], (N_ODD, KV_LORA_RANK, MLA_HEADS * (QK_NOPE_DIM + V_HEAD_DIM)), KV_LORA_RANK ** -0.5),
        'w_out_cd': nrm(ks[19], (N_ODD, D_MODEL, D_MODEL), D_MODEL ** -0.5),
        'w_ffn_gate': nrm(ks[20], (DEPTH, D_MODEL, D_FF), D_MODEL ** -0.5),
        'w_ffn_up': nrm(ks[21], (DEPTH, D_MODEL, D_FF), D_MODEL ** -0.5),
        'w_ffn_down': nrm(ks[22], (DEPTH, D_FF, D_MODEL), D_FF ** -0.5),
        'final_g': gain(ks[23], (D_MODEL,)),
    }


def reference(x, c, ctx, c_ctx, w_mod, b_mod, norm1_g, norm2_g, w_in_ab, w_four, na_rpb, w_out_ab,
              w_in_cd, w_pool, pool_scale, mla_gq, mla_gkv, w_uq, w_ukv, w_out_cd,
              w_ffn_gate, w_ffn_up, w_ffn_down, final_g):
    silu_c = jax.nn.silu(c)
    silu_cc = jax.nn.silu(c_ctx)
    xc = ctx
    for l in range(DEPTH):
        last = l == DEPTH - 1
        mod = (silu_c @ w_mod[l] + b_mod[l])[:, None, :]
        modc = (silu_cc @ w_mod[l] + b_mod[l])[None, None, :]
        sh1, sc1, g1, sh2, sc2, g2 = jnp.split(mod, 6, axis=-1)
        csh1, csc1, cg1, csh2, csc2, cg2 = jnp.split(modc, 6, axis=-1)
        h = modulate(rmsnorm(x, norm1_g[l]), sh1, sc1)
        hc = modulate(rmsnorm(xc, norm1_g[l]), csh1, csc1)
        i = l // 2
        if l % 2 == 0:
            y, yc = even_mixer(h, hc, w_in_ab[i], w_four[i], na_rpb[i], w_out_ab[i], not last)
        else:
            y, yc = odd_mixer(h, hc, w_in_cd[i], w_pool[i], pool_scale[i], mla_gq[i], mla_gkv[i],
                              w_uq[i], w_ukv[i], w_out_cd[i], not last)
        x = x + g1 * y
        x = x + g2 * swiglu(modulate(rmsnorm(x, norm2_g[l]), sh2, sc2), w_ffn_gate[l], w_ffn_up[l], w_ffn_down[l])
        if not last:
            xc = xc + cg1 * yc
            xc = xc + cg2 * swiglu(modulate(rmsnorm(xc, norm2_g[l]), csh2, csc2), w_ffn_gate[l], w_ffn_up[l], w_ffn_down[l])
    return rmsnorm(x, final_g)
```

```python
import functools
import math

import numpy as np
import jax
import jax.numpy as jnp
from jax import lax
from jax.experimental import pallas as pl
from jax.experimental.pallas import tpu as pltpu

GRID_W = 64
HEAD_DIM = 128
EPS = 1e-6
NEG_INF = -1e30
NA_WIN_H = 8
NA_WIN_W = 16
POOL_WINDOWS = (2, 4, 8, 16)
QK_NOPE_DIM = 128
QK_ROPE_DIM = 64
V_HEAD_DIM = 128
ROPE_BASE = 10000.0

V7X_LANES = 128
V7X_VMEM_BYTES = 64 * 1024 * 1024
V7X_VMEM_CAP = 56 * 1024 * 1024
NA_ROWS_PER_TILE = 2

BF16 = jnp.bfloat16
F32 = jnp.float32


def _pick_tile(n, target, mult=16):
    if n <= target:
        return n
    for t in range(target, 0, -1):
        if n % t == 0 and t % mult == 0:
            return t
    return n


def _nbytes(shape, dtype):
    return int(np.prod(shape)) * jnp.dtype(dtype).itemsize


def _params(block_bytes, scratch_bytes=0, ndims=3):
    est = 2 * block_bytes + scratch_bytes
    limit = int(min(max(est * 5 // 4 + (4 << 20), 16 << 20), V7X_VMEM_CAP))
    return pltpu.CompilerParams(dimension_semantics=("arbitrary",) * ndims, vmem_limit_bytes=limit)


def _mod_kernel(c_ref, w_ref, b_ref, o_ref):
    cv = c_ref[...]
    s = cv * (1.0 / (1.0 + jnp.exp(-cv)))
    acc = jnp.dot(s.astype(BF16), w_ref[...].astype(BF16), preferred_element_type=F32)
    o_ref[...] = acc + b_ref[...]


def _mod_call(cvec, w_mod, b_mod):
    depth, d, n = w_mod.shape
    r = cvec.shape[0]
    tn = _pick_tile(n, 1024, V7X_LANES)
    blocks = _nbytes((r, d), F32) + _nbytes((d, tn), F32) + 2 * _nbytes((r, tn), F32)
    return pl.pallas_call(
        _mod_kernel,
        grid=(depth, n // tn),
        in_specs=[
            pl.BlockSpec((r, d), lambda l, j: (0, 0)),
            pl.BlockSpec((None, d, tn), lambda l, j: (l, 0, j)),
            pl.BlockSpec((None, 1, tn), lambda l, j: (l, 0, j)),
        ],
        out_specs=pl.BlockSpec((None, r, tn), lambda l, j: (l, 0, j)),
        out_shape=jax.ShapeDtypeStruct((depth, r, n), F32),
        compiler_params=_params(blocks, _nbytes((d, tn), BF16), 2),
        name="mod_vectors",
    )(cvec, w_mod, b_mod.reshape(depth, 1, n))


def _norm_mod_kernel(x_ref, g_ref, sh_ref, sc_ref, o_ref):
    xf = x_ref[...]
    y = xf * lax.rsqrt(jnp.mean(xf * xf, axis=-1, keepdims=True) + EPS) * g_ref[...]
    o_ref[...] = (y * (1.0 + sc_ref[...]) + sh_ref[...]).astype(o_ref.dtype)


def _norm_kernel(x_ref, g_ref, o_ref):
    xf = x_ref[...]
    y = xf * lax.rsqrt(jnp.mean(xf * xf, axis=-1, keepdims=True) + EPS) * g_ref[...]
    o_ref[...] = y.astype(o_ref.dtype)


def _norm_call(x, g, shift=None, scale=None, out_dtype=BF16):
    b, s, d = x.shape
    ts = _pick_tile(s, 512)
    xspec = pl.BlockSpec((None, ts, d), lambda bi, i: (bi, i, 0))
    gspec = pl.BlockSpec((1, d), lambda bi, i: (0, 0))
    mspec = pl.BlockSpec((None, 1, d), lambda bi, i: (bi, 0, 0))
    blocks = _nbytes((ts, d), F32) + _nbytes((ts, d), out_dtype) + 3 * _nbytes((1, d), F32)
    common = dict(
        grid=(b, s // ts),
        out_specs=xspec,
        out_shape=jax.ShapeDtypeStruct((b, s, d), out_dtype),
        compiler_params=_params(blocks, 2 * _nbytes((ts, d), F32), 2),
    )
    if shift is None:
        return pl.pallas_call(_norm_kernel, in_specs=[xspec, gspec], name="rmsnorm", **common)(
            x, g.reshape(1, d))
    return pl.pallas_call(_norm_mod_kernel, in_specs=[xspec, gspec, mspec, mspec], name="rmsnorm_modulate",
                          **common)(x, g.reshape(1, d), shift, scale)


def _mm_kernel(*refs, n_ops, n_extra, epilogue):
    a_refs = refs[:n_ops]
    w_refs = refs[n_ops:2 * n_ops]
    e_refs = refs[2 * n_ops:2 * n_ops + n_extra]
    o_refs = refs[2 * n_ops + n_extra:]
    acc = None
    for a_ref, w_ref in zip(a_refs, w_refs):
        part = jnp.dot(a_ref[...], w_ref[...], preferred_element_type=F32)
        acc = part if acc is None else acc + part
    epilogue(acc, e_refs, o_refs)


def _mm_call(a_ops, w_ops, extras, epilogue, outs, grid, tm, tn, name, tmp_bytes=0):
    arrays = [a for a, _ in a_ops] + [w for w, _ in w_ops] + [e for e, _ in extras]
    in_specs = [s for _, s in a_ops] + [s for _, s in w_ops] + [s for _, s in extras]
    blocks = 0
    for arr, spec in a_ops + w_ops + extras:
        blocks += _nbytes([d for d in spec.block_shape if d is not None], arr.dtype)
    for sds, spec in outs:
        blocks += _nbytes([d for d in spec.block_shape if d is not None], sds.dtype)
    kern = functools.partial(_mm_kernel, n_ops=len(a_ops), n_extra=len(extras), epilogue=epilogue)
    res = pl.pallas_call(
        kern,
        grid=grid,
        in_specs=in_specs,
        out_specs=[s for _, s in outs],
        out_shape=[o for o, _ in outs],
        compiler_params=_params(blocks, 3 * _nbytes((tm, tn), F32) + tmp_bytes, len(grid)),
        name=name,
    )(*arrays)
    return res


def _a_spec(tm, k):
    return pl.BlockSpec((None, tm, k), lambda b, i, j: (b, i, 0))


def _w_spec(k, tn):
    return pl.BlockSpec((k, tn), lambda b, i, j: (0, j))


def _tile_spec(tm, tn):
    return pl.BlockSpec((None, tm, tn), lambda b, i, j: (b, i, j))


def _vec_spec(tn):
    return pl.BlockSpec((None, 1, tn), lambda b, i, j: (b, 0, j))


def _ep_store(acc, e_refs, o_refs):
    o_refs[0][...] = acc.astype(o_refs[0].dtype)


def _mm_plain(a, w, out_dtype, name, tm_target=1024, tn_target=512):
    b, s, k = a.shape
    n = w.shape[1]
    tm = _pick_tile(s, tm_target)
    tn = _pick_tile(n, tn_target, V7X_LANES)
    return _mm_call([(a, _a_spec(tm, k))], [(w, _w_spec(k, tn))], [], _ep_store,
                    [(jax.ShapeDtypeStruct((b, s, n), out_dtype), _tile_spec(tm, tn))],
                    (b, s // tm, n // tn), tm, tn, name)[0]


def _ep_chunks(acc, e_refs, o_refs, *, width):
    o_ref = o_refs[0]
    for c in range(acc.shape[1] // width):
        o_ref[c] = acc[:, c * width:(c + 1) * width].astype(o_ref.dtype)


def _mm_chunked(a, w, name, width=HEAD_DIM, tm_target=1024, tn_target=512):
    b, s, k = a.shape
    n = w.shape[1]
    tm = _pick_tile(s, tm_target)
    tn = _pick_tile(n, tn_target, width)
    cpt = tn // width
    ospec = pl.BlockSpec((None, cpt, tm, width), lambda bi, i, j: (bi, j, i, 0))
    return _mm_call([(a, _a_spec(tm, k))], [(w, _w_spec(k, tn))], [],
                    functools.partial(_ep_chunks, width=width),
                    [(jax.ShapeDtypeStruct((b, n // width, s, width), BF16), ospec)],
                    (b, s // tm, n // tn), tm, tn, name)[0]


def _ep_residual(acc, e_refs, o_refs):
    x_ref, g_ref = e_refs
    o_refs[0][...] = x_ref[...] + g_ref[...] * acc


def _mm_residual(a_list, w_list, x, gate, name, tm_target=1024, tn_target=512):
    b, s, n = x.shape
    tm = _pick_tile(s, tm_target)
    tn = _pick_tile(n, tn_target, V7X_LANES)
    a_ops = [(a, _a_spec(tm, a.shape[2])) for a in a_list]
    w_ops = [(w, _w_spec(w.shape[0], tn)) for w in w_list]
    extras = [(x, _tile_spec(tm, tn)), (gate, _vec_spec(tn))]
    return _mm_call(a_ops, w_ops, extras, _ep_residual,
                    [(jax.ShapeDtypeStruct((b, s, n), F32), _tile_spec(tm, tn))],
                    (b, s // tm, n // tn), tm, tn, name)[0]


def _swiglu_kernel(a_ref, wg_ref, wu_ref, o_ref):
    a = a_ref[...]
    g = jnp.dot(a, wg_ref[...], preferred_element_type=F32)
    u = jnp.dot(a, wu_ref[...], preferred_element_type=F32)
    o_ref[...] = (g * (1.0 / (1.0 + jnp.exp(-g))) * u).astype(o_ref.dtype)


def _swiglu_up(a, wg, wu, name, tm_target=1024, tn_target=512):
    b, s, k = a.shape
    n = wg.shape[1]
    tm = _pick_tile(s, tm_target)
    tn = _pick_tile(n, tn_target, V7X_LANES)
    blocks = _nbytes((tm, k), BF16) + 2 * _nbytes((k, tn), BF16) + _nbytes((tm, tn), BF16)
    return pl.pallas_call(
        _swiglu_kernel,
        grid=(b, s // tm, n // tn),
        in_specs=[_a_spec(tm, k), _w_spec(k, tn), _w_spec(k, tn)],
        out_specs=_tile_spec(tm, tn),
        out_shape=jax.ShapeDtypeStruct((b, s, n), BF16),
        compiler_params=_params(blocks, 4 * _nbytes((tm, tn), F32), 3),
        name=name,
    )(a, wg, wu)


def _ep_rmsnorm(acc, e_refs, o_refs):
    g_ref, = e_refs
    y = acc * lax.rsqrt(jnp.mean(acc * acc, axis=-1, keepdims=True) + EPS) * g_ref[...]
    o_refs[0][...] = y.astype(o_refs[0].dtype)


def _mm_rmsnorm(a, w, g, name, tm_target=1024):
    b, s, k = a.shape
    n = w.shape[1]
    tm = _pick_tile(s, tm_target)
    gspec = pl.BlockSpec((1, n), lambda bi, i, j: (0, 0))
    return _mm_call([(a, _a_spec(tm, k))], [(w, _w_spec(k, n))], [(g.reshape(1, n), gspec)], _ep_rmsnorm,
                    [(jax.ShapeDtypeStruct((b, s, n), BF16), _tile_spec(tm, n))],
                    (b, s // tm, 1), tm, n, name)[0]


def _rope_pair(hi, tab):
    z = hi * tab
    return z + pltpu.roll(z, QK_ROPE_DIM, axis=1)


def _ep_krope(acc, e_refs, o_refs):
    tab_ref, = e_refs
    r = _rope_pair(acc, tab_ref[...])
    lane = lax.broadcasted_iota(jnp.int32, r.shape, 1)
    o_refs[0][...] = jnp.where(lane < QK_ROPE_DIM, r, 0.0).astype(o_refs[0].dtype)


def _mm_krope(a, w, tab, name, tm_target=1024):
    b, s, k = a.shape
    n = w.shape[1]
    tm = _pick_tile(s, tm_target)
    tspec = pl.BlockSpec((tm, n), lambda bi, i, j: (i, 0))
    return _mm_call([(a, _a_spec(tm, k))], [(w, _w_spec(k, n))], [(tab, tspec)], _ep_krope,
                    [(jax.ShapeDtypeStruct((b, s, n), BF16), _tile_spec(tm, n))],
                    (b, s // tm, 1), tm, n, name)[0]


def _ep_qrope(acc, e_refs, o_refs, *, heads):
    tab_ref, = e_refs
    o_ref = o_refs[0]
    tab = tab_ref[...]
    blk = QK_NOPE_DIM + 2 * QK_ROPE_DIM
    for h in range(heads):
        o_ref[h, :, 0:QK_NOPE_DIM] = acc[:, h * blk:h * blk + QK_NOPE_DIM].astype(o_ref.dtype)
        hi = acc[:, h * blk + QK_NOPE_DIM:(h + 1) * blk]
        o_ref[h, :, QK_NOPE_DIM:blk] = _rope_pair(hi, tab).astype(o_ref.dtype)


def _mm_qrope(a, w, tab, name, tm_target=1024, heads_per_tile=4):
    b, s, k = a.shape
    blk = QK_NOPE_DIM + 2 * QK_ROPE_DIM
    nh = w.shape[1] // blk
    hpt = math.gcd(nh, heads_per_tile)
    tn = hpt * blk
    tm = _pick_tile(s, tm_target)
    tspec = pl.BlockSpec((tm, 2 * QK_ROPE_DIM), lambda bi, i, j: (i, 0))
    ospec = pl.BlockSpec((None, hpt, tm, blk), lambda bi, i, j: (bi, j, i, 0))
    return _mm_call([(a, _a_spec(tm, k))], [(w, _w_spec(k, tn))], [(tab, tspec)],
                    functools.partial(_ep_qrope, heads=hpt),
                    [(jax.ShapeDtypeStruct((b, nh, s, blk), BF16), ospec)],
                    (b, s // tm, nh // hpt), tm, tn, name)[0]


def _ep_kv(acc, e_refs, o_refs, *, heads):
    kr_ref, = e_refs
    k_ref, v_ref = o_refs
    kr = kr_ref[...]
    blk = QK_NOPE_DIM + V_HEAD_DIM
    for h in range(heads):
        k_ref[h, :, 0:QK_NOPE_DIM] = acc[:, h * blk:h * blk + QK_NOPE_DIM].astype(k_ref.dtype)
        k_ref[h, :, QK_NOPE_DIM:QK_NOPE_DIM + kr.shape[1]] = kr
        v_ref[h] = acc[:, h * blk + QK_NOPE_DIM:(h + 1) * blk].astype(v_ref.dtype)


def _mm_kv(a, w, krot, name, tm_target=1024, heads_per_tile=4):
    b, s, k = a.shape
    blk = QK_NOPE_DIM + V_HEAD_DIM
    nh = w.shape[1] // blk
    hpt = math.gcd(nh, heads_per_tile)
    tn = hpt * blk
    tm = _pick_tile(s, tm_target)
    kd = QK_NOPE_DIM + krot.shape[2]
    krspec = pl.BlockSpec((None, tm, krot.shape[2]), lambda bi, i, j: (bi, i, 0))
    kspec = pl.BlockSpec((None, hpt, tm, kd), lambda bi, i, j: (bi, j, i, 0))
    vspec = pl.BlockSpec((None, hpt, tm, V_HEAD_DIM), lambda bi, i, j: (bi, j, i, 0))
    return _mm_call([(a, _a_spec(tm, k))], [(w, _w_spec(k, tn))], [(krot, krspec)],
                    functools.partial(_ep_kv, heads=hpt),
                    [(jax.ShapeDtypeStruct((b, nh, s, kd), BF16), kspec),
                     (jax.ShapeDtypeStruct((b, nh, s, V_HEAD_DIM), BF16), vspec)],
                    (b, s // tm, nh // hpt), tm, tn, name)


def _dft_matrix(n):
    k = jnp.arange(n, dtype=jnp.int32)
    m = jnp.arange(n, dtype=jnp.int32)
    side = 1
    while side * side < n:
        side *= 2
    if side * side != n:
        ang = (2.0 * math.pi / n) * ((k[:, None] * m[None, :]) % n).astype(F32)
        cs, sn = jnp.cos(ang), jnp.sin(ang)
    else:
        a = jnp.arange(side, dtype=jnp.int32)
        ang_x = (2.0 * math.pi / side) * ((a[:, None] * m[None, :]) % side).astype(F32)
        ang_y = (2.0 * math.pi / n) * ((a[:, None] * m[None, :]) % n).astype(F32)
        cx, sx = jnp.cos(ang_x)[:, None, :], jnp.sin(ang_x)[:, None, :]
        cy, sy = jnp.cos(ang_y)[None, :, :], jnp.sin(ang_y)[None, :, :]
        cs = (cx * cy - sx * sy).reshape(n, n)
        sn = (sx * cy + cx * sy).reshape(n, n)
    return jnp.concatenate([cs, sn], axis=0).astype(BF16)[None]


def _dft_positions(u):
    b, s, w = u.shape
    mat = _dft_matrix(s)
    tm = _pick_tile(2 * s, 1024)
    tn = _pick_tile(w, 512, V7X_LANES)
    aspec = pl.BlockSpec((None, tm, s), lambda bi, i, j: (0, i, 0))
    wspec = pl.BlockSpec((None, s, tn), lambda bi, i, j: (bi, 0, j))
    return _mm_call([(mat, aspec)], [(u, wspec)], [], _ep_store,
                    [(jax.ShapeDtypeStruct((b, 2 * s, w), BF16), _tile_spec(tm, tn))],
                    (b, 2 * s // tm, w // tn), tm, tn, "fourier_positions")[0]


def _four_channel_kernel(fr_ref, fi_ref, cs_ref, w_ref, o_ref, *, norm):
    gd = fr_ref.shape[1]
    f = jnp.dot(fr_ref[...], cs_ref[0:gd, :], preferred_element_type=F32)
    f = f + jnp.dot(fi_ref[...], cs_ref[gd:2 * gd, :], preferred_element_type=F32)
    f = (f * norm).astype(BF16)
    o_ref[...] = jnp.dot(f, w_ref[...], preferred_element_type=F32).astype(o_ref.dtype)


def _four_channels(f2, w_four):
    b, s2, w = f2.shape
    s = s2 // 2
    g, gd, _ = w_four.shape
    idx = np.arange(gd)
    ang = 2.0 * np.pi * ((idx[:, None] * idx[None, :]) % gd) / gd
    cs = jnp.asarray(np.concatenate([np.cos(ang), -np.sin(ang)], axis=0), dtype=F32).astype(BF16)
    ts = _pick_tile(s, 1024)
    nblk = s // ts
    blocks = 3 * _nbytes((ts, gd), BF16) + _nbytes((2 * gd, gd), BF16) + _nbytes((gd, gd), BF16)
    return pl.pallas_call(
        functools.partial(_four_channel_kernel, norm=1.0 / math.sqrt(s * gd)),
        grid=(b, nblk, g),
        in_specs=[
            pl.BlockSpec((None, ts, gd), lambda bi, i, j: (bi, i, j)),
            pl.BlockSpec((None, ts, gd), lambda bi, i, j: (bi, nblk + i, j)),
            pl.BlockSpec((2 * gd, gd), lambda bi, i, j: (0, 0)),
            pl.BlockSpec((None, gd, gd), lambda bi, i, j: (j, 0, 0)),
        ],
        out_specs=pl.BlockSpec((None, ts, gd), lambda bi, i, j: (bi, i, j)),
        out_shape=jax.ShapeDtypeStruct((b, s, w), BF16),
        compiler_params=_params(blocks, 3 * _nbytes((ts, gd), F32), 3),
        name="fourier_channels",
    )(f2, f2, cs, w_four.astype(BF16))


def _na_tables(rows):
    rpt = NA_ROWS_PER_TILE
    kh = min(NA_WIN_H, rows)
    wrows = min(kh + rpt - 1, rows)
    ntile = rows // rpt
    qi = np.arange(rpt * GRID_W)
    kj = np.arange(wrows * GRID_W)
    dr, qcol = qi // GRID_W, qi % GRID_W
    wrow, kcol = kj // GRID_W, kj % GRID_W
    cstart = np.clip(qcol - NA_WIN_W // 2, 0, GRID_W - NA_WIN_W)
    col_ok = (kcol[None, :] >= cstart[:, None]) & (kcol[None, :] < cstart[:, None] + NA_WIN_W)
    cidx = np.clip(kcol[None, :] - qcol[:, None] + NA_WIN_W - 1, 0, 2 * NA_WIN_W - 2)
    cidx = cidx[:GRID_W, :GRID_W]
    keys, ws_list, case_list, ridx_list, ok_list = {}, [], [], [], []
    for t in range(ntile):
        r0 = t * rpt
        ws = int(np.clip(r0 - kh // 2, 0, rows - wrows))
        r = r0 + dr
        rs = np.clip(r - kh // 2, 0, rows - kh)
        key = (r0 - ws,) + tuple((rs - ws)[::GRID_W])
        if key not in keys:
            keys[key] = len(keys)
            krow = ws + wrow
            row_ok = (krow[None, :] >= rs[:, None]) & (krow[None, :] < rs[:, None] + kh)
            rel = (ws + np.arange(wrows))[None, :] - (r0 + np.arange(rpt))[:, None] + NA_WIN_H - 1
            ridx_list.append(np.clip(rel, 0, 2 * NA_WIN_H - 2))
            ok_list.append(row_ok & col_ok)
        ws_list.append(ws)
        case_list.append(keys[key])
    return wrows, ws_list, case_list, np.stack(ridx_list), cidx, np.stack(ok_list)


def _na_bias(rpb, ridx, cidx, ok):
    nr, nc = rpb.shape[1], rpb.shape[2]
    nh = rpb.shape[0]
    rsel = jnp.asarray(np.eye(nr, dtype=np.float32)[ridx])
    csel = jnp.asarray(np.eye(nc, dtype=np.float32)[cidx])
    rows = jnp.einsum("cdwr,hrs->hcdws", rsel, rpb, precision=lax.Precision.HIGHEST)
    bias = jnp.einsum("hcdws,qks->hcdqwk", rows, csel, precision=lax.Precision.HIGHEST)
    ncase, rq, rk = ridx.shape
    bias = bias.reshape(nh, ncase, rq * cidx.shape[0], rk * cidx.shape[1])
    return jnp.where(jnp.asarray(ok)[None], bias, NEG_INF)


def _na_kernel(ws_ref, case_ref, q_ref, k_ref, v_ref, kc_ref, vc_ref, bias_ref, o_ref, *, ntile, tq, tk, scale):
    kc = kc_ref[...]
    vc = vc_ref[...]
    dn = (((1,), (1,)), ((), ()))

    def tile(t, carry):
        q0 = pl.multiple_of(t * tq, tq)
        k0 = pl.multiple_of(ws_ref[t] * GRID_W, GRID_W)
        q = q_ref[pl.ds(q0, tq), :]
        k = k_ref[pl.ds(k0, tk), :]
        v = v_ref[pl.ds(k0, tk), :]
        s_loc = lax.dot_general(q, k, dn, preferred_element_type=F32) * scale + bias_ref[case_ref[t]]
        s_ctx = lax.dot_general(q, kc, dn, preferred_element_type=F32) * scale
        m = jnp.maximum(jnp.max(s_loc, axis=-1, keepdims=True), jnp.max(s_ctx, axis=-1, keepdims=True))
        p_loc = jnp.exp(s_loc - m)
        p_ctx = jnp.exp(s_ctx - m)
        den = jnp.sum(p_loc, axis=-1, keepdims=True) + jnp.sum(p_ctx, axis=-1, keepdims=True)
        o = jnp.dot(p_loc.astype(BF16), v, preferred_element_type=F32)
        o = o + jnp.dot(p_ctx.astype(BF16), vc, preferred_element_type=F32)
        o_ref[pl.ds(q0, tq), :] = (o / den).astype(o_ref.dtype)
        return carry

    lax.fori_loop(0, ntile, tile, 0)


def _na_attention(qkv, qkvc, rpb):
    b, c3, s, d = qkv.shape
    nh = c3 // 3
    lc = qkvc.shape[2]
    rows = s // GRID_W
    wrows, ws_list, case_list, ridx, cidx, ok = _na_tables(rows)
    bias = _na_bias(rpb, ridx, cidx, ok)
    ncase = bias.shape[1]
    tq = NA_ROWS_PER_TILE * GRID_W
    tk = wrows * GRID_W
    ntile = rows // NA_ROWS_PER_TILE
    ws_arr = jnp.asarray(np.array(ws_list, np.int32))
    case_arr = jnp.asarray(np.array(case_list, np.int32))
    blocks = 3 * _nbytes((s, d), BF16) + 2 * _nbytes((lc, d), BF16) + _nbytes((ncase, tq, tk), F32) \
        + _nbytes((s, d), BF16)
    grid_spec = pltpu.PrefetchScalarGridSpec(
        num_scalar_prefetch=2,
        grid=(nh, b),
        in_specs=[
            pl.BlockSpec((None, None, s, d), lambda h, bi, *_: (bi, h, 0, 0)),
            pl.BlockSpec((None, None, s, d), lambda h, bi, *_: (bi, nh + h, 0, 0)),
            pl.BlockSpec((None, None, s, d), lambda h, bi, *_: (bi, 2 * nh + h, 0, 0)),
            pl.BlockSpec((None, None, lc, d), lambda h, bi, *_: (bi, nh + h, 0, 0)),
            pl.BlockSpec((None, None, lc, d), lambda h, bi, *_: (bi, 2 * nh + h, 0, 0)),
            pl.BlockSpec((None, ncase, tq, tk), lambda h, bi, *_: (h, 0, 0, 0)),
        ],
        out_specs=pl.BlockSpec((None, s, d), lambda h, bi, *_: (bi, 0, h)),
    )
    return pl.pallas_call(
        functools.partial(_na_kernel, ntile=ntile, tq=tq, tk=tk, scale=d ** -0.5),
        grid_spec=grid_spec,
        out_shape=jax.ShapeDtypeStruct((b, s, nh * d), BF16),
        compiler_params=_params(blocks, 8 * _nbytes((tq, tk + lc), F32), 2),
        name="neighbourhood_attention",
    )(ws_arr, case_arr, qkv, qkv, qkv, qkvc, qkvc, bias)


def _dense_attn_kernel(q_ref, k_ref, v_ref, o_ref, *, scale):
    s = lax.dot_general(q_ref[...], k_ref[...], (((1,), (1,)), ((), ())), preferred_element_type=F32) * scale
    m = jnp.max(s, axis=-1, keepdims=True)
    p = jnp.exp(s - m)
    den = jnp.sum(p, axis=-1, keepdims=True)
    o = jnp.dot(p.astype(BF16), v_ref[...], preferred_element_type=F32)
    o_ref[...] = (o / den).astype(o_ref.dtype)


def _dense_attention(q, k, v, nh, q_off, k_off, v_off, scale, name, tq_target=512):
    b, _, sq, dk = q.shape
    sk = k.shape[2]
    dv = v.shape[3]
    tq = _pick_tile(sq, tq_target)
    blocks = _nbytes((tq, dk), BF16) + _nbytes((sk, dk), BF16) + _nbytes((sk, dv), BF16) + _nbytes((tq, dv), BF16)
    return pl.pallas_call(
        functools.partial(_dense_attn_kernel, scale=scale),
        grid=(b, nh, sq // tq),
        in_specs=[
            pl.BlockSpec((None, None, tq, dk), lambda bi, h, i: (bi, q_off + h, i, 0)),
            pl.BlockSpec((None, None, sk, dk), lambda bi, h, i: (bi, k_off + h, 0, 0)),
            pl.BlockSpec((None, None, sk, dv), lambda bi, h, i: (bi, v_off + h, 0, 0)),
        ],
        out_specs=pl.BlockSpec((None, tq, dv), lambda bi, h, i: (bi, i, h)),
        out_shape=jax.ShapeDtypeStruct((b, sq, nh * dv), BF16),
        compiler_params=_params(blocks, 3 * _nbytes((tq, sk), F32), 3),
        name=name,
    )(q, k, v)


def _pool_kernel(u_ref, w_ref, sc_ref, o_ref, *, windows):
    n, gd = u_ref.shape[0], w_ref.shape[1]
    t = lax.broadcasted_iota(jnp.int32, (n, gd), 0)
    for g, window in enumerate(windows):
        cols = slice(g * gd, (g + 1) * gd)
        u = u_ref[:, cols]
        half = window // 2
        ssum = u
        for j in range(1, half + 1):
            ssum = ssum + jnp.where(t >= j, pltpu.roll(u, j, axis=0), 0.0)
        for j in range(1, half):
            ssum = ssum + jnp.where(t < n - j, pltpu.roll(u, n - j, axis=0), 0.0)
        cnt = jnp.minimum(t + half, n) - jnp.maximum(t - half, 0)
        p = ssum / cnt.astype(F32) - u
        y = jnp.dot(p.astype(BF16), w_ref[g], preferred_element_type=F32)
        o_ref[:, cols] = (y * sc_ref[:, cols]).astype(o_ref.dtype)


def _pool_mix(u, w_pool, pool_scale):
    b, s, w = u.shape
    g, gd, _ = w_pool.shape
    blocks = _nbytes((s, w), F32) + _nbytes((g, gd, gd), BF16) + _nbytes((s, w), BF16)
    return pl.pallas_call(
        functools.partial(_pool_kernel, windows=POOL_WINDOWS[:g]),
        grid=(b,),
        in_specs=[
            pl.BlockSpec((None, s, w), lambda bi: (bi, 0, 0)),
            pl.BlockSpec((g, gd, gd), lambda bi: (0, 0, 0)),
            pl.BlockSpec((1, w), lambda bi: (0, 0)),
        ],
        out_specs=pl.BlockSpec((None, s, w), lambda bi: (bi, 0, 0)),
        out_shape=jax.ShapeDtypeStruct((b, s, w), BF16),
        compiler_params=_params(blocks, 6 * _nbytes((s, gd), F32), 1),
        name="pool_mix",
    )(u, w_pool.astype(BF16), pool_scale.reshape(1, w))


def _rope_table(n_latent, n_plain):
    half = QK_ROPE_DIM // 2
    nf = half // 2
    t = jnp.arange(n_latent)
    inv = ROPE_BASE ** (-jnp.arange(nf, dtype=F32) / nf)
    ar = (t // GRID_W).astype(F32)[:, None] * inv[None, :]
    ac = (t % GRID_W).astype(F32)[:, None] * inv[None, :]
    cr, sr, cc, sc = jnp.cos(ar), jnp.sin(ar), jnp.cos(ac), jnp.sin(ac)
    tab = jnp.concatenate([cr, cr, cc, cc, -sr, sr, -sc, sc], axis=1)
    if n_plain:
        plain = jnp.concatenate([jnp.ones((n_plain, QK_ROPE_DIM), F32), jnp.zeros((n_plain, QK_ROPE_DIM), F32)], axis=1)
        tab = jnp.concatenate([tab, plain], axis=0)
    return tab


def _swap_rope_cols(w):
    nf = QK_ROPE_DIM // 4
    lead = w.shape[:-1]
    return jnp.flip(w.reshape(lead + (2, 2, nf)), axis=-2).reshape(lead + (QK_ROPE_DIM,))


def _even_mixer(h, hc, w_in, w_four, rpb, with_ctx_out):
    fw = w_four.shape[0] * w_four.shape[1]
    nh = rpb.shape[0]
    w_in = w_in.astype(BF16)
    w_u, w_qkv = w_in[:, :fw], w_in[:, fw:]
    u = _mm_plain(h, w_u, BF16, "even_in_fourier")
    qkv = _mm_chunked(h, w_qkv, "even_in_qkv")
    qkvc = _mm_chunked(hc, w_qkv, "even_in_qkv_ctx")
    y_f = _four_channels(_dft_positions(u), w_four)
    y_a = _na_attention(qkv, qkvc, rpb)
    ctx_ops = None
    if with_ctx_out:
        uc = _mm_plain(hc, w_u, BF16, "even_in_fourier_ctx")
        yc_f = _four_channels(_dft_positions(uc), w_four)
        yc_a = _dense_attention(qkvc, qkvc, qkvc, nh, 0, nh, 2 * nh, HEAD_DIM ** -0.5, "ctx_attention_even")
        ctx_ops = [yc_f, yc_a]
    return [y_f, y_a], ctx_ops


def _odd_mixer(h, hc, w_in, w_pool, pool_scale, g_q, g_kv, w_uq, w_ukv, with_ctx_out):
    b, s, d = h.shape
    lc = hc.shape[1]
    pw = w_pool.shape[0] * w_pool.shape[1]
    q_rank, kv_rank = g_q.shape[0], g_kv.shape[0]
    nh = w_uq.shape[1] // (QK_NOPE_DIM + QK_ROPE_DIM)
    o_q, o_kv, o_kr = pw, pw + q_rank, pw + q_rank + kv_rank
    w_in = w_in.astype(BF16)
    w_pool_in, w_q_in, w_kv_in, w_kr = w_in[:, :o_q], w_in[:, o_q:o_kv], w_in[:, o_kv:o_kr], w_in[:, o_kr:]
    w_kr2 = jnp.concatenate([w_kr, _swap_rope_cols(w_kr)], axis=1)
    wq = w_uq.astype(BF16).reshape(q_rank, nh, QK_NOPE_DIM + QK_ROPE_DIM)
    wq_r = wq[..., QK_NOPE_DIM:]
    wq2 = jnp.concatenate([wq[..., :QK_NOPE_DIM], wq_r, _swap_rope_cols(wq_r)], axis=-1).reshape(q_rank, -1)
    w_ukv = w_ukv.astype(BF16)

    hk = jnp.concatenate([h, hc], axis=1)
    tab_k = _rope_table(s, lc)
    tab_q = tab_k[:s]
    ckv_n = _mm_rmsnorm(hk, w_kv_in, g_kv, "odd_in_ckv")
    k_rot = _mm_krope(hk, w_kr2, tab_k, "odd_in_krope")
    k_all, v_all = _mm_kv(ckv_n, w_ukv, k_rot, "odd_up_kv")
    cq_n = _mm_rmsnorm(h, w_q_in, g_q, "odd_in_cq")
    q = _mm_qrope(cq_n, wq2, tab_q, "odd_up_q")
    scale = (QK_NOPE_DIM + QK_ROPE_DIM) ** -0.5
    attn = _dense_attention(q, k_all, v_all, nh, 0, 0, 0, scale, "mla_attention")
    u = _mm_plain(h, w_pool_in, F32, "odd_in_pool")
    pooled = _pool_mix(u, w_pool, pool_scale)
    ctx_ops = None
    if with_ctx_out:
        tab_c = _rope_table(0, lc)
        cqc_n = _mm_rmsnorm(hc, w_q_in, g_q, "odd_in_cq_ctx")
        qc = _mm_qrope(cqc_n, wq2, tab_c, "odd_up_q_ctx")
        kc = k_all[:, :, s:]
        vc = v_all[:, :, s:]
        attn_c = _dense_attention(qc, kc, vc, nh, 0, 0, 0, scale, "ctx_attention_odd")
        uc = _mm_plain(hc, w_pool_in, F32, "odd_in_pool_ctx")
        ctx_ops = [_pool_mix(uc, w_pool, pool_scale), attn_c]
    return [pooled, attn], ctx_ops


def _split_rows(w, ops):
    out, off = [], 0
    for a in ops:
        out.append(w[off:off + a.shape[2]])
        off += a.shape[2]
    return out


def _ffn(xs, norm_g, shift, scale, gate, wg, wu, wd, tag):
    h2 = _norm_call(xs, norm_g, shift, scale)
    act = _swiglu_up(h2, wg, wu, "ffn_up_" + tag)
    return _mm_residual([act], [wd], xs, gate, "ffn_down_" + tag, tm_target=512)


def kernel(x, c, ctx, c_ctx, w_mod, b_mod, norm1_g, norm2_g, w_in_ab, w_four, na_rpb, w_out_ab, w_in_cd, w_pool,
           pool_scale, mla_gq, mla_gkv, w_uq, w_ukv, w_out_cd, w_ffn_gate, w_ffn_up, w_ffn_down, final_g):
    b, s, d = x.shape
    depth = w_mod.shape[0]
    nrow = -(-(b + 1) // 8) * 8
    cvec = jnp.concatenate([c, c_ctx[None, :], jnp.zeros((nrow - b - 1, d), F32)], axis=0)
    mods = _mod_call(cvec, w_mod, b_mod)
    xc = ctx
    for l in range(depth):
        last = l == depth - 1
        mod = mods[l, :b].reshape(b, 1, 6 * d)
        modc = jnp.broadcast_to(mods[l, b].reshape(1, 1, 6 * d), (b, 1, 6 * d))
        sh1, sc1, g1, sh2, sc2, g2 = jnp.split(mod, 6, axis=-1)
        csh1, csc1, cg1, csh2, csc2, cg2 = jnp.split(modc, 6, axis=-1)
        h = _norm_call(x, norm1_g[l], sh1, sc1)
        hc = _norm_call(xc, norm1_g[l], csh1, csc1)
        i = l // 2
        if l % 2 == 0:
            ops, ctx_ops = _even_mixer(h, hc, w_in_ab[i], w_four[i], na_rpb[i], not last)
            w_out = w_out_ab[i].astype(BF16)
        else:
            ops, ctx_ops = _odd_mixer(h, hc, w_in_cd[i], w_pool[i], pool_scale[i], mla_gq[i], mla_gkv[i],
                                      w_uq[i], w_ukv[i], not last)
            w_out = w_out_cd[i].astype(BF16)
        wg, wu, wd = w_ffn_gate[l].astype(BF16), w_ffn_up[l].astype(BF16), w_ffn_down[l].astype(BF16)
        x = _mm_residual(ops, _split_rows(w_out, ops), x, g1, f"out_proj_{l}")
        x = _ffn(x, norm2_g[l], sh2, sc2, g2, wg, wu, wd, f"{l}")
        if not last:
            xc = _mm_residual(ctx_ops, _split_rows(w_out, ctx_ops), xc, cg1, f"out_proj_ctx_{l}")
            xc = _ffn(xc, norm2_g[l], csh2, csc2, cg2, wg, wu, wd, f"ctx_{l}")
    return _norm_call(x, final_g, out_dtype=F32)
```

```python
import functools
import math

import numpy as np
import jax
import jax.numpy as jnp
from jax import lax
from jax.experimental import pallas as pl
from jax.experimental.pallas import tpu as pltpu

GRID_W = 64
HEAD_DIM = 128
EPS = 1e-6
NEG_INF = -1e30
NA_WIN_H = 8
NA_WIN_W = 16
POOL_WINDOWS = (2, 4, 8, 16)
QK_NOPE_DIM = 128
QK_ROPE_DIM = 64
V_HEAD_DIM = 128
ROPE_BASE = 10000.0
LOG2E = 1.4426950408889634

V7X_LANES = 128
V7X_VMEM_BYTES = 64 * 1024 * 1024
V7X_VMEM_CAP = 56 * 1024 * 1024
NA_ROWS_PER_TILE = 4

BF16 = jnp.bfloat16
F32 = jnp.float32


def _pick_tile(n, target, mult=16):
    if n <= target:
        return n
    for t in range(target, 0, -1):
        if n % t == 0 and t % mult == 0:
            return t
    return n


def _nbytes(shape, dtype):
    return int(np.prod(shape)) * jnp.dtype(dtype).itemsize


def _params(block_bytes, scratch_bytes=0, ndims=3):
    est = 2 * block_bytes + scratch_bytes
    limit = int(min(max(est * 5 // 4 + (4 << 20), 16 << 20), V7X_VMEM_CAP))
    return pltpu.CompilerParams(dimension_semantics=("arbitrary",) * ndims, vmem_limit_bytes=limit)


def _mod_kernel(c_ref, w_ref, b_ref, o_ref):
    cv = c_ref[...]
    s = cv * (1.0 / (1.0 + jnp.exp(-cv)))
    acc = jnp.dot(s.astype(BF16), w_ref[...].astype(BF16), preferred_element_type=F32)
    o_ref[...] = acc + b_ref[...]


def _mod_call(cvec, w_mod, b_mod):
    depth, d, n = w_mod.shape
    r = cvec.shape[0]
    tn = _pick_tile(n, 1024, V7X_LANES)
    blocks = _nbytes((r, d), F32) + _nbytes((d, tn), F32) + 2 * _nbytes((r, tn), F32)
    return pl.pallas_call(
        _mod_kernel,
        grid=(depth, n // tn),
        in_specs=[
            pl.BlockSpec((r, d), lambda l, j: (0, 0)),
            pl.BlockSpec((None, d, tn), lambda l, j: (l, 0, j)),
            pl.BlockSpec((None, 1, tn), lambda l, j: (l, 0, j)),
        ],
        out_specs=pl.BlockSpec((None, r, tn), lambda l, j: (l, 0, j)),
        out_shape=jax.ShapeDtypeStruct((depth, r, n), F32),
        compiler_params=_params(blocks, _nbytes((d, tn), BF16), 2),
        name="mod_vectors",
    )(cvec, w_mod, b_mod.reshape(depth, 1, n))


def _norm_mod_kernel(x_ref, g_ref, sh_ref, sc_ref, o_ref):
    xf = x_ref[...]
    y = xf * lax.rsqrt(jnp.mean(xf * xf, axis=-1, keepdims=True) + EPS) * g_ref[...]
    o_ref[...] = (y * (1.0 + sc_ref[...]) + sh_ref[...]).astype(o_ref.dtype)


def _norm_kernel(x_ref, g_ref, o_ref):
    xf = x_ref[...]
    y = xf * lax.rsqrt(jnp.mean(xf * xf, axis=-1, keepdims=True) + EPS) * g_ref[...]
    o_ref[...] = y.astype(o_ref.dtype)


def _norm_call(x, g, shift=None, scale=None, out_dtype=BF16):
    b, s, d = x.shape
    ts = _pick_tile(s, 512)
    xspec = pl.BlockSpec((None, ts, d), lambda bi, i: (bi, i, 0))
    gspec = pl.BlockSpec((1, d), lambda bi, i: (0, 0))
    mspec = pl.BlockSpec((None, 1, d), lambda bi, i: (bi, 0, 0))
    blocks = _nbytes((ts, d), F32) + _nbytes((ts, d), out_dtype) + 3 * _nbytes((1, d), F32)
    common = dict(
        grid=(b, s // ts),
        out_specs=xspec,
        out_shape=jax.ShapeDtypeStruct((b, s, d), out_dtype),
        compiler_params=_params(blocks, 2 * _nbytes((ts, d), F32), 2),
    )
    if shift is None:
        return pl.pallas_call(_norm_kernel, in_specs=[xspec, gspec], name="rmsnorm", **common)(
            x, g.reshape(1, d))
    return pl.pallas_call(_norm_mod_kernel, in_specs=[xspec, gspec, mspec, mspec], name="rmsnorm_modulate",
                          **common)(x, g.reshape(1, d), shift, scale)


def _mm_kernel(*refs, n_ops, n_extra, epilogue):
    a_refs = refs[:n_ops]
    w_refs = refs[n_ops:2 * n_ops]
    e_refs = refs[2 * n_ops:2 * n_ops + n_extra]
    o_refs = refs[2 * n_ops + n_extra:]
    acc = None
    for a_ref, w_ref in zip(a_refs, w_refs):
        part = jnp.dot(a_ref[...], w_ref[...], preferred_element_type=F32)
        acc = part if acc is None else acc + part
    epilogue(acc, e_refs, o_refs)


def _mm_call(a_ops, w_ops, extras, epilogue, outs, grid, tm, tn, name, tmp_bytes=0):
    arrays = [a for a, _ in a_ops] + [w for w, _ in w_ops] + [e for e, _ in extras]
    in_specs = [s for _, s in a_ops] + [s for _, s in w_ops] + [s for _, s in extras]
    blocks = 0
    for arr, spec in a_ops + w_ops + extras:
        blocks += _nbytes([d for d in spec.block_shape if d is not None], arr.dtype)
    for sds, spec in outs:
        blocks += _nbytes([d for d in spec.block_shape if d is not None], sds.dtype)
    kern = functools.partial(_mm_kernel, n_ops=len(a_ops), n_extra=len(extras), epilogue=epilogue)
    res = pl.pallas_call(
        kern,
        grid=grid,
        in_specs=in_specs,
        out_specs=[s for _, s in outs],
        out_shape=[o for o, _ in outs],
        compiler_params=_params(blocks, 3 * _nbytes((tm, tn), F32) + tmp_bytes, len(grid)),
        name=name,
    )(*arrays)
    return res


def _a_spec(tm, k, col_blk=0):
    return pl.BlockSpec((None, tm, k), lambda b, i, j: (b, i, col_blk))


def _w3(w):
    return w if w.ndim == 3 else w[None]


def _w_spec(k, tn, layer=0, row_blk=0, col_off=0):
    return pl.BlockSpec((None, k, tn), lambda b, i, j: (layer, row_blk, j + col_off))


def _tile_spec(tm, tn):
    return pl.BlockSpec((None, tm, tn), lambda b, i, j: (b, i, j))


def _vec_spec(tn):
    return pl.BlockSpec((None, 1, tn), lambda b, i, j: (b, 0, j))


def _ep_store(acc, e_refs, o_refs):
    o_refs[0][...] = acc.astype(o_refs[0].dtype)


def _col_tile(n, col0, target, mult):
    return _pick_tile(math.gcd(n, col0) if col0 else n, target, mult)


def _mm_plain(a, w, out_dtype, name, layer=0, col0=0, n=None, tm_target=1024, tn_target=512):
    b, s, k = a.shape
    w = _w3(w)
    n = w.shape[2] - col0 if n is None else n
    tm = _pick_tile(s, tm_target)
    tn = _col_tile(n, col0, tn_target, V7X_LANES)
    return _mm_call([(a, _a_spec(tm, k))], [(w, _w_spec(k, tn, layer, 0, col0 // tn))], [], _ep_store,
                    [(jax.ShapeDtypeStruct((b, s, n), out_dtype), _tile_spec(tm, tn))],
                    (b, s // tm, n // tn), tm, tn, name)[0]


def _ep_chunks(acc, e_refs, o_refs, *, width):
    o_ref = o_refs[0]
    for c in range(acc.shape[1] // width):
        o_ref[c] = acc[:, c * width:(c + 1) * width].astype(o_ref.dtype)


def _mm_chunked(a, w, name, layer=0, col0=0, n=None, width=HEAD_DIM, tm_target=1024, tn_target=512):
    b, s, k = a.shape
    w = _w3(w)
    n = w.shape[2] - col0 if n is None else n
    tm = _pick_tile(s, tm_target)
    tn = _col_tile(n, col0, tn_target, width)
    cpt = tn // width
    ospec = pl.BlockSpec((None, cpt, tm, width), lambda bi, i, j: (bi, j, i, 0))
    return _mm_call([(a, _a_spec(tm, k))], [(w, _w_spec(k, tn, layer, 0, col0 // tn))], [],
                    functools.partial(_ep_chunks, width=width),
                    [(jax.ShapeDtypeStruct((b, n // width, s, width), BF16), ospec)],
                    (b, s // tm, n // tn), tm, tn, name)[0]


def _ep_residual(acc, e_refs, o_refs):
    x_ref, g_ref = e_refs
    o_refs[0][...] = x_ref[...] + g_ref[...] * acc


def _mm_residual(a_list, w, x, gate, name, layer=0, tm_target=1024, tn_target=512):
    b, s, n = x.shape
    w = _w3(w)
    tm = _pick_tile(s, tm_target)
    tn = _pick_tile(n, tn_target, V7X_LANES)
    kblk = functools.reduce(math.gcd, [a.shape[2] for a in a_list])
    a_ops, w_ops = [], []
    for a in a_list:
        for cb in range(a.shape[2] // kblk):
            a_ops.append((a, _a_spec(tm, kblk, cb)))
            w_ops.append((w, _w_spec(kblk, tn, layer, len(w_ops))))
    extras = [(x, _tile_spec(tm, tn)), (gate, _vec_spec(tn))]
    return _mm_call(a_ops, w_ops, extras, _ep_residual,
                    [(jax.ShapeDtypeStruct((b, s, n), F32), _tile_spec(tm, tn))],
                    (b, s // tm, n // tn), tm, tn, name)[0]


def _swiglu_kernel(a_ref, wg_ref, wu_ref, o_ref):
    a = a_ref[...]
    g = jnp.dot(a, wg_ref[...], preferred_element_type=F32)
    u = jnp.dot(a, wu_ref[...], preferred_element_type=F32)
    o_ref[...] = (g * (1.0 / (1.0 + jnp.exp(-g))) * u).astype(o_ref.dtype)


def _swiglu_up(a, wg, wu, layer, name, tm_target=1024, tn_target=512):
    b, s, k = a.shape
    n = wg.shape[2]
    tm = _pick_tile(s, tm_target)
    tn = _pick_tile(n, tn_target, V7X_LANES)
    blocks = _nbytes((tm, k), BF16) + 2 * _nbytes((k, tn), BF16) + _nbytes((tm, tn), BF16)
    return pl.pallas_call(
        _swiglu_kernel,
        grid=(b, s // tm, n // tn),
        in_specs=[_a_spec(tm, k), _w_spec(k, tn, layer), _w_spec(k, tn, layer)],
        out_specs=_tile_spec(tm, tn),
        out_shape=jax.ShapeDtypeStruct((b, s, n), BF16),
        compiler_params=_params(blocks, 4 * _nbytes((tm, tn), F32), 3),
        name=name,
    )(a, wg, wu)


def _ep_rmsnorm(acc, e_refs, o_refs):
    g_ref, = e_refs
    y = acc * lax.rsqrt(jnp.mean(acc * acc, axis=-1, keepdims=True) + EPS) * g_ref[...]
    o_refs[0][...] = y.astype(o_refs[0].dtype)


def _mm_rmsnorm(a, w, g, name, tm_target=1024):
    b, s, k = a.shape
    n = w.shape[1]
    tm = _pick_tile(s, tm_target)
    gspec = pl.BlockSpec((1, n), lambda bi, i, j: (0, 0))
    return _mm_call([(a, _a_spec(tm, k))], [(_w3(w), _w_spec(k, n))], [(g.reshape(1, n), gspec)], _ep_rmsnorm,
                    [(jax.ShapeDtypeStruct((b, s, n), BF16), _tile_spec(tm, n))],
                    (b, s // tm, 1), tm, n, name)[0]


def _rope_pair(hi, tab):
    z = hi * tab
    return z + pltpu.roll(z, QK_ROPE_DIM, axis=1)


def _ep_krope(acc, e_refs, o_refs):
    tab_ref, = e_refs
    r = _rope_pair(acc, tab_ref[...])
    lane = lax.broadcasted_iota(jnp.int32, r.shape, 1)
    o_refs[0][...] = jnp.where(lane < QK_ROPE_DIM, r, 0.0).astype(o_refs[0].dtype)


def _mm_krope(a, w, tab, name, tm_target=1024):
    b, s, k = a.shape
    n = w.shape[1]
    tm = _pick_tile(s, tm_target)
    tspec = pl.BlockSpec((tm, n), lambda bi, i, j: (i, 0))
    return _mm_call([(a, _a_spec(tm, k))], [(_w3(w), _w_spec(k, n))], [(tab, tspec)], _ep_krope,
                    [(jax.ShapeDtypeStruct((b, s, n), BF16), _tile_spec(tm, n))],
                    (b, s // tm, 1), tm, n, name)[0]


def _ep_qrope(acc, e_refs, o_refs, *, heads):
    tab_ref, = e_refs
    o_ref = o_refs[0]
    tab = tab_ref[...]
    blk = QK_NOPE_DIM + 2 * QK_ROPE_DIM
    for h in range(heads):
        o_ref[h, :, 0:QK_NOPE_DIM] = acc[:, h * blk:h * blk + QK_NOPE_DIM].astype(o_ref.dtype)
        hi = acc[:, h * blk + QK_NOPE_DIM:(h + 1) * blk]
        o_ref[h, :, QK_NOPE_DIM:blk] = _rope_pair(hi, tab).astype(o_ref.dtype)


def _mm_qrope(a, w, tab, name, tm_target=1024, heads_per_tile=4):
    b, s, k = a.shape
    blk = QK_NOPE_DIM + 2 * QK_ROPE_DIM
    nh = w.shape[1] // blk
    hpt = math.gcd(nh, heads_per_tile)
    tn = hpt * blk
    tm = _pick_tile(s, tm_target)
    tspec = pl.BlockSpec((tm, 2 * QK_ROPE_DIM), lambda bi, i, j: (i, 0))
    ospec = pl.BlockSpec((None, hpt, tm, blk), lambda bi, i, j: (bi, j, i, 0))
    return _mm_call([(a, _a_spec(tm, k))], [(_w3(w), _w_spec(k, tn))], [(tab, tspec)],
                    functools.partial(_ep_qrope, heads=hpt),
                    [(jax.ShapeDtypeStruct((b, nh, s, blk), BF16), ospec)],
                    (b, s // tm, nh // hpt), tm, tn, name)[0]


def _ep_kv(acc, e_refs, o_refs, *, heads):
    kr_ref, = e_refs
    k_ref, v_ref = o_refs
    kr = kr_ref[...]
    blk = QK_NOPE_DIM + V_HEAD_DIM
    for h in range(heads):
        k_ref[h, :, 0:QK_NOPE_DIM] = acc[:, h * blk:h * blk + QK_NOPE_DIM].astype(k_ref.dtype)
        k_ref[h, :, QK_NOPE_DIM:QK_NOPE_DIM + kr.shape[1]] = kr
        v_ref[h] = acc[:, h * blk + QK_NOPE_DIM:(h + 1) * blk].astype(v_ref.dtype)


def _mm_kv(a, w, krot, name, tm_target=1024, heads_per_tile=4):
    b, s, k = a.shape
    blk = QK_NOPE_DIM + V_HEAD_DIM
    nh = w.shape[1] // blk
    hpt = math.gcd(nh, heads_per_tile)
    tn = hpt * blk
    tm = _pick_tile(s, tm_target)
    kd = QK_NOPE_DIM + krot.shape[2]
    krspec = pl.BlockSpec((None, tm, krot.shape[2]), lambda bi, i, j: (bi, i, 0))
    kspec = pl.BlockSpec((None, hpt, tm, kd), lambda bi, i, j: (bi, j, i, 0))
    vspec = pl.BlockSpec((None, hpt, tm, V_HEAD_DIM), lambda bi, i, j: (bi, j, i, 0))
    return _mm_call([(a, _a_spec(tm, k))], [(_w3(w), _w_spec(k, tn))], [(krot, krspec)],
                    functools.partial(_ep_kv, heads=hpt),
                    [(jax.ShapeDtypeStruct((b, nh, s, kd), BF16), kspec),
                     (jax.ShapeDtypeStruct((b, nh, s, V_HEAD_DIM), BF16), vspec)],
                    (b, s // tm, nh // hpt), tm, tn, name)


def _dft_matrix(n):
    m = jnp.arange(n, dtype=jnp.int32)
    side = 1
    while side * side < n:
        side *= 2
    if side * side != n:
        k = jnp.arange(n, dtype=jnp.int32)
        ang = (2.0 * math.pi / n) * ((k[:, None] * m[None, :]) % n).astype(F32)
        return jnp.concatenate([jnp.cos(ang), jnp.sin(ang)], axis=0).astype(BF16)[None]
    a = jnp.arange(side, dtype=jnp.int32)
    ang_x = (2.0 * math.pi / side) * ((a[:, None] * m[None, :]) % side).astype(F32)
    ang_y = (2.0 * math.pi / n) * ((a[:, None] * m[None, :]) % n).astype(F32)
    cx, sx = jnp.cos(ang_x), jnp.sin(ang_x)
    cy, sy = jnp.cos(ang_y)[None], jnp.sin(ang_y)[None]
    p = jnp.concatenate([cx, sx], axis=0)[:, None, :]
    q = jnp.concatenate([sx, -cx], axis=0)[:, None, :]
    return (p * cy - q * sy).astype(BF16).reshape(1, 2 * n, n)


def _dft_positions(u):
    b, s, w = u.shape
    mat = _dft_matrix(s)
    tm = _pick_tile(2 * s, 1024)
    tn = _pick_tile(w, 512, V7X_LANES)
    aspec = pl.BlockSpec((None, tm, s), lambda bi, i, j: (0, i, 0))
    wspec = pl.BlockSpec((None, s, tn), lambda bi, i, j: (bi, 0, j))
    return _mm_call([(mat, aspec)], [(u, wspec)], [], _ep_store,
                    [(jax.ShapeDtypeStruct((b, 2 * s, w), BF16), _tile_spec(tm, tn))],
                    (b, 2 * s // tm, w // tn), tm, tn, "fourier_positions")[0]


def _four_channel_kernel(fr_ref, fi_ref, cs_ref, w_ref, o_ref, *, norm):
    gd = fr_ref.shape[1]
    f = jnp.dot(fr_ref[...], cs_ref[0:gd, :], preferred_element_type=F32)
    f = f + jnp.dot(fi_ref[...], cs_ref[gd:2 * gd, :], preferred_element_type=F32)
    f = (f * norm).astype(BF16)
    o_ref[...] = jnp.dot(f, w_ref[...], preferred_element_type=F32).astype(o_ref.dtype)


def _four_channels(f2, w_four):
    b, s2, w = f2.shape
    s = s2 // 2
    g, gd, _ = w_four.shape
    idx = np.arange(gd)
    ang = 2.0 * np.pi * ((idx[:, None] * idx[None, :]) % gd) / gd
    cs = jnp.asarray(np.concatenate([np.cos(ang), -np.sin(ang)], axis=0), dtype=F32).astype(BF16)
    ts = _pick_tile(s, 1024)
    nblk = s // ts
    blocks = 3 * _nbytes((ts, gd), BF16) + _nbytes((2 * gd, gd), BF16) + _nbytes((gd, gd), BF16)
    return pl.pallas_call(
        functools.partial(_four_channel_kernel, norm=1.0 / math.sqrt(s * gd)),
        grid=(b, nblk, g),
        in_specs=[
            pl.BlockSpec((None, ts, gd), lambda bi, i, j: (bi, i, j)),
            pl.BlockSpec((None, ts, gd), lambda bi, i, j: (bi, nblk + i, j)),
            pl.BlockSpec((2 * gd, gd), lambda bi, i, j: (0, 0)),
            pl.BlockSpec((None, gd, gd), lambda bi, i, j: (j, 0, 0)),
        ],
        out_specs=pl.BlockSpec((None, ts, gd), lambda bi, i, j: (bi, i, j)),
        out_shape=jax.ShapeDtypeStruct((b, s, w), BF16),
        compiler_params=_params(blocks, 3 * _nbytes((ts, gd), F32), 3),
        name="fourier_channels",
    )(f2, f2, cs, w_four.astype(BF16))


def _na_plan(rows):
    rpt = NA_ROWS_PER_TILE
    kh = min(NA_WIN_H, rows)
    wrows = min(kh + rpt - 1, rows)
    cases, ws_list, case_list = {}, [], []
    for t in range(rows // rpt):
        r0 = t * rpt
        ws = int(np.clip(r0 - kh // 2, 0, rows - wrows))
        plan = []
        for r in range(r0, r0 + rpt):
            rs = int(np.clip(r - kh // 2, 0, rows - kh))
            plan.append(tuple((ws + w) - r + NA_WIN_H - 1 if rs <= ws + w < rs + kh else None
                              for w in range(wrows)))
        plan = tuple(plan)
        cases.setdefault(plan, len(cases))
        ws_list.append(ws)
        case_list.append(cases[plan])
    return wrows, ws_list, case_list, list(cases)


def _na_block_table(rpb):
    qcol = np.arange(GRID_W)[:, None]
    kcol = np.arange(GRID_W)[None, :]
    cstart = np.clip(qcol - NA_WIN_W // 2, 0, GRID_W - NA_WIN_W)
    col_ok = (kcol >= cstart) & (kcol < cstart + NA_WIN_W)
    cidx = np.clip(kcol - qcol + NA_WIN_W - 1, 0, 2 * NA_WIN_W - 2)
    csel = jnp.asarray(np.eye(rpb.shape[2], dtype=np.float32)[cidx])
    blk = jnp.einsum("hrs,qks->hrqk", rpb, csel, precision=lax.Precision.HIGHEST)
    blk = jnp.where(jnp.asarray(col_ok), blk * LOG2E, NEG_INF)
    return jnp.concatenate([blk, blk], axis=-1)


def _ones_column(rows, width):
    lane = lax.broadcasted_iota(jnp.int32, (rows, width), 1)
    return jnp.where(lane == 0, 1.0, 0.0).astype(BF16)


def _na_kernel(ws_ref, case_ref, q_ref, k_ref, v_ref, kc_ref, vc_ref, blk_ref, o_ref, bias_ref, vx_ref, vcx_ref,
               *, ntile, tq, tk, scale, plans):
    d = v_ref.shape[1]

    @pl.when(pl.program_id(1) == 0)
    def _():
        for c, plan in enumerate(plans):
            for qr, row in enumerate(plan):
                for kr, ridx in enumerate(row):
                    half = slice((kr % 2) * GRID_W, (kr % 2 + 1) * GRID_W)
                    if ridx is None:
                        blk = jnp.full((GRID_W, GRID_W), NEG_INF, F32)
                    else:
                        blk = blk_ref[ridx][:, half]
                    bias_ref[c, qr * GRID_W:(qr + 1) * GRID_W, kr * GRID_W:(kr + 1) * GRID_W] = blk

    vx_ref[:, 0:d] = v_ref[...]
    vx_ref[:, d:2 * d] = _ones_column(v_ref.shape[0], d)
    vcx_ref[:, 0:d] = vc_ref[...]
    vcx_ref[:, d:2 * d] = _ones_column(vc_ref.shape[0], d)
    kc = kc_ref[...]
    dn = (((1,), (1,)), ((), ()))
    c2 = scale * LOG2E

    def tile(t, carry):
        q0 = pl.multiple_of(t * tq, tq)
        k0 = pl.multiple_of(ws_ref[t] * GRID_W, GRID_W)
        q = q_ref[pl.ds(q0, tq), :]
        s_loc = lax.dot_general(q, k_ref[pl.ds(k0, tk), :], dn, preferred_element_type=F32) * c2 \
            + bias_ref[case_ref[t]]
        s_ctx = lax.dot_general(q, kc, dn, preferred_element_type=F32) * c2
        m = jnp.maximum(jnp.max(s_loc, axis=-1, keepdims=True), jnp.max(s_ctx, axis=-1, keepdims=True))
        p_loc = jnp.exp2(s_loc - m).astype(BF16)
        p_ctx = jnp.exp2(s_ctx - m).astype(BF16)
        acc = jnp.dot(p_loc, vx_ref[pl.ds(k0, tk), :], preferred_element_type=F32)
        acc = acc + jnp.dot(p_ctx, vcx_ref[...], preferred_element_type=F32)
        o_ref[pl.ds(q0, tq), :] = (acc[:, 0:d] / acc[:, d:d + 1]).astype(o_ref.dtype)
        return carry

    lax.fori_loop(0, ntile, tile, 0, unroll=2 if ntile % 2 == 0 else 1)


def _na_attention(qkv, qkvc, rpb):
    b, c3, s, d = qkv.shape
    nh = c3 // 3
    lc = qkvc.shape[2]
    rows = s // GRID_W
    wrows, ws_list, case_list, plans = _na_plan(rows)
    blk = _na_block_table(rpb)
    nrel = blk.shape[1]
    tq = NA_ROWS_PER_TILE * GRID_W
    tk = wrows * GRID_W
    ntile = rows // NA_ROWS_PER_TILE
    ws_arr = jnp.asarray(np.array(ws_list, np.int32))
    case_arr = jnp.asarray(np.array(case_list, np.int32))
    blocks = 3 * _nbytes((s, d), BF16) + 2 * _nbytes((lc, d), BF16) + _nbytes((nrel, GRID_W, 2 * GRID_W), F32) \
        + _nbytes((s, d), BF16)
    scratch = _nbytes((len(plans), tq, tk), F32) + _nbytes((s + lc, 2 * d), BF16)
    grid_spec = pltpu.PrefetchScalarGridSpec(
        num_scalar_prefetch=2,
        grid=(nh, b),
        in_specs=[
            pl.BlockSpec((None, None, s, d), lambda h, bi, *_: (bi, h, 0, 0)),
            pl.BlockSpec((None, None, s, d), lambda h, bi, *_: (bi, nh + h, 0, 0)),
            pl.BlockSpec((None, None, s, d), lambda h, bi, *_: (bi, 2 * nh + h, 0, 0)),
            pl.BlockSpec((None, None, lc, d), lambda h, bi, *_: (bi, nh + h, 0, 0)),
            pl.BlockSpec((None, None, lc, d), lambda h, bi, *_: (bi, 2 * nh + h, 0, 0)),
            pl.BlockSpec((None, nrel, GRID_W, 2 * GRID_W), lambda h, bi, *_: (h, 0, 0, 0)),
        ],
        out_specs=pl.BlockSpec((None, s, d), lambda h, bi, *_: (bi, 0, h)),
        scratch_shapes=[pltpu.VMEM((len(plans), tq, tk), F32), pltpu.VMEM((s, 2 * d), BF16),
                        pltpu.VMEM((lc, 2 * d), BF16)],
    )
    return pl.pallas_call(
        functools.partial(_na_kernel, ntile=ntile, tq=tq, tk=tk, scale=d ** -0.5, plans=plans),
        grid_spec=grid_spec,
        out_shape=jax.ShapeDtypeStruct((b, s, nh * d), BF16),
        compiler_params=_params(blocks, scratch + 8 * _nbytes((tq, tk + lc), F32), 2),
        name="neighbourhood_attention",
    )(ws_arr, case_arr, qkv, qkv, qkv, qkvc, qkvc, blk)


def _dense_attn_kernel(*refs, scale, n_src, chunks):
    q_ref = refs[0]
    k_refs = refs[1:1 + n_src]
    v_refs = refs[1 + n_src:1 + 2 * n_src]
    o_ref = refs[1 + 2 * n_src]
    vx_refs = refs[2 + 2 * n_src:]
    dv = v_refs[0].shape[1]

    @pl.when(pl.program_id(2) == 0)
    def _():
        for v_ref, vx_ref in zip(v_refs, vx_refs):
            vx_ref[:, 0:dv] = v_ref[...]
            vx_ref[:, dv:2 * dv] = _ones_column(v_ref.shape[0], dv)

    q = q_ref[...]
    c2 = scale * LOG2E
    m = acc = None
    for src, start, size in chunks:
        s = lax.dot_general(q, k_refs[src][start:start + size, :], (((1,), (1,)), ((), ())),
                            preferred_element_type=F32)
        m_chunk = jnp.max(s, axis=-1, keepdims=True)
        m_new = m_chunk if m is None else jnp.maximum(m, m_chunk)
        p = jnp.exp2((s - m_new) * c2)
        pv = jnp.dot(p.astype(BF16), vx_refs[src][start:start + size, :], preferred_element_type=F32)
        acc = pv if acc is None else acc * jnp.exp2((m - m_new) * c2) + pv
        m = m_new
    o_ref[...] = (acc[:, 0:dv] / acc[:, dv:dv + 1]).astype(o_ref.dtype)


def _dense_attention(q, kv_list, nh, q_off, scale, name, tq_target=512, chunk_target=512):
    b, _, sq, dk = q.shape
    dv = kv_list[0][1].shape[3]
    tq = _pick_tile(sq, tq_target)
    chunks, in_specs, arrays_k, arrays_v, scratch = [], [], [], [], []
    blocks = _nbytes((tq, dk), BF16) + _nbytes((tq, dv), BF16)
    tmp = 4 * _nbytes((tq, chunk_target), F32) + 2 * _nbytes((tq, 2 * dv), F32)
    kspecs, vspecs = [], []
    for src, (k, v, k_off, v_off) in enumerate(kv_list):
        sk = k.shape[2]
        chunks += [(src, st, min(chunk_target, sk - st)) for st in range(0, sk, chunk_target)]
        kspecs.append(pl.BlockSpec((None, None, sk, dk), lambda bi, h, i, off=k_off: (bi, off + h, 0, 0)))
        vspecs.append(pl.BlockSpec((None, None, sk, dv), lambda bi, h, i, off=v_off: (bi, off + h, 0, 0)))
        arrays_k.append(k)
        arrays_v.append(v)
        scratch.append(pltpu.VMEM((sk, 2 * dv), BF16))
        blocks += _nbytes((sk, dk), BF16) + _nbytes((sk, dv), BF16)
        tmp += _nbytes((sk, 2 * dv), BF16)
    return pl.pallas_call(
        functools.partial(_dense_attn_kernel, scale=scale, n_src=len(kv_list), chunks=chunks),
        grid=(b, nh, sq // tq),
        in_specs=[pl.BlockSpec((None, None, tq, dk), lambda bi, h, i: (bi, q_off + h, i, 0))] + kspecs + vspecs,
        out_specs=pl.BlockSpec((None, tq, dv), lambda bi, h, i: (bi, i, h)),
        out_shape=jax.ShapeDtypeStruct((b, sq, nh * dv), BF16),
        scratch_shapes=scratch,
        compiler_params=_params(blocks, tmp, 3),
        name=name,
    )(q, *arrays_k, *arrays_v)


def _pool_kernel(u_ref, w_ref, sc_ref, o_ref, *, windows):
    n, gd = u_ref.shape[0], w_ref.shape[1]
    t = lax.broadcasted_iota(jnp.int32, (n, gd), 0)
    for g, window in enumerate(windows):
        cols = slice(g * gd, (g + 1) * gd)
        u = u_ref[:, cols]
        half = window // 2
        ssum = u
        for j in range(1, half + 1):
            ssum = ssum + jnp.where(t >= j, pltpu.roll(u, j, axis=0), 0.0)
        for j in range(1, half):
            ssum = ssum + jnp.where(t < n - j, pltpu.roll(u, n - j, axis=0), 0.0)
        cnt = jnp.minimum(t + half, n) - jnp.maximum(t - half, 0)
        p = ssum / cnt.astype(F32) - u
        y = jnp.dot(p.astype(BF16), w_ref[g], preferred_element_type=F32)
        o_ref[:, cols] = (y * sc_ref[:, cols]).astype(o_ref.dtype)


def _pool_mix(u, w_pool, pool_scale):
    b, s, w = u.shape
    g, gd, _ = w_pool.shape
    blocks = _nbytes((s, w), F32) + _nbytes((g, gd, gd), BF16) + _nbytes((s, w), BF16)
    return pl.pallas_call(
        functools.partial(_pool_kernel, windows=POOL_WINDOWS[:g]),
        grid=(b,),
        in_specs=[
            pl.BlockSpec((None, s, w), lambda bi: (bi, 0, 0)),
            pl.BlockSpec((g, gd, gd), lambda bi: (0, 0, 0)),
            pl.BlockSpec((1, w), lambda bi: (0, 0)),
        ],
        out_specs=pl.BlockSpec((None, s, w), lambda bi: (bi, 0, 0)),
        out_shape=jax.ShapeDtypeStruct((b, s, w), BF16),
        compiler_params=_params(blocks, 6 * _nbytes((s, gd), F32), 1),
        name="pool_mix",
    )(u, w_pool.astype(BF16), pool_scale.reshape(1, w))


def _rope_table(n):
    half = QK_ROPE_DIM // 2
    nf = half // 2
    t = jnp.arange(n)
    inv = ROPE_BASE ** (-jnp.arange(nf, dtype=F32) / nf)
    ar = (t // GRID_W).astype(F32)[:, None] * inv[None, :]
    ac = (t % GRID_W).astype(F32)[:, None] * inv[None, :]
    cr, sr, cc, sc = jnp.cos(ar), jnp.sin(ar), jnp.cos(ac), jnp.sin(ac)
    return jnp.concatenate([cr, cr, cc, cc, -sr, sr, -sc, sc], axis=1)


def _identity_rope_table(n):
    return jnp.concatenate([jnp.ones((n, QK_ROPE_DIM), F32), jnp.zeros((n, QK_ROPE_DIM), F32)], axis=1)


def _swap_rope_cols(w):
    nf = QK_ROPE_DIM // 4
    lead = w.shape[:-1]
    return jnp.flip(w.reshape(lead + (2, 2, nf)), axis=-2).reshape(lead + (QK_ROPE_DIM,))


def _even_mixer(h, hc, w_in, layer, w_four, rpb, with_ctx_out):
    fw = w_four.shape[0] * w_four.shape[1]
    nh = rpb.shape[0]
    u = _mm_plain(h, w_in, BF16, "even_in_fourier", layer, 0, fw)
    qkv = _mm_chunked(h, w_in, "even_in_qkv", layer, fw)
    qkvc = _mm_chunked(hc, w_in, "even_in_qkv_ctx", layer, fw)
    y_f = _four_channels(_dft_positions(u), w_four)
    y_a = _na_attention(qkv, qkvc, rpb)
    ctx_ops = None
    if with_ctx_out:
        uc = _mm_plain(hc, w_in, BF16, "even_in_fourier_ctx", layer, 0, fw)
        yc_f = _four_channels(_dft_positions(uc), w_four)
        yc_a = _dense_attention(qkvc, [(qkvc, qkvc, nh, 2 * nh)], nh, 0, HEAD_DIM ** -0.5, "ctx_attention_even")
        ctx_ops = [yc_f, yc_a]
    return [y_f, y_a], ctx_ops


def _odd_mixer(h, hc, w_in, w_pool, pool_scale, g_q, g_kv, w_uq, w_ukv, with_ctx_out):
    b, s, d = h.shape
    lc = hc.shape[1]
    pw = w_pool.shape[0] * w_pool.shape[1]
    q_rank, kv_rank = g_q.shape[0], g_kv.shape[0]
    nh = w_uq.shape[1] // (QK_NOPE_DIM + QK_ROPE_DIM)
    o_q, o_kv, o_kr = pw, pw + q_rank, pw + q_rank + kv_rank
    w_in = w_in.astype(BF16)
    w_pool_in, w_q_in, w_kv_in, w_kr = w_in[:, :o_q], w_in[:, o_q:o_kv], w_in[:, o_kv:o_kr], w_in[:, o_kr:]
    w_kr2 = jnp.concatenate([w_kr, _swap_rope_cols(w_kr)], axis=1)
    wq = w_uq.astype(BF16).reshape(q_rank, nh, QK_NOPE_DIM + QK_ROPE_DIM)
    wq_r = wq[..., QK_NOPE_DIM:]
    wq2 = jnp.concatenate([wq[..., :QK_NOPE_DIM], wq_r, _swap_rope_cols(wq_r)], axis=-1).reshape(q_rank, -1)
    w_ukv = w_ukv.astype(BF16)
    tab, tab_c = _rope_table(s), _identity_rope_table(lc)

    def keys_values(hh, table, tag):
        ckv_n = _mm_rmsnorm(hh, w_kv_in, g_kv, "odd_in_ckv" + tag)
        k_rot = _mm_krope(hh, w_kr2, table, "odd_in_krope" + tag)
        return _mm_kv(ckv_n, w_ukv, k_rot, "odd_up_kv" + tag)

    k_l, v_l = keys_values(h, tab, "")
    k_c, v_c = keys_values(hc, tab_c, "_ctx")
    cq_n = _mm_rmsnorm(h, w_q_in, g_q, "odd_in_cq")
    q = _mm_qrope(cq_n, wq2, tab, "odd_up_q")
    scale = (QK_NOPE_DIM + QK_ROPE_DIM) ** -0.5
    attn = _dense_attention(q, [(k_l, v_l, 0, 0), (k_c, v_c, 0, 0)], nh, 0, scale, "mla_attention")
    u = _mm_plain(h, w_pool_in, F32, "odd_in_pool")
    pooled = _pool_mix(u, w_pool, pool_scale)
    ctx_ops = None
    if with_ctx_out:
        cqc_n = _mm_rmsnorm(hc, w_q_in, g_q, "odd_in_cq_ctx")
        qc = _mm_qrope(cqc_n, wq2, tab_c, "odd_up_q_ctx")
        attn_c = _dense_attention(qc, [(k_c, v_c, 0, 0)], nh, 0, scale, "ctx_attention_odd")
        uc = _mm_plain(hc, w_pool_in, F32, "odd_in_pool_ctx")
        ctx_ops = [_pool_mix(uc, w_pool, pool_scale), attn_c]
    return [pooled, attn], ctx_ops


def _ffn(xs, norm_g, shift, scale, gate, wg, wu, wd, layer, tag):
    h2 = _norm_call(xs, norm_g, shift, scale)
    act = _swiglu_up(h2, wg, wu, layer, "ffn_up_" + tag)
    return _mm_residual([act], wd, xs, gate, "ffn_down_" + tag, layer)


def kernel(x, c, ctx, c_ctx, w_mod, b_mod, norm1_g, norm2_g, w_in_ab, w_four, na_rpb, w_out_ab, w_in_cd, w_pool,
           pool_scale, mla_gq, mla_gkv, w_uq, w_ukv, w_out_cd, w_ffn_gate, w_ffn_up, w_ffn_down, final_g):
    b, s, d = x.shape
    depth = w_mod.shape[0]
    nrow = -(-(b + 1) // 8) * 8
    cvec = jnp.concatenate([c, c_ctx[None, :], jnp.zeros((nrow - b - 1, d), F32)], axis=0)
    mods = _mod_call(cvec, w_mod, b_mod)
    w_in_ab, w_out_ab, w_out_cd = w_in_ab.astype(BF16), w_out_ab.astype(BF16), w_out_cd.astype(BF16)
    wg, wu, wd = w_ffn_gate.astype(BF16), w_ffn_up.astype(BF16), w_ffn_down.astype(BF16)
    xc = ctx
    for l in range(depth):
        last = l == depth - 1
        mod = mods[l, :b].reshape(b, 1, 6 * d)
        modc = jnp.broadcast_to(mods[l, b].reshape(1, 1, 6 * d), (b, 1, 6 * d))
        sh1, sc1, g1, sh2, sc2, g2 = jnp.split(mod, 6, axis=-1)
        csh1, csc1, cg1, csh2, csc2, cg2 = jnp.split(modc, 6, axis=-1)
        h = _norm_call(x, norm1_g[l], sh1, sc1)
        hc = _norm_call(xc, norm1_g[l], csh1, csc1)
        i = l // 2
        if l % 2 == 0:
            ops, ctx_ops = _even_mixer(h, hc, w_in_ab, i, w_four[i], na_rpb[i], not last)
            w_out = w_out_ab
        else:
            ops, ctx_ops = _odd_mixer(h, hc, w_in_cd[i], w_pool[i], pool_scale[i], mla_gq[i], mla_gkv[i],
                                      w_uq[i], w_ukv[i], not last)
            w_out = w_out_cd
        x = _mm_residual(ops, w_out, x, g1, f"out_proj_{l}", i)
        x = _ffn(x, norm2_g[l], sh2, sc2, g2, wg, wu, wd, l, f"{l}")
        if not last:
            xc = _mm_residual(ctx_ops, w_out, xc, cg1, f"out_proj_ctx_{l}", i)
            xc = _ffn(xc, norm2_g[l], csh2, csc2, cg2, wg, wu, wd, l, f"ctx_{l}")
    return _norm_call(x, final_g, out_dtype=F32)
```

```python
import functools
import math

import numpy as np
import jax
import jax.numpy as jnp
from jax import lax
from jax.experimental import pallas as pl
from jax.experimental.pallas import tpu as pltpu

GRID_W = 64
HEAD_DIM = 128
EPS = 1e-6
NEG_INF = -1e30
NA_WIN_H = 8
NA_WIN_W = 16
POOL_WINDOWS = (2, 4, 8, 16)
QK_NOPE_DIM = 128
QK_ROPE_DIM = 64
V_HEAD_DIM = 128
ROPE_BASE = 10000.0
LOG2E = 1.4426950408889634

V7X_LANES = 128
V7X_VMEM_BYTES = 64 * 1024 * 1024
V7X_VMEM_CAP = 56 * 1024 * 1024
NA_ROWS_PER_TILE = 4

BF16 = jnp.bfloat16
F32 = jnp.float32


def _pick_tile(n, target, mult=16):
    if n <= target:
        return n
    for t in range(target, 0, -1):
        if n % t == 0 and t % mult == 0:
            return t
    return n


def _nbytes(shape, dtype):
    return int(np.prod(shape)) * jnp.dtype(dtype).itemsize


def _params(block_bytes, scratch_bytes=0, ndims=3):
    est = 2 * block_bytes + scratch_bytes
    limit = int(min(max(est * 5 // 4 + (4 << 20), 16 << 20), V7X_VMEM_CAP))
    return pltpu.CompilerParams(dimension_semantics=("arbitrary",) * ndims, vmem_limit_bytes=limit)


def _mod_kernel(c_ref, w_ref, b_ref, o_ref):
    cv = c_ref[...]
    s = cv * (1.0 / (1.0 + jnp.exp(-cv)))
    acc = jnp.dot(s.astype(BF16), w_ref[...].astype(BF16), preferred_element_type=F32)
    o_ref[...] = acc + b_ref[...]


def _mod_call(cvec, w_mod, b_mod):
    depth, d, n = w_mod.shape
    r = cvec.shape[0]
    tn = _pick_tile(n, 1024, V7X_LANES)
    blocks = _nbytes((r, d), F32) + _nbytes((d, tn), F32) + 2 * _nbytes((r, tn), F32)
    return pl.pallas_call(
        _mod_kernel,
        grid=(depth, n // tn),
        in_specs=[
            pl.BlockSpec((r, d), lambda l, j: (0, 0)),
            pl.BlockSpec((None, d, tn), lambda l, j: (l, 0, j)),
            pl.BlockSpec((None, 1, tn), lambda l, j: (l, 0, j)),
        ],
        out_specs=pl.BlockSpec((None, r, tn), lambda l, j: (l, 0, j)),
        out_shape=jax.ShapeDtypeStruct((depth, r, n), F32),
        compiler_params=_params(blocks, _nbytes((d, tn), BF16), 2),
        name="mod_vectors",
    )(cvec, w_mod, b_mod.reshape(depth, 1, n))


def _norm_mod_kernel(x_ref, g_ref, sh_ref, sc_ref, o_ref):
    xf = x_ref[...]
    y = xf * lax.rsqrt(jnp.mean(xf * xf, axis=-1, keepdims=True) + EPS) * g_ref[...]
    o_ref[...] = (y * (1.0 + sc_ref[...]) + sh_ref[...]).astype(o_ref.dtype)


def _norm_kernel(x_ref, g_ref, o_ref):
    xf = x_ref[...]
    y = xf * lax.rsqrt(jnp.mean(xf * xf, axis=-1, keepdims=True) + EPS) * g_ref[...]
    o_ref[...] = y.astype(o_ref.dtype)


def _norm_call(x, g, shift=None, scale=None, out_dtype=BF16):
    b, s, d = x.shape
    ts = _pick_tile(s, 512)
    xspec = pl.BlockSpec((None, ts, d), lambda bi, i: (bi, i, 0))
    gspec = pl.BlockSpec((1, d), lambda bi, i: (0, 0))
    mspec = pl.BlockSpec((None, 1, d), lambda bi, i: (bi, 0, 0))
    blocks = _nbytes((ts, d), F32) + _nbytes((ts, d), out_dtype) + 3 * _nbytes((1, d), F32)
    common = dict(
        grid=(b, s // ts),
        out_specs=xspec,
        out_shape=jax.ShapeDtypeStruct((b, s, d), out_dtype),
        compiler_params=_params(blocks, 2 * _nbytes((ts, d), F32), 2),
    )
    if shift is None:
        return pl.pallas_call(_norm_kernel, in_specs=[xspec, gspec], name="rmsnorm", **common)(
            x, g.reshape(1, d))
    return pl.pallas_call(_norm_mod_kernel, in_specs=[xspec, gspec, mspec, mspec], name="rmsnorm_modulate",
                          **common)(x, g.reshape(1, d), shift, scale)


def _mm_kernel(*refs, n_ops, n_extra, epilogue):
    a_refs = refs[:n_ops]
    w_refs = refs[n_ops:2 * n_ops]
    e_refs = refs[2 * n_ops:2 * n_ops + n_extra]
    o_refs = refs[2 * n_ops + n_extra:]
    acc = None
    for a_ref, w_ref in zip(a_refs, w_refs):
        part = jnp.dot(a_ref[...], w_ref[...], preferred_element_type=F32)
        acc = part if acc is None else acc + part
    epilogue(acc, e_refs, o_refs)


def _mm_call(a_ops, w_ops, extras, epilogue, outs, grid, tm, tn, name, tmp_bytes=0):
    arrays = [a for a, _ in a_ops] + [w for w, _ in w_ops] + [e for e, _ in extras]
    in_specs = [s for _, s in a_ops] + [s for _, s in w_ops] + [s for _, s in extras]
    blocks = 0
    for arr, spec in a_ops + w_ops + extras:
        blocks += _nbytes([d for d in spec.block_shape if d is not None], arr.dtype)
    for sds, spec in outs:
        blocks += _nbytes([d for d in spec.block_shape if d is not None], sds.dtype)
    kern = functools.partial(_mm_kernel, n_ops=len(a_ops), n_extra=len(extras), epilogue=epilogue)
    res = pl.pallas_call(
        kern,
        grid=grid,
        in_specs=in_specs,
        out_specs=[s for _, s in outs],
        out_shape=[o for o, _ in outs],
        compiler_params=_params(blocks, 3 * _nbytes((tm, tn), F32) + tmp_bytes, len(grid)),
        name=name,
    )(*arrays)
    return res


def _a_spec(tm, k, col_blk=0):
    return pl.BlockSpec((None, tm, k), lambda b, i, j: (b, i, col_blk))


def _w3(w):
    return w if w.ndim == 3 else w[None]


def _w_spec(k, tn, layer=0, row_blk=0, col_off=0):
    return pl.BlockSpec((None, k, tn), lambda b, i, j: (layer, row_blk, j + col_off))


def _tile_spec(tm, tn):
    return pl.BlockSpec((None, tm, tn), lambda b, i, j: (b, i, j))


def _vec_spec(tn):
    return pl.BlockSpec((None, 1, tn), lambda b, i, j: (b, 0, j))


def _ep_store(acc, e_refs, o_refs):
    o_refs[0][...] = acc.astype(o_refs[0].dtype)


def _col_tile(n, col0, target, mult):
    return _pick_tile(math.gcd(n, col0) if col0 else n, target, mult)


def _mm_plain(a, w, out_dtype, name, layer=0, col0=0, n=None, tm_target=1024, tn_target=512):
    b, s, k = a.shape
    w = _w3(w)
    n = w.shape[2] - col0 if n is None else n
    tm = _pick_tile(s, tm_target)
    tn = _col_tile(n, col0, tn_target, V7X_LANES)
    return _mm_call([(a, _a_spec(tm, k))], [(w, _w_spec(k, tn, layer, 0, col0 // tn))], [], _ep_store,
                    [(jax.ShapeDtypeStruct((b, s, n), out_dtype), _tile_spec(tm, tn))],
                    (b, s // tm, n // tn), tm, tn, name)[0]


def _ep_chunks(acc, e_refs, o_refs, *, width):
    o_ref = o_refs[0]
    for c in range(acc.shape[1] // width):
        o_ref[c] = acc[:, c * width:(c + 1) * width].astype(o_ref.dtype)


def _mm_chunked(a, w, name, layer=0, col0=0, n=None, width=HEAD_DIM, tm_target=1024, tn_target=512):
    b, s, k = a.shape
    w = _w3(w)
    n = w.shape[2] - col0 if n is None else n
    tm = _pick_tile(s, tm_target)
    tn = _col_tile(n, col0, tn_target, width)
    cpt = tn // width
    ospec = pl.BlockSpec((None, cpt, tm, width), lambda bi, i, j: (bi, j, i, 0))
    return _mm_call([(a, _a_spec(tm, k))], [(w, _w_spec(k, tn, layer, 0, col0 // tn))], [],
                    functools.partial(_ep_chunks, width=width),
                    [(jax.ShapeDtypeStruct((b, n // width, s, width), BF16), ospec)],
                    (b, s // tm, n // tn), tm, tn, name)[0]


def _ep_residual(acc, e_refs, o_refs):
    x_ref, g_ref = e_refs
    o_refs[0][...] = x_ref[...] + g_ref[...] * acc


def _mm_residual(a_list, w, x, gate, name, layer=0, tm_target=1024, tn_target=512):
    b, s, n = x.shape
    w = _w3(w)
    tm = _pick_tile(s, tm_target)
    tn = _pick_tile(n, tn_target, V7X_LANES)
    kblk = functools.reduce(math.gcd, [a.shape[2] for a in a_list])
    a_ops, w_ops = [], []
    for a in a_list:
        for cb in range(a.shape[2] // kblk):
            a_ops.append((a, _a_spec(tm, kblk, cb)))
            w_ops.append((w, _w_spec(kblk, tn, layer, len(w_ops))))
    extras = [(x, _tile_spec(tm, tn)), (gate, _vec_spec(tn))]
    return _mm_call(a_ops, w_ops, extras, _ep_residual,
                    [(jax.ShapeDtypeStruct((b, s, n), F32), _tile_spec(tm, tn))],
                    (b, s // tm, n // tn), tm, tn, name)[0]


def _swiglu_kernel(a_ref, wg_ref, wu_ref, o_ref, wgb_ref, wub_ref):
    @pl.when((pl.program_id(1) == 0) & (pl.program_id(2) == 0))
    def _():
        wgb_ref[...] = wg_ref[...].astype(BF16)
        wub_ref[...] = wu_ref[...].astype(BF16)

    a = a_ref[...]
    g = jnp.dot(a, wgb_ref[...], preferred_element_type=F32)
    u = jnp.dot(a, wub_ref[...], preferred_element_type=F32)
    o_ref[...] = (g * (1.0 / (1.0 + jnp.exp(-g))) * u).astype(o_ref.dtype)


def _swiglu_up(a, wg, wu, layer, name, tm_target=1024, tn_target=512):
    b, s, k = a.shape
    n = wg.shape[2]
    tm = _pick_tile(s, tm_target)
    tn = _pick_tile(n, tn_target, V7X_LANES)
    blocks = _nbytes((tm, k), BF16) + 2 * _nbytes((k, tn), F32) + _nbytes((tm, tn), BF16)
    wspec = pl.BlockSpec((None, k, tn), lambda j, bi, i: (layer, 0, j))
    return pl.pallas_call(
        _swiglu_kernel,
        grid=(n // tn, b, s // tm),
        in_specs=[pl.BlockSpec((None, tm, k), lambda j, bi, i: (bi, i, 0)), wspec, wspec],
        out_specs=pl.BlockSpec((None, tm, tn), lambda j, bi, i: (bi, i, j)),
        out_shape=jax.ShapeDtypeStruct((b, s, n), BF16),
        scratch_shapes=[pltpu.VMEM((k, tn), BF16), pltpu.VMEM((k, tn), BF16)],
        compiler_params=_params(blocks, 2 * _nbytes((k, tn), BF16) + 4 * _nbytes((tm, tn), F32), 3),
        name=name,
    )(a, wg, wu)


def _ep_rmsnorm(acc, e_refs, o_refs):
    g_ref, = e_refs
    y = acc * lax.rsqrt(jnp.mean(acc * acc, axis=-1, keepdims=True) + EPS) * g_ref[...]
    o_refs[0][...] = y.astype(o_refs[0].dtype)


def _mm_rmsnorm(a, w, g, name, tm_target=1024):
    b, s, k = a.shape
    n = w.shape[1]
    tm = _pick_tile(s, tm_target)
    gspec = pl.BlockSpec((1, n), lambda bi, i, j: (0, 0))
    return _mm_call([(a, _a_spec(tm, k))], [(_w3(w), _w_spec(k, n))], [(g.reshape(1, n), gspec)], _ep_rmsnorm,
                    [(jax.ShapeDtypeStruct((b, s, n), BF16), _tile_spec(tm, n))],
                    (b, s // tm, 1), tm, n, name)[0]


def _rope_pair(hi, tab):
    z = hi * tab
    return z + pltpu.roll(z, QK_ROPE_DIM, axis=1)


def _ep_krope(acc, e_refs, o_refs):
    tab_ref, = e_refs
    r = _rope_pair(acc, tab_ref[...])
    lane = lax.broadcasted_iota(jnp.int32, r.shape, 1)
    o_refs[0][...] = jnp.where(lane < QK_ROPE_DIM, r, 0.0).astype(o_refs[0].dtype)


def _mm_krope(a, w, tab, name, tm_target=1024):
    b, s, k = a.shape
    n = w.shape[1]
    tm = _pick_tile(s, tm_target)
    tspec = pl.BlockSpec((tm, n), lambda bi, i, j: (i, 0))
    return _mm_call([(a, _a_spec(tm, k))], [(_w3(w), _w_spec(k, n))], [(tab, tspec)], _ep_krope,
                    [(jax.ShapeDtypeStruct((b, s, n), BF16), _tile_spec(tm, n))],
                    (b, s // tm, 1), tm, n, name)[0]


def _ep_qrope(acc, e_refs, o_refs, *, heads):
    tab_ref, = e_refs
    o_ref = o_refs[0]
    tab = tab_ref[...]
    blk = QK_NOPE_DIM + 2 * QK_ROPE_DIM
    for h in range(heads):
        o_ref[h, :, 0:QK_NOPE_DIM] = acc[:, h * blk:h * blk + QK_NOPE_DIM].astype(o_ref.dtype)
        hi = acc[:, h * blk + QK_NOPE_DIM:(h + 1) * blk]
        o_ref[h, :, QK_NOPE_DIM:blk] = _rope_pair(hi, tab).astype(o_ref.dtype)


def _mm_qrope(a, w, tab, name, tm_target=1024, heads_per_tile=4):
    b, s, k = a.shape
    blk = QK_NOPE_DIM + 2 * QK_ROPE_DIM
    nh = w.shape[1] // blk
    hpt = math.gcd(nh, heads_per_tile)
    tn = hpt * blk
    tm = _pick_tile(s, tm_target)
    tspec = pl.BlockSpec((tm, 2 * QK_ROPE_DIM), lambda bi, i, j: (i, 0))
    ospec = pl.BlockSpec((None, hpt, tm, blk), lambda bi, i, j: (bi, j, i, 0))
    return _mm_call([(a, _a_spec(tm, k))], [(_w3(w), _w_spec(k, tn))], [(tab, tspec)],
                    functools.partial(_ep_qrope, heads=hpt),
                    [(jax.ShapeDtypeStruct((b, nh, s, blk), BF16), ospec)],
                    (b, s // tm, nh // hpt), tm, tn, name)[0]


def _ep_kv(acc, e_refs, o_refs, *, heads):
    kr_ref, = e_refs
    k_ref, v_ref = o_refs
    kr = kr_ref[...]
    blk = QK_NOPE_DIM + V_HEAD_DIM
    for h in range(heads):
        k_ref[h, :, 0:QK_NOPE_DIM] = acc[:, h * blk:h * blk + QK_NOPE_DIM].astype(k_ref.dtype)
        k_ref[h, :, QK_NOPE_DIM:QK_NOPE_DIM + kr.shape[1]] = kr
        v_ref[h] = acc[:, h * blk + QK_NOPE_DIM:(h + 1) * blk].astype(v_ref.dtype)


def _mm_kv(a, w, krot, name, tm_target=1024, heads_per_tile=4):
    b, s, k = a.shape
    blk = QK_NOPE_DIM + V_HEAD_DIM
    nh = w.shape[1] // blk
    hpt = math.gcd(nh, heads_per_tile)
    tn = hpt * blk
    tm = _pick_tile(s, tm_target)
    kd = QK_NOPE_DIM + krot.shape[2]
    krspec = pl.BlockSpec((None, tm, krot.shape[2]), lambda bi, i, j: (bi, i, 0))
    kspec = pl.BlockSpec((None, hpt, tm, kd), lambda bi, i, j: (bi, j, i, 0))
    vspec = pl.BlockSpec((None, hpt, tm, V_HEAD_DIM), lambda bi, i, j: (bi, j, i, 0))
    return _mm_call([(a, _a_spec(tm, k))], [(_w3(w), _w_spec(k, tn))], [(krot, krspec)],
                    functools.partial(_ep_kv, heads=hpt),
                    [(jax.ShapeDtypeStruct((b, nh, s, kd), BF16), kspec),
                     (jax.ShapeDtypeStruct((b, nh, s, V_HEAD_DIM), BF16), vspec)],
                    (b, s // tm, nh // hpt), tm, tn, name)


def _dft_matrix(n):
    m = jnp.arange(n, dtype=jnp.int32)
    side = 1
    while side * side < n:
        side *= 2
    if side * side != n:
        k = jnp.arange(n, dtype=jnp.int32)
        ang = (2.0 * math.pi / n) * ((k[:, None] * m[None, :]) % n).astype(F32)
        return jnp.concatenate([jnp.cos(ang), jnp.sin(ang)], axis=0).astype(BF16)[None]
    a = jnp.arange(side, dtype=jnp.int32)
    ang_x = (2.0 * math.pi / side) * ((a[:, None] * m[None, :]) % side).astype(F32)
    ang_y = (2.0 * math.pi / n) * ((a[:, None] * m[None, :]) % n).astype(F32)
    cx, sx = jnp.cos(ang_x), jnp.sin(ang_x)
    cy, sy = jnp.cos(ang_y)[None], jnp.sin(ang_y)[None]
    p = jnp.concatenate([cx, sx], axis=0)[:, None, :]
    q = jnp.concatenate([sx, -cx], axis=0)[:, None, :]
    return (p * cy - q * sy).astype(BF16).reshape(1, 2 * n, n)


def _dft_positions(u):
    b, s, w = u.shape
    mat = _dft_matrix(s)
    tm = _pick_tile(2 * s, 1024)
    tn = _pick_tile(w, 512, V7X_LANES)
    aspec = pl.BlockSpec((None, tm, s), lambda bi, i, j: (0, i, 0))
    wspec = pl.BlockSpec((None, s, tn), lambda bi, i, j: (bi, 0, j))
    return _mm_call([(mat, aspec)], [(u, wspec)], [], _ep_store,
                    [(jax.ShapeDtypeStruct((b, 2 * s, w), BF16), _tile_spec(tm, tn))],
                    (b, 2 * s // tm, w // tn), tm, tn, "fourier_positions")[0]


def _four_channel_kernel(fr_ref, fi_ref, cs_ref, w_ref, o_ref, *, norm):
    gd = fr_ref.shape[1]
    f = jnp.dot(fr_ref[...], cs_ref[0:gd, :], preferred_element_type=F32)
    f = f + jnp.dot(fi_ref[...], cs_ref[gd:2 * gd, :], preferred_element_type=F32)
    f = (f * norm).astype(BF16)
    o_ref[...] = jnp.dot(f, w_ref[...], preferred_element_type=F32).astype(o_ref.dtype)


def _four_channels(f2, w_four):
    b, s2, w = f2.shape
    s = s2 // 2
    g, gd, _ = w_four.shape
    idx = np.arange(gd)
    ang = 2.0 * np.pi * ((idx[:, None] * idx[None, :]) % gd) / gd
    cs = jnp.asarray(np.concatenate([np.cos(ang), -np.sin(ang)], axis=0), dtype=F32).astype(BF16)
    ts = _pick_tile(s, 1024)
    nblk = s // ts
    blocks = 3 * _nbytes((ts, gd), BF16) + _nbytes((2 * gd, gd), BF16) + _nbytes((gd, gd), BF16)
    return pl.pallas_call(
        functools.partial(_four_channel_kernel, norm=1.0 / math.sqrt(s * gd)),
        grid=(b, nblk, g),
        in_specs=[
            pl.BlockSpec((None, ts, gd), lambda bi, i, j: (bi, i, j)),
            pl.BlockSpec((None, ts, gd), lambda bi, i, j: (bi, nblk + i, j)),
            pl.BlockSpec((2 * gd, gd), lambda bi, i, j: (0, 0)),
            pl.BlockSpec((None, gd, gd), lambda bi, i, j: (j, 0, 0)),
        ],
        out_specs=pl.BlockSpec((None, ts, gd), lambda bi, i, j: (bi, i, j)),
        out_shape=jax.ShapeDtypeStruct((b, s, w), BF16),
        compiler_params=_params(blocks, 3 * _nbytes((ts, gd), F32), 3),
        name="fourier_channels",
    )(f2, f2, cs, w_four.astype(BF16))


def _na_plan(rows):
    rpt = NA_ROWS_PER_TILE
    kh = min(NA_WIN_H, rows)
    wrows = min(kh + rpt - 1, rows)
    cases, ws_list, case_list = {}, [], []
    for t in range(rows // rpt):
        r0 = t * rpt
        ws = int(np.clip(r0 - kh // 2, 0, rows - wrows))
        plan = []
        for r in range(r0, r0 + rpt):
            rs = int(np.clip(r - kh // 2, 0, rows - kh))
            plan.append(tuple((ws + w) - r + NA_WIN_H - 1 if rs <= ws + w < rs + kh else None
                              for w in range(wrows)))
        plan = tuple(plan)
        cases.setdefault(plan, len(cases))
        ws_list.append(ws)
        case_list.append(cases[plan])
    return wrows, ws_list, case_list, list(cases)


def _na_block_table(rpb):
    qcol = np.arange(GRID_W)[:, None]
    kcol = np.arange(GRID_W)[None, :]
    cstart = np.clip(qcol - NA_WIN_W // 2, 0, GRID_W - NA_WIN_W)
    col_ok = (kcol >= cstart) & (kcol < cstart + NA_WIN_W)
    cidx = np.clip(kcol - qcol + NA_WIN_W - 1, 0, 2 * NA_WIN_W - 2)
    csel = jnp.asarray(np.eye(rpb.shape[2], dtype=np.float32)[cidx])
    blk = jnp.einsum("hrs,qks->hrqk", rpb, csel, precision=lax.Precision.HIGHEST)
    blk = jnp.where(jnp.asarray(col_ok), blk * LOG2E, NEG_INF)
    return jnp.concatenate([blk, blk], axis=-1)


def _ones_column(rows, width):
    lane = lax.broadcasted_iota(jnp.int32, (rows, width), 1)
    return jnp.where(lane == 0, 1.0, 0.0).astype(BF16)


def _na_kernel(q_ref, k_ref, v_ref, kc_ref, vc_ref, blk_ref, o_ref, bias_ref, vx_ref, vcx_ref,
               *, tiles, tq, tk, scale, plans):
    d = v_ref.shape[1]

    @pl.when(pl.program_id(1) == 0)
    def _():
        for c, plan in enumerate(plans):
            for qr, row in enumerate(plan):
                for kr, ridx in enumerate(row):
                    half = slice((kr % 2) * GRID_W, (kr % 2 + 1) * GRID_W)
                    if ridx is None:
                        blk = jnp.full((GRID_W, GRID_W), NEG_INF, F32)
                    else:
                        blk = blk_ref[ridx][:, half]
                    bias_ref[c, qr * GRID_W:(qr + 1) * GRID_W, kr * GRID_W:(kr + 1) * GRID_W] = blk

    vx_ref[:, 0:d] = v_ref[...]
    vx_ref[:, d:2 * d] = _ones_column(v_ref.shape[0], d)
    vcx_ref[:, 0:d] = vc_ref[...]
    vcx_ref[:, d:2 * d] = _ones_column(vc_ref.shape[0], d)
    kc = kc_ref[...]
    dn = (((1,), (1,)), ((), ()))
    c2 = scale * LOG2E

    for t, (ws, case) in enumerate(tiles):
        q0, k0 = t * tq, ws * GRID_W
        q = q_ref[q0:q0 + tq, :]
        s_loc = lax.dot_general(q, k_ref[k0:k0 + tk, :], dn, preferred_element_type=F32) * c2 + bias_ref[case]
        s_ctx = lax.dot_general(q, kc, dn, preferred_element_type=F32) * c2
        m = jnp.maximum(jnp.max(s_loc, axis=-1, keepdims=True), jnp.max(s_ctx, axis=-1, keepdims=True))
        p_loc = jnp.exp2(s_loc - m).astype(BF16)
        p_ctx = jnp.exp2(s_ctx - m).astype(BF16)
        acc = jnp.dot(p_loc, vx_ref[k0:k0 + tk, :], preferred_element_type=F32)
        acc = acc + jnp.dot(p_ctx, vcx_ref[...], preferred_element_type=F32)
        o_ref[q0:q0 + tq, :] = (acc[:, 0:d] / acc[:, d:d + 1]).astype(o_ref.dtype)


def _na_attention(qkv, qkvc, rpb):
    b, c3, s, d = qkv.shape
    nh = c3 // 3
    lc = qkvc.shape[2] // b
    rows = s // GRID_W
    wrows, ws_list, case_list, plans = _na_plan(rows)
    blk = _na_block_table(rpb)
    nrel = blk.shape[1]
    tq = NA_ROWS_PER_TILE * GRID_W
    tk = wrows * GRID_W
    blocks = 3 * _nbytes((s, d), BF16) + 2 * _nbytes((lc, d), BF16) + _nbytes((nrel, GRID_W, 2 * GRID_W), F32) \
        + _nbytes((s, d), BF16)
    scratch = _nbytes((len(plans), tq, tk), F32) + _nbytes((s + lc, 2 * d), BF16)
    return pl.pallas_call(
        functools.partial(_na_kernel, tiles=list(zip(ws_list, case_list)), tq=tq, tk=tk, scale=d ** -0.5,
                          plans=plans),
        grid=(nh, b),
        in_specs=[
            pl.BlockSpec((None, None, s, d), lambda h, bi: (bi, h, 0, 0)),
            pl.BlockSpec((None, None, s, d), lambda h, bi: (bi, nh + h, 0, 0)),
            pl.BlockSpec((None, None, s, d), lambda h, bi: (bi, 2 * nh + h, 0, 0)),
            pl.BlockSpec((None, None, lc, d), lambda h, bi: (0, nh + h, bi, 0)),
            pl.BlockSpec((None, None, lc, d), lambda h, bi: (0, 2 * nh + h, bi, 0)),
            pl.BlockSpec((None, nrel, GRID_W, 2 * GRID_W), lambda h, bi: (h, 0, 0, 0)),
        ],
        out_specs=pl.BlockSpec((None, s, d), lambda h, bi: (bi, 0, h)),
        out_shape=jax.ShapeDtypeStruct((b, s, nh * d), BF16),
        scratch_shapes=[pltpu.VMEM((len(plans), tq, tk), F32), pltpu.VMEM((s, 2 * d), BF16),
                        pltpu.VMEM((lc, 2 * d), BF16)],
        compiler_params=_params(blocks, scratch + 8 * _nbytes((tq, tk + lc), F32), 2),
        name="neighbourhood_attention",
    )(qkv, qkv, qkv, qkvc, qkvc, blk)


def _dense_attn_kernel(*refs, scale, n_src, chunks):
    q_ref = refs[0]
    k_refs = refs[1:1 + n_src]
    v_refs = refs[1 + n_src:1 + 2 * n_src]
    o_ref = refs[1 + 2 * n_src]
    vx_refs = refs[2 + 2 * n_src:]
    dv = v_refs[0].shape[1]

    @pl.when(pl.program_id(2) == 0)
    def _():
        for v_ref, vx_ref in zip(v_refs, vx_refs):
            vx_ref[:, 0:dv] = v_ref[...]
            vx_ref[:, dv:2 * dv] = _ones_column(v_ref.shape[0], dv)

    q = q_ref[...]
    c2 = scale * LOG2E
    m = acc = None
    for src, start, size in chunks:
        s = lax.dot_general(q, k_refs[src][start:start + size, :], (((1,), (1,)), ((), ())),
                            preferred_element_type=F32)
        m_chunk = jnp.max(s, axis=-1, keepdims=True)
        m_new = m_chunk if m is None else jnp.maximum(m, m_chunk)
        p = jnp.exp2((s - m_new) * c2)
        pv = jnp.dot(p.astype(BF16), vx_refs[src][start:start + size, :], preferred_element_type=F32)
        acc = pv if acc is None else acc * jnp.exp2((m - m_new) * c2) + pv
        m = m_new
    o_ref[...] = (acc[:, 0:dv] / acc[:, dv:dv + 1]).astype(o_ref.dtype)


def _head_major_spec(arr, batch, rows, off, tiled):
    per_batch = arr.shape[2] * arr.shape[0] // batch // rows
    if arr.shape[0] == 1:
        return pl.BlockSpec((None, None, rows, arr.shape[3]),
                            lambda bi, h, i: (0, off + h, bi * per_batch + (i if tiled else 0), 0))
    return pl.BlockSpec((None, None, rows, arr.shape[3]), lambda bi, h, i: (bi, off + h, i if tiled else 0, 0))


def _dense_attention(q, kv_list, nh, q_off, scale, name, batch, tq_target=1024, chunk_target=512):
    dk = q.shape[3]
    sq = q.shape[0] * q.shape[2] // batch
    dv = kv_list[0][1].shape[3]
    tq = _pick_tile(sq, tq_target)
    nq = sq // tq
    chunks, arrays_k, arrays_v, scratch = [], [], [], []
    blocks = _nbytes((tq, dk), BF16) + _nbytes((tq, dv), BF16)
    tmp = 4 * _nbytes((tq, chunk_target), F32) + 2 * _nbytes((tq, 2 * dv), F32)
    kspecs, vspecs = [], []
    for src, (k, v, k_off, v_off) in enumerate(kv_list):
        sk = k.shape[0] * k.shape[2] // batch
        chunks += [(src, st, min(chunk_target, sk - st)) for st in range(0, sk, chunk_target)]
        kspecs.append(_head_major_spec(k, batch, sk, k_off, False))
        vspecs.append(_head_major_spec(v, batch, sk, v_off, False))
        arrays_k.append(k)
        arrays_v.append(v)
        scratch.append(pltpu.VMEM((sk, 2 * dv), BF16))
        blocks += _nbytes((sk, dk), BF16) + _nbytes((sk, dv), BF16)
        tmp += _nbytes((sk, 2 * dv), BF16)
    if q.shape[0] == 1:
        out_spec = pl.BlockSpec((None, tq, dv), lambda bi, h, i: (0, bi * nq + i, h))
    else:
        out_spec = pl.BlockSpec((None, tq, dv), lambda bi, h, i: (bi, i, h))
    return pl.pallas_call(
        functools.partial(_dense_attn_kernel, scale=scale, n_src=len(kv_list), chunks=chunks),
        grid=(batch, nh, nq),
        in_specs=[_head_major_spec(q, batch, tq, q_off, True)] + kspecs + vspecs,
        out_specs=out_spec,
        out_shape=jax.ShapeDtypeStruct((q.shape[0], q.shape[2], nh * dv), BF16),
        scratch_shapes=scratch,
        compiler_params=_params(blocks, tmp, 3),
        name=name,
    )(q, *arrays_k, *arrays_v)


def _pool_kernel(u_ref, w_ref, sc_ref, o_ref, *, windows):
    n, gd = u_ref.shape[0], w_ref.shape[1]
    t = lax.broadcasted_iota(jnp.int32, (n, gd), 0)
    for g, window in enumerate(windows):
        cols = slice(g * gd, (g + 1) * gd)
        u = u_ref[:, cols]
        half = window // 2
        ssum = u
        for j in range(1, half + 1):
            ssum = ssum + jnp.where(t >= j, pltpu.roll(u, j, axis=0), 0.0)
        for j in range(1, half):
            ssum = ssum + jnp.where(t < n - j, pltpu.roll(u, n - j, axis=0), 0.0)
        cnt = jnp.minimum(t + half, n) - jnp.maximum(t - half, 0)
        p = ssum / cnt.astype(F32) - u
        y = jnp.dot(p.astype(BF16), w_ref[g], preferred_element_type=F32)
        o_ref[:, cols] = (y * sc_ref[:, cols]).astype(o_ref.dtype)


def _pool_mix(u, w_pool, pool_scale):
    b, s, w = u.shape
    g, gd, _ = w_pool.shape
    blocks = _nbytes((s, w), F32) + _nbytes((g, gd, gd), BF16) + _nbytes((s, w), BF16)
    return pl.pallas_call(
        functools.partial(_pool_kernel, windows=POOL_WINDOWS[:g]),
        grid=(b,),
        in_specs=[
            pl.BlockSpec((None, s, w), lambda bi: (bi, 0, 0)),
            pl.BlockSpec((g, gd, gd), lambda bi: (0, 0, 0)),
            pl.BlockSpec((1, w), lambda bi: (0, 0)),
        ],
        out_specs=pl.BlockSpec((None, s, w), lambda bi: (bi, 0, 0)),
        out_shape=jax.ShapeDtypeStruct((b, s, w), BF16),
        compiler_params=_params(blocks, 6 * _nbytes((s, gd), F32), 1),
        name="pool_mix",
    )(u, w_pool.astype(BF16), pool_scale.reshape(1, w))


def _rope_table(n):
    half = QK_ROPE_DIM // 2
    nf = half // 2
    t = jnp.arange(n)
    inv = ROPE_BASE ** (-jnp.arange(nf, dtype=F32) / nf)
    ar = (t // GRID_W).astype(F32)[:, None] * inv[None, :]
    ac = (t % GRID_W).astype(F32)[:, None] * inv[None, :]
    cr, sr, cc, sc = jnp.cos(ar), jnp.sin(ar), jnp.cos(ac), jnp.sin(ac)
    return jnp.concatenate([cr, cr, cc, cc, -sr, sr, -sc, sc], axis=1)


def _identity_rope_table(n):
    return jnp.concatenate([jnp.ones((n, QK_ROPE_DIM), F32), jnp.zeros((n, QK_ROPE_DIM), F32)], axis=1)


def _swap_rope_cols(w):
    nf = QK_ROPE_DIM // 4
    lead = w.shape[:-1]
    return jnp.flip(w.reshape(lead + (2, 2, nf)), axis=-2).reshape(lead + (QK_ROPE_DIM,))


def _even_mixer(h, hc, w_in, layer, w_four, rpb, with_ctx_out):
    fw = w_four.shape[0] * w_four.shape[1]
    nh = rpb.shape[0]
    u = _mm_plain(h, w_in, BF16, "even_in_fourier", layer, 0, fw)
    qkv = _mm_chunked(h, w_in, "even_in_qkv", layer, fw)
    qkvc = _mm_chunked(hc, w_in, "even_in_qkv_ctx", layer, fw)
    y_f = _four_channels(_dft_positions(u), w_four)
    y_a = _na_attention(qkv, qkvc, rpb)
    ctx_ops = None
    if with_ctx_out:
        b = h.shape[0]
        uc = _mm_plain(hc, w_in, BF16, "even_in_fourier_ctx", layer, 0, fw).reshape(b, -1, fw)
        yc_f = _four_channels(_dft_positions(uc), w_four).reshape(1, -1, fw)
        yc_a = _dense_attention(qkvc, [(qkvc, qkvc, nh, 2 * nh)], nh, 0, HEAD_DIM ** -0.5, "ctx_attention_even", b)
        ctx_ops = [yc_f, yc_a]
    return [y_f, y_a], ctx_ops


def _odd_mixer(h, hc, w_in, w_pool, pool_scale, g_q, g_kv, w_uq, w_ukv, with_ctx_out):
    b, s, d = h.shape
    lc = hc.shape[1] // b
    pw = w_pool.shape[0] * w_pool.shape[1]
    q_rank, kv_rank = g_q.shape[0], g_kv.shape[0]
    nh = w_uq.shape[1] // (QK_NOPE_DIM + QK_ROPE_DIM)
    o_q, o_kv, o_kr = pw, pw + q_rank, pw + q_rank + kv_rank
    w_in = w_in.astype(BF16)
    w_pool_in, w_q_in, w_kv_in, w_kr = w_in[:, :o_q], w_in[:, o_q:o_kv], w_in[:, o_kv:o_kr], w_in[:, o_kr:]
    w_kr2 = jnp.concatenate([w_kr, _swap_rope_cols(w_kr)], axis=1)
    wq = w_uq.astype(BF16).reshape(q_rank, nh, QK_NOPE_DIM + QK_ROPE_DIM)
    wq_r = wq[..., QK_NOPE_DIM:]
    wq2 = jnp.concatenate([wq[..., :QK_NOPE_DIM], wq_r, _swap_rope_cols(wq_r)], axis=-1).reshape(q_rank, -1)
    w_ukv = w_ukv.astype(BF16)
    tab, tab_c = _rope_table(s), _identity_rope_table(b * lc)

    def keys_values(hh, table, tag):
        ckv_n = _mm_rmsnorm(hh, w_kv_in, g_kv, "odd_in_ckv" + tag)
        k_rot = _mm_krope(hh, w_kr2, table, "odd_in_krope" + tag)
        return _mm_kv(ckv_n, w_ukv, k_rot, "odd_up_kv" + tag)

    k_l, v_l = keys_values(h, tab, "")
    k_c, v_c = keys_values(hc, tab_c, "_ctx")
    cq_n = _mm_rmsnorm(h, w_q_in, g_q, "odd_in_cq")
    q = _mm_qrope(cq_n, wq2, tab, "odd_up_q")
    scale = (QK_NOPE_DIM + QK_ROPE_DIM) ** -0.5
    attn = _dense_attention(q, [(k_l, v_l, 0, 0), (k_c, v_c, 0, 0)], nh, 0, scale, "mla_attention", b)
    u = _mm_plain(h, w_pool_in, F32, "odd_in_pool")
    pooled = _pool_mix(u, w_pool, pool_scale)
    ctx_ops = None
    if with_ctx_out:
        cqc_n = _mm_rmsnorm(hc, w_q_in, g_q, "odd_in_cq_ctx")
        qc = _mm_qrope(cqc_n, wq2, tab_c, "odd_up_q_ctx")
        attn_c = _dense_attention(qc, [(k_c, v_c, 0, 0)], nh, 0, scale, "ctx_attention_odd", b)
        uc = _mm_plain(hc, w_pool_in, F32, "odd_in_pool_ctx").reshape(b, lc, pw)
        ctx_ops = [_pool_mix(uc, w_pool, pool_scale).reshape(1, b * lc, pw), attn_c]
    return [pooled, attn], ctx_ops


def _ffn(xs, norm_g, shift, scale, gate, wg, wu, wd, layer, tag):
    h2 = _norm_call(xs, norm_g, shift, scale)
    act = _swiglu_up(h2, wg, wu, layer, "ffn_up_" + tag)
    return _mm_residual([act], wd, xs, gate, "ffn_down_" + tag, layer)


def kernel(x, c, ctx, c_ctx, w_mod, b_mod, norm1_g, norm2_g, w_in_ab, w_four, na_rpb, w_out_ab, w_in_cd, w_pool,
           pool_scale, mla_gq, mla_gkv, w_uq, w_ukv, w_out_cd, w_ffn_gate, w_ffn_up, w_ffn_down, final_g):
    b, s, d = x.shape
    depth = w_mod.shape[0]
    nrow = -(-(b + 1) // 8) * 8
    cvec = jnp.concatenate([c, c_ctx[None, :], jnp.zeros((nrow - b - 1, d), F32)], axis=0)
    mods = _mod_call(cvec, w_mod, b_mod)
    w_in_ab, w_out_ab, w_out_cd = w_in_ab.astype(BF16), w_out_ab.astype(BF16), w_out_cd.astype(BF16)
    wg, wu, wd = w_ffn_gate, w_ffn_up, w_ffn_down.astype(BF16)
    xc = ctx.reshape(1, -1, d)
    for l in range(depth):
        last = l == depth - 1
        mod = mods[l, :b].reshape(b, 1, 6 * d)
        modc = mods[l, b].reshape(1, 1, 6 * d)
        sh1, sc1, g1, sh2, sc2, g2 = jnp.split(mod, 6, axis=-1)
        csh1, csc1, cg1, csh2, csc2, cg2 = jnp.split(modc, 6, axis=-1)
        h = _norm_call(x, norm1_g[l], sh1, sc1)
        hc = _norm_call(xc, norm1_g[l], csh1, csc1)
        i = l // 2
        if l % 2 == 0:
            ops, ctx_ops = _even_mixer(h, hc, w_in_ab, i, w_four[i], na_rpb[i], not last)
            w_out = w_out_ab
        else:
            ops, ctx_ops = _odd_mixer(h, hc, w_in_cd[i], w_pool[i], pool_scale[i], mla_gq[i], mla_gkv[i],
                                      w_uq[i], w_ukv[i], not last)
            w_out = w_out_cd
        x = _mm_residual(ops, w_out, x, g1, f"out_proj_{l}", i)
        x = _ffn(x, norm2_g[l], sh2, sc2, g2, wg, wu, wd, l, f"{l}")
        if not last:
            xc = _mm_residual(ctx_ops, w_out, xc, cg1, f"out_proj_ctx_{l}", i)
            xc = _ffn(xc, norm2_g[l], csh2, csc2, cg2, wg, wu, wd, l, f"ctx_{l}")
    return _norm_call(x, final_g, out_dtype=F32)
```

```python
import functools
import math

import numpy as np
import jax
import jax.numpy as jnp
from jax import lax
from jax.experimental import pallas as pl
from jax.experimental.pallas import tpu as pltpu

GRID_W = 64
HEAD_DIM = 128
EPS = 1e-6
NEG_INF = -1e30
NA_WIN_H = 8
NA_WIN_W = 16
POOL_WINDOWS = (2, 4, 8, 16)
QK_NOPE_DIM = 128
QK_ROPE_DIM = 64
V_HEAD_DIM = 128
ROPE_BASE = 10000.0
LOG2E = 1.4426950408889634

V7X_LANES = 128
V7X_VMEM_BYTES = 64 * 1024 * 1024
V7X_VMEM_CAP = 56 * 1024 * 1024
NA_ROWS_PER_TILE = 4

BF16 = jnp.bfloat16
F32 = jnp.float32


def _pick_tile(n, target, mult=16):
    if n <= target:
        return n
    for t in range(target, 0, -1):
        if n % t == 0 and t % mult == 0:
            return t
    return n


def _nbytes(shape, dtype):
    return int(np.prod(shape)) * jnp.dtype(dtype).itemsize


def _params(block_bytes, scratch_bytes=0, ndims=3):
    est = 2 * block_bytes + scratch_bytes
    limit = int(min(max(est * 5 // 4 + (4 << 20), 16 << 20), V7X_VMEM_CAP))
    return pltpu.CompilerParams(dimension_semantics=("arbitrary",) * ndims, vmem_limit_bytes=limit)


def _mod_kernel(c_ref, w_ref, b_ref, o_ref):
    cv = c_ref[...]
    s = cv * (1.0 / (1.0 + jnp.exp(-cv)))
    acc = jnp.dot(s.astype(BF16), w_ref[...].astype(BF16), preferred_element_type=F32)
    o_ref[...] = acc + b_ref[...]


def _mod_call(cvec, w_mod, b_mod):
    depth, d, n = w_mod.shape
    r = cvec.shape[0]
    tn = _pick_tile(n, 1024, V7X_LANES)
    blocks = _nbytes((r, d), F32) + _nbytes((d, tn), F32) + 2 * _nbytes((r, tn), F32)
    return pl.pallas_call(
        _mod_kernel,
        grid=(depth, n // tn),
        in_specs=[
            pl.BlockSpec((r, d), lambda l, j: (0, 0)),
            pl.BlockSpec((None, d, tn), lambda l, j: (l, 0, j)),
            pl.BlockSpec((None, 1, tn), lambda l, j: (l, 0, j)),
        ],
        out_specs=pl.BlockSpec((None, r, tn), lambda l, j: (l, 0, j)),
        out_shape=jax.ShapeDtypeStruct((depth, r, n), F32),
        compiler_params=_params(blocks, _nbytes((d, tn), BF16), 2),
        name="mod_vectors",
    )(cvec, w_mod, b_mod.reshape(depth, 1, n))


def _norm_mod_kernel(x_ref, g_ref, sh_ref, sc_ref, o_ref):
    xf = x_ref[...]
    y = xf * lax.rsqrt(jnp.mean(xf * xf, axis=-1, keepdims=True) + EPS) * g_ref[...]
    o_ref[...] = (y * (1.0 + sc_ref[...]) + sh_ref[...]).astype(o_ref.dtype)


def _norm_kernel(x_ref, g_ref, o_ref):
    xf = x_ref[...]
    y = xf * lax.rsqrt(jnp.mean(xf * xf, axis=-1, keepdims=True) + EPS) * g_ref[...]
    o_ref[...] = y.astype(o_ref.dtype)


def _norm_call(x, g, shift=None, scale=None, out_dtype=BF16):
    b, s, d = x.shape
    ts = _pick_tile(s, 512)
    xspec = pl.BlockSpec((None, ts, d), lambda bi, i: (bi, i, 0))
    gspec = pl.BlockSpec((1, d), lambda bi, i: (0, 0))
    mspec = pl.BlockSpec((None, 1, d), lambda bi, i: (bi, 0, 0))
    blocks = _nbytes((ts, d), F32) + _nbytes((ts, d), out_dtype) + 3 * _nbytes((1, d), F32)
    common = dict(
        grid=(b, s // ts),
        out_specs=xspec,
        out_shape=jax.ShapeDtypeStruct((b, s, d), out_dtype),
        compiler_params=_params(blocks, 2 * _nbytes((ts, d), F32), 2),
    )
    if shift is None:
        return pl.pallas_call(_norm_kernel, in_specs=[xspec, gspec], name="rmsnorm", **common)(
            x, g.reshape(1, d))
    return pl.pallas_call(_norm_mod_kernel, in_specs=[xspec, gspec, mspec, mspec], name="rmsnorm_modulate",
                          **common)(x, g.reshape(1, d), shift, scale)


def _mm_kernel(*refs, n_ops, n_extra, epilogue):
    a_refs = refs[:n_ops]
    w_refs = refs[n_ops:2 * n_ops]
    e_refs = refs[2 * n_ops:2 * n_ops + n_extra]
    o_refs = refs[2 * n_ops + n_extra:]
    acc = None
    for a_ref, w_ref in zip(a_refs, w_refs):
        part = jnp.dot(a_ref[...], w_ref[...], preferred_element_type=F32)
        acc = part if acc is None else acc + part
    epilogue(acc, e_refs, o_refs)


def _mm_call(a_ops, w_ops, extras, epilogue, outs, grid, tm, tn, name, tmp_bytes=0):
    arrays = [a for a, _ in a_ops] + [w for w, _ in w_ops] + [e for e, _ in extras]
    in_specs = [s for _, s in a_ops] + [s for _, s in w_ops] + [s for _, s in extras]
    blocks = 0
    for arr, spec in a_ops + w_ops + extras:
        blocks += _nbytes([d for d in spec.block_shape if d is not None], arr.dtype)
    for sds, spec in outs:
        blocks += _nbytes([d for d in spec.block_shape if d is not None], sds.dtype)
    kern = functools.partial(_mm_kernel, n_ops=len(a_ops), n_extra=len(extras), epilogue=epilogue)
    res = pl.pallas_call(
        kern,
        grid=grid,
        in_specs=in_specs,
        out_specs=[s for _, s in outs],
        out_shape=[o for o, _ in outs],
        compiler_params=_params(blocks, 3 * _nbytes((tm, tn), F32) + tmp_bytes, len(grid)),
        name=name,
    )(*arrays)
    return res


def _a_spec(tm, k, col_blk=0):
    return pl.BlockSpec((None, tm, k), lambda b, i, j: (b, i, col_blk))


def _w3(w):
    return w if w.ndim == 3 else w[None]


def _w_spec(k, tn, layer=0, row_blk=0, col_off=0):
    return pl.BlockSpec((None, k, tn), lambda b, i, j: (layer, row_blk, j + col_off))


def _tile_spec(tm, tn):
    return pl.BlockSpec((None, tm, tn), lambda b, i, j: (b, i, j))


def _vec_spec(tn):
    return pl.BlockSpec((None, 1, tn), lambda b, i, j: (b, 0, j))


def _ep_store(acc, e_refs, o_refs):
    o_refs[0][...] = acc.astype(o_refs[0].dtype)


def _col_tile(n, col0, target, mult):
    return _pick_tile(math.gcd(n, col0) if col0 else n, target, mult)


def _mm_plain(a, w, out_dtype, name, layer=0, col0=0, n=None, tm_target=1024, tn_target=512):
    b, s, k = a.shape
    w = _w3(w)
    n = w.shape[2] - col0 if n is None else n
    tm = _pick_tile(s, tm_target)
    tn = _col_tile(n, col0, tn_target, V7X_LANES)
    return _mm_call([(a, _a_spec(tm, k))], [(w, _w_spec(k, tn, layer, 0, col0 // tn))], [], _ep_store,
                    [(jax.ShapeDtypeStruct((b, s, n), out_dtype), _tile_spec(tm, tn))],
                    (b, s // tm, n // tn), tm, tn, name)[0]


def _ep_chunks(acc, e_refs, o_refs, *, width):
    o_ref = o_refs[0]
    for c in range(acc.shape[1] // width):
        o_ref[c] = acc[:, c * width:(c + 1) * width].astype(o_ref.dtype)


def _mm_chunked(a, w, name, layer=0, col0=0, n=None, width=HEAD_DIM, tm_target=1024, tn_target=512):
    b, s, k = a.shape
    w = _w3(w)
    n = w.shape[2] - col0 if n is None else n
    tm = _pick_tile(s, tm_target)
    tn = _col_tile(n, col0, tn_target, width)
    cpt = tn // width
    ospec = pl.BlockSpec((None, cpt, tm, width), lambda bi, i, j: (bi, j, i, 0))
    return _mm_call([(a, _a_spec(tm, k))], [(w, _w_spec(k, tn, layer, 0, col0 // tn))], [],
                    functools.partial(_ep_chunks, width=width),
                    [(jax.ShapeDtypeStruct((b, n // width, s, width), BF16), ospec)],
                    (b, s // tm, n // tn), tm, tn, name)[0]


def _ep_residual(acc, e_refs, o_refs):
    x_ref, g_ref = e_refs
    o_refs[0][...] = x_ref[...] + g_ref[...] * acc


def _mm_residual(a_list, w, x, gate, name, layer=0, tm_target=1024, tn_target=512):
    b, s, n = x.shape
    w = _w3(w)
    tm = _pick_tile(s, tm_target)
    tn = _pick_tile(n, tn_target, V7X_LANES)
    kblk = functools.reduce(math.gcd, [a.shape[2] for a in a_list])
    a_ops, w_ops = [], []
    for a in a_list:
        for cb in range(a.shape[2] // kblk):
            a_ops.append((a, _a_spec(tm, kblk, cb)))
            w_ops.append((w, _w_spec(kblk, tn, layer, len(w_ops))))
    extras = [(x, _tile_spec(tm, tn)), (gate, _vec_spec(tn))]
    return _mm_call(a_ops, w_ops, extras, _ep_residual,
                    [(jax.ShapeDtypeStruct((b, s, n), F32), _tile_spec(tm, tn))],
                    (b, s // tm, n // tn), tm, tn, name)[0]


def _swiglu_kernel(a_ref, wg_ref, wu_ref, o_ref, wgb_ref, wub_ref):
    @pl.when((pl.program_id(1) == 0) & (pl.program_id(2) == 0))
    def _():
        wgb_ref[...] = wg_ref[...].astype(BF16)
        wub_ref[...] = wu_ref[...].astype(BF16)

    a = a_ref[...]
    g = jnp.dot(a, wgb_ref[...], preferred_element_type=F32)
    u = jnp.dot(a, wub_ref[...], preferred_element_type=F32)
    o_ref[...] = (g * (1.0 / (1.0 + jnp.exp(-g))) * u).astype(o_ref.dtype)


def _swiglu_up(a, wg, wu, layer, name, tm_target=2048, tn_target=512):
    b, s, k = a.shape
    n = wg.shape[2]
    tm = _pick_tile(s, tm_target)
    tn = _pick_tile(n, tn_target, V7X_LANES)
    blocks = _nbytes((tm, k), BF16) + 2 * _nbytes((k, tn), F32) + _nbytes((tm, tn), BF16)
    wspec = pl.BlockSpec((None, k, tn), lambda j, bi, i: (layer, 0, j))
    return pl.pallas_call(
        _swiglu_kernel,
        grid=(n // tn, b, s // tm),
        in_specs=[pl.BlockSpec((None, tm, k), lambda j, bi, i: (bi, i, 0)), wspec, wspec],
        out_specs=pl.BlockSpec((None, tm, tn), lambda j, bi, i: (bi, i, j)),
        out_shape=jax.ShapeDtypeStruct((b, s, n), BF16),
        scratch_shapes=[pltpu.VMEM((k, tn), BF16), pltpu.VMEM((k, tn), BF16)],
        compiler_params=_params(blocks, 2 * _nbytes((k, tn), BF16) + 4 * _nbytes((tm, tn), F32), 3),
        name=name,
    )(a, wg, wu)


def _ep_rmsnorm(acc, e_refs, o_refs):
    g_ref, = e_refs
    y = acc * lax.rsqrt(jnp.mean(acc * acc, axis=-1, keepdims=True) + EPS) * g_ref[...]
    o_refs[0][...] = y.astype(o_refs[0].dtype)


def _mm_rmsnorm(a, w, g, name, tm_target=1024):
    b, s, k = a.shape
    n = w.shape[1]
    tm = _pick_tile(s, tm_target)
    gspec = pl.BlockSpec((1, n), lambda bi, i, j: (0, 0))
    return _mm_call([(a, _a_spec(tm, k))], [(_w3(w), _w_spec(k, n))], [(g.reshape(1, n), gspec)], _ep_rmsnorm,
                    [(jax.ShapeDtypeStruct((b, s, n), BF16), _tile_spec(tm, n))],
                    (b, s // tm, 1), tm, n, name)[0]


def _rope_pair(hi, tab):
    z = hi * tab
    return z + pltpu.roll(z, QK_ROPE_DIM, axis=1)


def _ep_krope(acc, e_refs, o_refs):
    tab_ref, = e_refs
    r = _rope_pair(acc, tab_ref[...])
    lane = lax.broadcasted_iota(jnp.int32, r.shape, 1)
    o_refs[0][...] = jnp.where(lane < QK_ROPE_DIM, r, 0.0).astype(o_refs[0].dtype)


def _mm_krope(a, w, tab, name, tm_target=1024):
    b, s, k = a.shape
    n = w.shape[1]
    tm = _pick_tile(s, tm_target)
    tspec = pl.BlockSpec((tm, n), lambda bi, i, j: (i, 0))
    return _mm_call([(a, _a_spec(tm, k))], [(_w3(w), _w_spec(k, n))], [(tab, tspec)], _ep_krope,
                    [(jax.ShapeDtypeStruct((b, s, n), BF16), _tile_spec(tm, n))],
                    (b, s // tm, 1), tm, n, name)[0]


def _ep_qrope(acc, e_refs, o_refs, *, heads):
    tab_ref, = e_refs
    o_ref = o_refs[0]
    tab = tab_ref[...]
    blk = QK_NOPE_DIM + 2 * QK_ROPE_DIM
    for h in range(heads):
        o_ref[h, :, 0:QK_NOPE_DIM] = acc[:, h * blk:h * blk + QK_NOPE_DIM].astype(o_ref.dtype)
        hi = acc[:, h * blk + QK_NOPE_DIM:(h + 1) * blk]
        o_ref[h, :, QK_NOPE_DIM:blk] = _rope_pair(hi, tab).astype(o_ref.dtype)


def _mm_qrope(a, w, tab, name, tm_target=1024, heads_per_tile=6):
    b, s, k = a.shape
    blk = QK_NOPE_DIM + 2 * QK_ROPE_DIM
    nh = w.shape[1] // blk
    hpt = math.gcd(nh, heads_per_tile)
    tn = hpt * blk
    tm = _pick_tile(s, tm_target)
    tspec = pl.BlockSpec((tm, 2 * QK_ROPE_DIM), lambda bi, i, j: (i, 0))
    ospec = pl.BlockSpec((None, hpt, tm, blk), lambda bi, i, j: (bi, j, i, 0))
    return _mm_call([(a, _a_spec(tm, k))], [(_w3(w), _w_spec(k, tn))], [(tab, tspec)],
                    functools.partial(_ep_qrope, heads=hpt),
                    [(jax.ShapeDtypeStruct((b, nh, s, blk), BF16), ospec)],
                    (b, s // tm, nh // hpt), tm, tn, name)[0]


def _ep_kv(acc, e_refs, o_refs, *, heads):
    kr_ref, = e_refs
    k_ref, v_ref = o_refs
    kr = kr_ref[...]
    blk = QK_NOPE_DIM + V_HEAD_DIM
    for h in range(heads):
        k_ref[h, :, 0:QK_NOPE_DIM] = acc[:, h * blk:h * blk + QK_NOPE_DIM].astype(k_ref.dtype)
        k_ref[h, :, QK_NOPE_DIM:QK_NOPE_DIM + kr.shape[1]] = kr
        v_ref[h] = acc[:, h * blk + QK_NOPE_DIM:(h + 1) * blk].astype(v_ref.dtype)


def _mm_kv(a, w, krot, name, tm_target=1024, heads_per_tile=6):
    b, s, k = a.shape
    blk = QK_NOPE_DIM + V_HEAD_DIM
    nh = w.shape[1] // blk
    hpt = math.gcd(nh, heads_per_tile)
    tn = hpt * blk
    tm = _pick_tile(s, tm_target)
    kd = QK_NOPE_DIM + krot.shape[2]
    krspec = pl.BlockSpec((None, tm, krot.shape[2]), lambda bi, i, j: (bi, i, 0))
    kspec = pl.BlockSpec((None, hpt, tm, kd), lambda bi, i, j: (bi, j, i, 0))
    vspec = pl.BlockSpec((None, hpt, tm, V_HEAD_DIM), lambda bi, i, j: (bi, j, i, 0))
    return _mm_call([(a, _a_spec(tm, k))], [(_w3(w), _w_spec(k, tn))], [(krot, krspec)],
                    functools.partial(_ep_kv, heads=hpt),
                    [(jax.ShapeDtypeStruct((b, nh, s, kd), BF16), kspec),
                     (jax.ShapeDtypeStruct((b, nh, s, V_HEAD_DIM), BF16), vspec)],
                    (b, s // tm, nh // hpt), tm, tn, name)


def _dft_matrix(n):
    m = jnp.arange(n, dtype=jnp.int32)
    side = 1
    while side * side < n:
        side *= 2
    if side * side != n:
        k = jnp.arange(n, dtype=jnp.int32)
        ang = (2.0 * math.pi / n) * ((k[:, None] * m[None, :]) % n).astype(F32)
        return jnp.concatenate([jnp.cos(ang), jnp.sin(ang)], axis=0).astype(BF16)[None]
    a = jnp.arange(side, dtype=jnp.int32)
    ang_x = (2.0 * math.pi / side) * ((a[:, None] * m[None, :]) % side).astype(F32)
    ang_y = (2.0 * math.pi / n) * ((a[:, None] * m[None, :]) % n).astype(F32)
    cx, sx = jnp.cos(ang_x), jnp.sin(ang_x)
    cy, sy = jnp.cos(ang_y)[None], jnp.sin(ang_y)[None]
    p = jnp.concatenate([cx, sx], axis=0)[:, None, :]
    q = jnp.concatenate([sx, -cx], axis=0)[:, None, :]
    return (p * cy - q * sy).astype(BF16).reshape(1, 2 * n, n)


def _dft_positions_dense(u):
    b, s, w = u.shape
    mat = _dft_matrix(s)
    tm = _pick_tile(2 * s, 1024)
    tn = _pick_tile(w, 512, V7X_LANES)
    aspec = pl.BlockSpec((None, tm, s), lambda bi, i, j: (0, i, 0))
    wspec = pl.BlockSpec((None, s, tn), lambda bi, i, j: (bi, 0, j))
    return _mm_call([(mat, aspec)], [(u, wspec)], [], _ep_store,
                    [(jax.ShapeDtypeStruct((b, 2 * s, w), BF16), _tile_spec(tm, tn))],
                    (b, 2 * s // tm, w // tn), tm, tn, "fourier_positions")[0]


DFT_ROWS_PER_STEP = 8


def _dft_stage1_kernel(x_ref, a_ref, tc_ref, ts_ref, o_ref, *, side, width, n2_per_step):
    g = jnp.dot(a_ref[...], x_ref[...], preferred_element_type=F32)
    gr, gs = g[0:side], g[side:2 * side]
    for n2 in range(n2_per_step):
        tcs = tc_ref[:, n2 * V7X_LANES:(n2 + 1) * V7X_LANES]
        tss = ts_ref[:, n2 * V7X_LANES:(n2 + 1) * V7X_LANES]
        for c0 in range(n2 * width, (n2 + 1) * width, V7X_LANES):
            cols = slice(c0, c0 + V7X_LANES)
            o_ref[0:side, cols] = (gr[:, cols] * tcs - gs[:, cols] * tss).astype(o_ref.dtype)
            o_ref[side:2 * side, cols] = (gr[:, cols] * tss + gs[:, cols] * tcs).astype(o_ref.dtype)


def _dft_stage2_kernel(h_ref, m_ref, o_ref, scr_ref, *, side):
    rows = h_ref.shape[1]
    ngrp = scr_ref.shape[0]
    for j in range(rows):
        hcat = jnp.concatenate([h_ref[0, j], h_ref[1, j]], axis=0)
        res = jnp.dot(m_ref[...], hcat, preferred_element_type=F32)
        for c in range(ngrp):
            cols = slice(c * V7X_LANES, (c + 1) * V7X_LANES)
            scr_ref[c, pl.ds(j, side, stride=rows), :] = res[0:side, cols]
            scr_ref[c, pl.ds(side * rows + j, side, stride=rows), :] = res[side:2 * side, cols]
    for c in range(ngrp):
        cols = slice(c * V7X_LANES, (c + 1) * V7X_LANES)
        for p in range(2):
            o_ref[p, :, :, cols] = scr_ref[c, p * side * rows:(p + 1) * side * rows, :].reshape(side, rows, V7X_LANES)


def _dft_positions(u):
    b, s, w = u.shape
    side = math.isqrt(s)
    if side * side != s or side % DFT_ROWS_PER_STEP or w % V7X_LANES:
        return _dft_positions_dense(u)
    idx = np.arange(side)
    ang = 2.0 * np.pi * ((idx[:, None] * idx[None, :]) % side) / side
    cs, sn = np.cos(ang), np.sin(ang)
    a1 = jnp.asarray(np.concatenate([cs, sn], axis=0), dtype=F32).astype(BF16)
    m2 = jnp.asarray(np.block([[cs, -sn], [sn, cs]]), dtype=F32).astype(BF16)
    tw = 2.0 * np.pi * (idx[:, None] * idx[None, :]) / s
    tcos = jnp.repeat(jnp.asarray(np.cos(tw), dtype=F32), V7X_LANES, axis=1)
    tsin = jnp.repeat(jnp.asarray(np.sin(tw), dtype=F32), V7X_LANES, axis=1)
    n2s = DFT_ROWS_PER_STEP
    tc = n2s * w
    blocks1 = _nbytes((side, tc), BF16) + _nbytes((2 * side, side), BF16) + 2 * _nbytes((side, n2s * V7X_LANES), F32) \
        + _nbytes((2 * side, tc), BF16)
    h = pl.pallas_call(
        functools.partial(_dft_stage1_kernel, side=side, width=w, n2_per_step=n2s),
        grid=(b, side // n2s),
        in_specs=[
            pl.BlockSpec((None, side, tc), lambda bi, j: (bi, 0, j)),
            pl.BlockSpec((2 * side, side), lambda bi, j: (0, 0)),
            pl.BlockSpec((side, n2s * V7X_LANES), lambda bi, j: (0, j)),
            pl.BlockSpec((side, n2s * V7X_LANES), lambda bi, j: (0, j)),
        ],
        out_specs=pl.BlockSpec((None, 2 * side, tc), lambda bi, j: (bi, 0, j)),
        out_shape=jax.ShapeDtypeStruct((b, 2 * side, side * w), BF16),
        compiler_params=_params(blocks1, 3 * _nbytes((2 * side, tc), F32), 2),
        name="fourier_positions_stage1",
    )(u.reshape(b, side, side * w), a1, tcos, tsin)
    rows = DFT_ROWS_PER_STEP
    blocks2 = _nbytes((2, rows, side, w), BF16) + _nbytes((2 * side, 2 * side), BF16) + _nbytes((2, side, rows, w), F32)
    f2 = pl.pallas_call(
        functools.partial(_dft_stage2_kernel, side=side),
        grid=(b, side // rows),
        in_specs=[
            pl.BlockSpec((None, 2, rows, side, w), lambda bi, i: (bi, 0, i, 0, 0)),
            pl.BlockSpec((2 * side, 2 * side), lambda bi, i: (0, 0)),
        ],
        out_specs=pl.BlockSpec((None, 2, side, rows, w), lambda bi, i: (bi, 0, 0, i, 0)),
        out_shape=jax.ShapeDtypeStruct((b, 2, side, side, w), F32),
        scratch_shapes=[pltpu.VMEM((w // V7X_LANES, 2 * side * rows, V7X_LANES), F32)],
        compiler_params=_params(blocks2, _nbytes((2 * side * rows, w), F32) + 4 * _nbytes((2 * side, w), F32), 2),
        name="fourier_positions_stage2",
    )(h.reshape(b, 2, side, side, w), m2)
    return f2.reshape(b, 2 * s, w)


def _four_channel_kernel(fr_ref, fi_ref, cs_ref, w_ref, o_ref, *, norm):
    gd = fr_ref.shape[1]
    f = jnp.dot(fr_ref[...].astype(BF16), cs_ref[0:gd, :], preferred_element_type=F32)
    f = f + jnp.dot(fi_ref[...].astype(BF16), cs_ref[gd:2 * gd, :], preferred_element_type=F32)
    f = (f * norm).astype(BF16)
    o_ref[...] = jnp.dot(f, w_ref[...], preferred_element_type=F32).astype(o_ref.dtype)


def _four_channels(f2, w_four):
    b, s2, w = f2.shape
    s = s2 // 2
    g, gd, _ = w_four.shape
    idx = np.arange(gd)
    ang = 2.0 * np.pi * ((idx[:, None] * idx[None, :]) % gd) / gd
    cs = jnp.asarray(np.concatenate([np.cos(ang), -np.sin(ang)], axis=0), dtype=F32).astype(BF16)
    ts = _pick_tile(s, 1024)
    nblk = s // ts
    blocks = 2 * _nbytes((ts, gd), f2.dtype) + _nbytes((ts, gd), BF16) + _nbytes((2 * gd, gd), BF16) \
        + _nbytes((gd, gd), BF16)
    return pl.pallas_call(
        functools.partial(_four_channel_kernel, norm=1.0 / math.sqrt(s * gd)),
        grid=(b, nblk, g),
        in_specs=[
            pl.BlockSpec((None, ts, gd), lambda bi, i, j: (bi, i, j)),
            pl.BlockSpec((None, ts, gd), lambda bi, i, j: (bi, nblk + i, j)),
            pl.BlockSpec((2 * gd, gd), lambda bi, i, j: (0, 0)),
            pl.BlockSpec((None, gd, gd), lambda bi, i, j: (j, 0, 0)),
        ],
        out_specs=pl.BlockSpec((None, ts, gd), lambda bi, i, j: (bi, i, j)),
        out_shape=jax.ShapeDtypeStruct((b, s, w), BF16),
        compiler_params=_params(blocks, 3 * _nbytes((ts, gd), F32), 3),
        name="fourier_channels",
    )(f2, f2, cs, w_four.astype(BF16))


def _na_plan(rows):
    rpt = NA_ROWS_PER_TILE
    kh = min(NA_WIN_H, rows)
    wrows = min(kh + rpt - 1, rows)
    cases, ws_list, case_list = {}, [], []
    for t in range(rows // rpt):
        r0 = t * rpt
        ws = int(np.clip(r0 - kh // 2, 0, rows - wrows))
        plan = []
        for r in range(r0, r0 + rpt):
            rs = int(np.clip(r - kh // 2, 0, rows - kh))
            plan.append(tuple((ws + w) - r + NA_WIN_H - 1 if rs <= ws + w < rs + kh else None
                              for w in range(wrows)))
        plan = tuple(plan)
        cases.setdefault(plan, len(cases))
        ws_list.append(ws)
        case_list.append(cases[plan])
    return wrows, ws_list, case_list, list(cases)


def _na_block_table(rpb):
    qcol = np.arange(GRID_W)[:, None]
    kcol = np.arange(GRID_W)[None, :]
    cstart = np.clip(qcol - NA_WIN_W // 2, 0, GRID_W - NA_WIN_W)
    col_ok = (kcol >= cstart) & (kcol < cstart + NA_WIN_W)
    cidx = np.clip(kcol - qcol + NA_WIN_W - 1, 0, 2 * NA_WIN_W - 2)
    csel = jnp.asarray(np.eye(rpb.shape[2], dtype=np.float32)[cidx])
    blk = jnp.einsum("hrs,qks->hrqk", rpb, csel, precision=lax.Precision.HIGHEST)
    blk = jnp.where(jnp.asarray(col_ok), blk * LOG2E, NEG_INF)
    return jnp.concatenate([blk, blk], axis=-1)


def _ones_column(rows, width):
    lane = lax.broadcasted_iota(jnp.int32, (rows, width), 1)
    return jnp.where(lane == 0, 1.0, 0.0).astype(BF16)


def _na_kernel(q_ref, k_ref, v_ref, kc_ref, vc_ref, blk_ref, o_ref, bias_ref, vx_ref, vcx_ref,
               *, tiles, tq, tk, scale, plans):
    d = v_ref.shape[1]

    @pl.when(pl.program_id(1) == 0)
    def _():
        for c, plan in enumerate(plans):
            for qr, row in enumerate(plan):
                for kr, ridx in enumerate(row):
                    half = slice((kr % 2) * GRID_W, (kr % 2 + 1) * GRID_W)
                    if ridx is None:
                        blk = jnp.full((GRID_W, GRID_W), NEG_INF, F32)
                    else:
                        blk = blk_ref[ridx][:, half]
                    bias_ref[c, qr * GRID_W:(qr + 1) * GRID_W, kr * GRID_W:(kr + 1) * GRID_W] = blk

    vx_ref[:, 0:d] = v_ref[...]
    vx_ref[:, d:2 * d] = _ones_column(v_ref.shape[0], d)
    vcx_ref[:, 0:d] = vc_ref[...]
    vcx_ref[:, d:2 * d] = _ones_column(vc_ref.shape[0], d)
    kc = kc_ref[...]
    dn = (((1,), (1,)), ((), ()))
    c2 = scale * LOG2E

    for t, (ws, case) in enumerate(tiles):
        q0, k0 = t * tq, ws * GRID_W
        q = q_ref[q0:q0 + tq, :]
        s_loc = lax.dot_general(q, k_ref[k0:k0 + tk, :], dn, preferred_element_type=F32) * c2 + bias_ref[case]
        s_ctx = lax.dot_general(q, kc, dn, preferred_element_type=F32) * c2
        m = jnp.maximum(jnp.max(s_loc, axis=-1, keepdims=True), jnp.max(s_ctx, axis=-1, keepdims=True))
        p_loc = jnp.exp2(s_loc - m).astype(BF16)
        p_ctx = jnp.exp2(s_ctx - m).astype(BF16)
        acc = jnp.dot(p_loc, vx_ref[k0:k0 + tk, :], preferred_element_type=F32)
        acc = acc + jnp.dot(p_ctx, vcx_ref[...], preferred_element_type=F32)
        o_ref[q0:q0 + tq, :] = (acc[:, 0:d] / acc[:, d:d + 1]).astype(o_ref.dtype)


def _na_attention(qkv, qkvc, rpb):
    b, c3, s, d = qkv.shape
    nh = c3 // 3
    lc = qkvc.shape[2] // b
    rows = s // GRID_W
    wrows, ws_list, case_list, plans = _na_plan(rows)
    blk = _na_block_table(rpb)
    nrel = blk.shape[1]
    tq = NA_ROWS_PER_TILE * GRID_W
    tk = wrows * GRID_W
    blocks = 3 * _nbytes((s, d), BF16) + 2 * _nbytes((lc, d), BF16) + _nbytes((nrel, GRID_W, 2 * GRID_W), F32) \
        + _nbytes((s, d), BF16)
    scratch = _nbytes((len(plans), tq, tk), F32) + _nbytes((s + lc, 2 * d), BF16)
    return pl.pallas_call(
        functools.partial(_na_kernel, tiles=list(zip(ws_list, case_list)), tq=tq, tk=tk, scale=d ** -0.5,
                          plans=plans),
        grid=(nh, b),
        in_specs=[
            pl.BlockSpec((None, None, s, d), lambda h, bi: (bi, h, 0, 0)),
            pl.BlockSpec((None, None, s, d), lambda h, bi: (bi, nh + h, 0, 0)),
            pl.BlockSpec((None, None, s, d), lambda h, bi: (bi, 2 * nh + h, 0, 0)),
            pl.BlockSpec((None, None, lc, d), lambda h, bi: (0, nh + h, bi, 0)),
            pl.BlockSpec((None, None, lc, d), lambda h, bi: (0, 2 * nh + h, bi, 0)),
            pl.BlockSpec((None, nrel, GRID_W, 2 * GRID_W), lambda h, bi: (h, 0, 0, 0)),
        ],
        out_specs=pl.BlockSpec((None, s, d), lambda h, bi: (bi, 0, h)),
        out_shape=jax.ShapeDtypeStruct((b, s, nh * d), BF16),
        scratch_shapes=[pltpu.VMEM((len(plans), tq, tk), F32), pltpu.VMEM((s, 2 * d), BF16),
                        pltpu.VMEM((lc, 2 * d), BF16)],
        compiler_params=_params(blocks, scratch + 8 * _nbytes((tq, tk + lc), F32), 2),
        name="neighbourhood_attention",
    )(qkv, qkv, qkv, qkvc, qkvc, blk)


def _dense_attn_kernel(*refs, scale, n_src, chunks):
    q_ref = refs[0]
    k_refs = refs[1:1 + n_src]
    v_refs = refs[1 + n_src:1 + 2 * n_src]
    o_ref = refs[1 + 2 * n_src]
    vx_refs = refs[2 + 2 * n_src:]
    dv = v_refs[0].shape[1]

    @pl.when(pl.program_id(2) == 0)
    def _():
        for v_ref, vx_ref in zip(v_refs, vx_refs):
            vx_ref[:, 0:dv] = v_ref[...]
            vx_ref[:, dv:2 * dv] = _ones_column(v_ref.shape[0], dv)

    q = q_ref[...]
    c2 = scale * LOG2E
    m = acc = None
    for src, start, size in chunks:
        s = lax.dot_general(q, k_refs[src][start:start + size, :], (((1,), (1,)), ((), ())),
                            preferred_element_type=F32)
        m_chunk = jnp.max(s, axis=-1, keepdims=True)
        m_new = m_chunk if m is None else jnp.maximum(m, m_chunk)
        p = jnp.exp2((s - m_new) * c2)
        pv = jnp.dot(p.astype(BF16), vx_refs[src][start:start + size, :], preferred_element_type=F32)
        acc = pv if acc is None else acc * jnp.exp2((m - m_new) * c2) + pv
        m = m_new
    o_ref[...] = (acc[:, 0:dv] / acc[:, dv:dv + 1]).astype(o_ref.dtype)


def _head_major_spec(arr, batch, rows, off, tiled):
    per_batch = arr.shape[2] * arr.shape[0] // batch // rows
    if arr.shape[0] == 1:
        return pl.BlockSpec((None, None, rows, arr.shape[3]),
                            lambda bi, h, i: (0, off + h, bi * per_batch + (i if tiled else 0), 0))
    return pl.BlockSpec((None, None, rows, arr.shape[3]), lambda bi, h, i: (bi, off + h, i if tiled else 0, 0))


def _dense_attention(q, kv_list, nh, q_off, scale, name, batch, tq_target=1024, chunk_target=512):
    dk = q.shape[3]
    sq = q.shape[0] * q.shape[2] // batch
    dv = kv_list[0][1].shape[3]
    tq = _pick_tile(sq, tq_target)
    nq = sq // tq
    chunks, arrays_k, arrays_v, scratch = [], [], [], []
    blocks = _nbytes((tq, dk), BF16) + _nbytes((tq, dv), BF16)
    tmp = 4 * _nbytes((tq, chunk_target), F32) + 2 * _nbytes((tq, 2 * dv), F32)
    kspecs, vspecs = [], []
    for src, (k, v, k_off, v_off) in enumerate(kv_list):
        sk = k.shape[0] * k.shape[2] // batch
        chunks += [(src, st, min(chunk_target, sk - st)) for st in range(0, sk, chunk_target)]
        kspecs.append(_head_major_spec(k, batch, sk, k_off, False))
        vspecs.append(_head_major_spec(v, batch, sk, v_off, False))
        arrays_k.append(k)
        arrays_v.append(v)
        scratch.append(pltpu.VMEM((sk, 2 * dv), BF16))
        blocks += _nbytes((sk, dk), BF16) + _nbytes((sk, dv), BF16)
        tmp += _nbytes((sk, 2 * dv), BF16)
    if q.shape[0] == 1:
        out_spec = pl.BlockSpec((None, tq, dv), lambda bi, h, i: (0, bi * nq + i, h))
    else:
        out_spec = pl.BlockSpec((None, tq, dv), lambda bi, h, i: (bi, i, h))
    return pl.pallas_call(
        functools.partial(_dense_attn_kernel, scale=scale, n_src=len(kv_list), chunks=chunks),
        grid=(batch, nh, nq),
        in_specs=[_head_major_spec(q, batch, tq, q_off, True)] + kspecs + vspecs,
        out_specs=out_spec,
        out_shape=jax.ShapeDtypeStruct((q.shape[0], q.shape[2], nh * dv), BF16),
        scratch_shapes=scratch,
        compiler_params=_params(blocks, tmp, 3),
        name=name,
    )(q, *arrays_k, *arrays_v)


def _pool_kernel(u_ref, w_ref, sc_ref, o_ref, *, windows):
    n, gd = u_ref.shape[0], w_ref.shape[1]
    t = lax.broadcasted_iota(jnp.int32, (n, gd), 0)
    for g, window in enumerate(windows):
        cols = slice(g * gd, (g + 1) * gd)
        u = u_ref[:, cols]
        half = window // 2
        ssum = u
        for j in range(1, half + 1):
            ssum = ssum + jnp.where(t >= j, pltpu.roll(u, j, axis=0), 0.0)
        for j in range(1, half):
            ssum = ssum + jnp.where(t < n - j, pltpu.roll(u, n - j, axis=0), 0.0)
        cnt = jnp.minimum(t + half, n) - jnp.maximum(t - half, 0)
        p = ssum / cnt.astype(F32) - u
        y = jnp.dot(p.astype(BF16), w_ref[g], preferred_element_type=F32)
        o_ref[:, cols] = (y * sc_ref[:, cols]).astype(o_ref.dtype)


def _pool_mix(u, w_pool, pool_scale):
    b, s, w = u.shape
    g, gd, _ = w_pool.shape
    blocks = _nbytes((s, w), F32) + _nbytes((g, gd, gd), BF16) + _nbytes((s, w), BF16)
    return pl.pallas_call(
        functools.partial(_pool_kernel, windows=POOL_WINDOWS[:g]),
        grid=(b,),
        in_specs=[
            pl.BlockSpec((None, s, w), lambda bi: (bi, 0, 0)),
            pl.BlockSpec((g, gd, gd), lambda bi: (0, 0, 0)),
            pl.BlockSpec((1, w), lambda bi: (0, 0)),
        ],
        out_specs=pl.BlockSpec((None, s, w), lambda bi: (bi, 0, 0)),
        out_shape=jax.ShapeDtypeStruct((b, s, w), BF16),
        compiler_params=_params(blocks, 6 * _nbytes((s, gd), F32), 1),
        name="pool_mix",
    )(u, w_pool.astype(BF16), pool_scale.reshape(1, w))


def _rope_table(n):
    half = QK_ROPE_DIM // 2
    nf = half // 2
    t = jnp.arange(n)
    inv = ROPE_BASE ** (-jnp.arange(nf, dtype=F32) / nf)
    ar = (t // GRID_W).astype(F32)[:, None] * inv[None, :]
    ac = (t % GRID_W).astype(F32)[:, None] * inv[None, :]
    cr, sr, cc, sc = jnp.cos(ar), jnp.sin(ar), jnp.cos(ac), jnp.sin(ac)
    return jnp.concatenate([cr, cr, cc, cc, -sr, sr, -sc, sc], axis=1)


def _identity_rope_table(n):
    return jnp.concatenate([jnp.ones((n, QK_ROPE_DIM), F32), jnp.zeros((n, QK_ROPE_DIM), F32)], axis=1)


def _swap_rope_cols(w):
    nf = QK_ROPE_DIM // 4
    lead = w.shape[:-1]
    return jnp.flip(w.reshape(lead + (2, 2, nf)), axis=-2).reshape(lead + (QK_ROPE_DIM,))


def _even_mixer(h, hc, w_in, layer, w_four, rpb, with_ctx_out):
    fw = w_four.shape[0] * w_four.shape[1]
    nh = rpb.shape[0]
    u = _mm_plain(h, w_in, BF16, "even_in_fourier", layer, 0, fw, tm_target=2048)
    qkv = _mm_chunked(h, w_in, "even_in_qkv", layer, fw, tm_target=2048)
    qkvc = _mm_chunked(hc, w_in, "even_in_qkv_ctx", layer, fw)
    y_f = _four_channels(_dft_positions(u), w_four)
    y_a = _na_attention(qkv, qkvc, rpb)
    ctx_ops = None
    if with_ctx_out:
        b = h.shape[0]
        uc = _mm_plain(hc, w_in, BF16, "even_in_fourier_ctx", layer, 0, fw).reshape(b, -1, fw)
        yc_f = _four_channels(_dft_positions(uc), w_four).reshape(1, -1, fw)
        yc_a = _dense_attention(qkvc, [(qkvc, qkvc, nh, 2 * nh)], nh, 0, HEAD_DIM ** -0.5, "ctx_attention_even", b)
        ctx_ops = [yc_f, yc_a]
    return [y_f, y_a], ctx_ops


def _odd_mixer(h, hc, w_in, w_pool, pool_scale, g_q, g_kv, w_uq, w_ukv, with_ctx_out):
    b, s, d = h.shape
    lc = hc.shape[1] // b
    pw = w_pool.shape[0] * w_pool.shape[1]
    q_rank, kv_rank = g_q.shape[0], g_kv.shape[0]
    nh = w_uq.shape[1] // (QK_NOPE_DIM + QK_ROPE_DIM)
    o_q, o_kv, o_kr = pw, pw + q_rank, pw + q_rank + kv_rank
    w_in = w_in.astype(BF16)
    w_pool_in, w_q_in, w_kv_in, w_kr = w_in[:, :o_q], w_in[:, o_q:o_kv], w_in[:, o_kv:o_kr], w_in[:, o_kr:]
    w_kr2 = jnp.concatenate([w_kr, _swap_rope_cols(w_kr)], axis=1)
    wq = w_uq.astype(BF16).reshape(q_rank, nh, QK_NOPE_DIM + QK_ROPE_DIM)
    wq_r = wq[..., QK_NOPE_DIM:]
    wq2 = jnp.concatenate([wq[..., :QK_NOPE_DIM], wq_r, _swap_rope_cols(wq_r)], axis=-1).reshape(q_rank, -1)
    w_ukv = w_ukv.astype(BF16)
    tab, tab_c = _rope_table(s), _identity_rope_table(b * lc)

    def keys_values(hh, table, tag):
        ckv_n = _mm_rmsnorm(hh, w_kv_in, g_kv, "odd_in_ckv" + tag)
        k_rot = _mm_krope(hh, w_kr2, table, "odd_in_krope" + tag)
        return _mm_kv(ckv_n, w_ukv, k_rot, "odd_up_kv" + tag)

    k_l, v_l = keys_values(h, tab, "")
    k_c, v_c = keys_values(hc, tab_c, "_ctx")
    cq_n = _mm_rmsnorm(h, w_q_in, g_q, "odd_in_cq")
    q = _mm_qrope(cq_n, wq2, tab, "odd_up_q")
    scale = (QK_NOPE_DIM + QK_ROPE_DIM) ** -0.5
    attn = _dense_attention(q, [(k_l, v_l, 0, 0), (k_c, v_c, 0, 0)], nh, 0, scale, "mla_attention", b)
    u = _mm_plain(h, w_pool_in, F32, "odd_in_pool")
    pooled = _pool_mix(u, w_pool, pool_scale)
    ctx_ops = None
    if with_ctx_out:
        cqc_n = _mm_rmsnorm(hc, w_q_in, g_q, "odd_in_cq_ctx")
        qc = _mm_qrope(cqc_n, wq2, tab_c, "odd_up_q_ctx")
        attn_c = _dense_attention(qc, [(k_c, v_c, 0, 0)], nh, 0, scale, "ctx_attention_odd", b)
        uc = _mm_plain(hc, w_pool_in, F32, "odd_in_pool_ctx").reshape(b, lc, pw)
        ctx_ops = [_pool_mix(uc, w_pool, pool_scale).reshape(1, b * lc, pw), attn_c]
    return [pooled, attn], ctx_ops


def _ffn(xs, norm_g, shift, scale, gate, wg, wu, wd, layer, tag):
    h2 = _norm_call(xs, norm_g, shift, scale)
    act = _swiglu_up(h2, wg, wu, layer, "ffn_up_" + tag)
    return _mm_residual([act], wd, xs, gate, "ffn_down_" + tag, layer)


def kernel(x, c, ctx, c_ctx, w_mod, b_mod, norm1_g, norm2_g, w_in_ab, w_four, na_rpb, w_out_ab, w_in_cd, w_pool,
           pool_scale, mla_gq, mla_gkv, w_uq, w_ukv, w_out_cd, w_ffn_gate, w_ffn_up, w_ffn_down, final_g):
    b, s, d = x.shape
    depth = w_mod.shape[0]
    nrow = -(-(b + 1) // 8) * 8
    cvec = jnp.concatenate([c, c_ctx[None, :], jnp.zeros((nrow - b - 1, d), F32)], axis=0)
    mods = _mod_call(cvec, w_mod, b_mod)
    w_in_ab, w_out_ab, w_out_cd = w_in_ab.astype(BF16), w_out_ab.astype(BF16), w_out_cd.astype(BF16)
    wg, wu, wd = w_ffn_gate, w_ffn_up, w_ffn_down.astype(BF16)
    xc = ctx.reshape(1, -1, d)
    for l in range(depth):
        last = l == depth - 1
        mod = mods[l, :b].reshape(b, 1, 6 * d)
        modc = mods[l, b].reshape(1, 1, 6 * d)
        sh1, sc1, g1, sh2, sc2, g2 = jnp.split(mod, 6, axis=-1)
        csh1, csc1, cg1, csh2, csc2, cg2 = jnp.split(modc, 6, axis=-1)
        h = _norm_call(x, norm1_g[l], sh1, sc1)
        hc = _norm_call(xc, norm1_g[l], csh1, csc1)
        i = l // 2
        if l % 2 == 0:
            ops, ctx_ops = _even_mixer(h, hc, w_in_ab, i, w_four[i], na_rpb[i], not last)
            w_out = w_out_ab
        else:
            ops, ctx_ops = _odd_mixer(h, hc, w_in_cd[i], w_pool[i], pool_scale[i], mla_gq[i], mla_gkv[i],
                                      w_uq[i], w_ukv[i], not last)
            w_out = w_out_cd
        x = _mm_residual(ops, w_out, x, g1, f"out_proj_{l}", i, tn_target=1024)
        x = _ffn(x, norm2_g[l], sh2, sc2, g2, wg, wu, wd, l, f"{l}")
        if not last:
            xc = _mm_residual(ctx_ops, w_out, xc, cg1, f"out_proj_ctx_{l}", i)
            xc = _ffn(xc, norm2_g[l], csh2, csc2, cg2, wg, wu, wd, l, f"ctx_{l}")
    return _norm_call(x, final_g, out_dtype=F32)
```

```python
import functools
import math

import numpy as np
import jax
import jax.numpy as jnp
from jax import lax
from jax.experimental import pallas as pl
from jax.experimental.pallas import tpu as pltpu

GRID_W = 64
HEAD_DIM = 128
EPS = 1e-6
NEG_INF = -1e30
NA_WIN_H = 8
NA_WIN_W = 16
POOL_WINDOWS = (2, 4, 8, 16)
QK_NOPE_DIM = 128
QK_ROPE_DIM = 64
V_HEAD_DIM = 128
ROPE_BASE = 10000.0
LOG2E = 1.4426950408889634

V7X_LANES = 128
V7X_VMEM_BYTES = 64 * 1024 * 1024
V7X_VMEM_CAP = 56 * 1024 * 1024
NA_ROWS_PER_TILE = 4

BF16 = jnp.bfloat16
F32 = jnp.float32


def _pick_tile(n, target, mult=16):
    if n <= target:
        return n
    for t in range(target, 0, -1):
        if n % t == 0 and t % mult == 0:
            return t
    return n


def _nbytes(shape, dtype):
    return int(np.prod(shape)) * jnp.dtype(dtype).itemsize


def _params(block_bytes, scratch_bytes=0, ndims=3):
    est = 2 * block_bytes + scratch_bytes
    limit = int(min(max(est * 5 // 4 + (4 << 20), 16 << 20), V7X_VMEM_CAP))
    return pltpu.CompilerParams(dimension_semantics=("arbitrary",) * ndims, vmem_limit_bytes=limit)


def _mod_kernel(c_ref, w_ref, b_ref, o_ref):
    cv = c_ref[...]
    s = cv * (1.0 / (1.0 + jnp.exp(-cv)))
    acc = jnp.dot(s.astype(BF16), w_ref[...].astype(BF16), preferred_element_type=F32)
    o_ref[...] = acc + b_ref[...]


def _mod_call(cvec, w_mod, b_mod):
    depth, d, n = w_mod.shape
    r = cvec.shape[0]
    tn = _pick_tile(n, 1024, V7X_LANES)
    blocks = _nbytes((r, d), F32) + _nbytes((d, tn), F32) + 2 * _nbytes((r, tn), F32)
    return pl.pallas_call(
        _mod_kernel,
        grid=(depth, n // tn),
        in_specs=[
            pl.BlockSpec((r, d), lambda l, j: (0, 0)),
            pl.BlockSpec((None, d, tn), lambda l, j: (l, 0, j)),
            pl.BlockSpec((None, 1, tn), lambda l, j: (l, 0, j)),
        ],
        out_specs=pl.BlockSpec((None, r, tn), lambda l, j: (l, 0, j)),
        out_shape=jax.ShapeDtypeStruct((depth, r, n), F32),
        compiler_params=_params(blocks, _nbytes((d, tn), BF16), 2),
        name="mod_vectors",
    )(cvec, w_mod, b_mod.reshape(depth, 1, n))


def _norm_mod_kernel(x_ref, g_ref, sh_ref, sc_ref, o_ref):
    xf = x_ref[...]
    y = xf * lax.rsqrt(jnp.mean(xf * xf, axis=-1, keepdims=True) + EPS) * g_ref[...]
    o_ref[...] = (y * (1.0 + sc_ref[...]) + sh_ref[...]).astype(o_ref.dtype)


def _norm_kernel(x_ref, g_ref, o_ref):
    xf = x_ref[...]
    y = xf * lax.rsqrt(jnp.mean(xf * xf, axis=-1, keepdims=True) + EPS) * g_ref[...]
    o_ref[...] = y.astype(o_ref.dtype)


def _norm_call(x, g, shift=None, scale=None, out_dtype=BF16):
    b, s, d = x.shape
    ts = _pick_tile(s, 1024)
    xspec = pl.BlockSpec((None, ts, d), lambda bi, i: (bi, i, 0))
    gspec = pl.BlockSpec((1, d), lambda bi, i: (0, 0))
    mspec = pl.BlockSpec((None, 1, d), lambda bi, i: (bi, 0, 0))
    blocks = _nbytes((ts, d), F32) + _nbytes((ts, d), out_dtype) + 3 * _nbytes((1, d), F32)
    common = dict(
        grid=(b, s // ts),
        out_specs=xspec,
        out_shape=jax.ShapeDtypeStruct((b, s, d), out_dtype),
        compiler_params=_params(blocks, 2 * _nbytes((ts, d), F32), 2),
    )
    if shift is None:
        return pl.pallas_call(_norm_kernel, in_specs=[xspec, gspec], name="rmsnorm", **common)(
            x, g.reshape(1, d))
    return pl.pallas_call(_norm_mod_kernel, in_specs=[xspec, gspec, mspec, mspec], name="rmsnorm_modulate",
                          **common)(x, g.reshape(1, d), shift, scale)


def _mm_kernel(*refs, n_ops, n_extra, epilogue):
    a_refs = refs[:n_ops]
    w_refs = refs[n_ops:2 * n_ops]
    e_refs = refs[2 * n_ops:2 * n_ops + n_extra]
    o_refs = refs[2 * n_ops + n_extra:]
    acc = None
    for a_ref, w_ref in zip(a_refs, w_refs):
        part = jnp.dot(a_ref[...], w_ref[...], preferred_element_type=F32)
        acc = part if acc is None else acc + part
    epilogue(acc, e_refs, o_refs)


def _mm_call(a_ops, w_ops, extras, epilogue, outs, grid, tm, tn, name, tmp_bytes=0):
    arrays = [a for a, _ in a_ops] + [w for w, _ in w_ops] + [e for e, _ in extras]
    in_specs = [s for _, s in a_ops] + [s for _, s in w_ops] + [s for _, s in extras]
    blocks = 0
    for arr, spec in a_ops + w_ops + extras:
        blocks += _nbytes([d for d in spec.block_shape if d is not None], arr.dtype)
    for sds, spec in outs:
        blocks += _nbytes([d for d in spec.block_shape if d is not None], sds.dtype)
    kern = functools.partial(_mm_kernel, n_ops=len(a_ops), n_extra=len(extras), epilogue=epilogue)
    res = pl.pallas_call(
        kern,
        grid=grid,
        in_specs=in_specs,
        out_specs=[s for _, s in outs],
        out_shape=[o for o, _ in outs],
        compiler_params=_params(blocks, 3 * _nbytes((tm, tn), F32) + tmp_bytes, len(grid)),
        name=name,
    )(*arrays)
    return res


def _a_spec(tm, k, col_blk=0):
    return pl.BlockSpec((None, tm, k), lambda b, i, j: (b, i, col_blk))


def _w3(w):
    return w if w.ndim == 3 else w[None]


def _w_spec(k, tn, layer=0, row_blk=0, col_off=0):
    return pl.BlockSpec((None, k, tn), lambda b, i, j: (layer, row_blk, j + col_off))


def _tile_spec(tm, tn):
    return pl.BlockSpec((None, tm, tn), lambda b, i, j: (b, i, j))


def _vec_spec(tn):
    return pl.BlockSpec((None, 1, tn), lambda b, i, j: (b, 0, j))


def _ep_store(acc, e_refs, o_refs):
    o_refs[0][...] = acc.astype(o_refs[0].dtype)


def _col_tile(n, col0, target, mult):
    return _pick_tile(math.gcd(n, col0) if col0 else n, target, mult)


def _mm_plain(a, w, out_dtype, name, layer=0, col0=0, n=None, tm_target=1024, tn_target=512):
    b, s, k = a.shape
    w = _w3(w)
    n = w.shape[2] - col0 if n is None else n
    tm = _pick_tile(s, tm_target)
    tn = _col_tile(n, col0, tn_target, V7X_LANES)
    return _mm_call([(a, _a_spec(tm, k))], [(w, _w_spec(k, tn, layer, 0, col0 // tn))], [], _ep_store,
                    [(jax.ShapeDtypeStruct((b, s, n), out_dtype), _tile_spec(tm, tn))],
                    (b, s // tm, n // tn), tm, tn, name)[0]


def _ep_chunks(acc, e_refs, o_refs, *, width):
    o_ref = o_refs[0]
    for c in range(acc.shape[1] // width):
        o_ref[c] = acc[:, c * width:(c + 1) * width].astype(o_ref.dtype)


def _mm_chunked(a, w, name, layer=0, col0=0, n=None, width=HEAD_DIM, tm_target=1024, tn_target=512):
    b, s, k = a.shape
    w = _w3(w)
    n = w.shape[2] - col0 if n is None else n
    tm = _pick_tile(s, tm_target)
    tn = _col_tile(n, col0, tn_target, width)
    cpt = tn // width
    ospec = pl.BlockSpec((None, cpt, tm, width), lambda bi, i, j: (bi, j, i, 0))
    return _mm_call([(a, _a_spec(tm, k))], [(w, _w_spec(k, tn, layer, 0, col0 // tn))], [],
                    functools.partial(_ep_chunks, width=width),
                    [(jax.ShapeDtypeStruct((b, n // width, s, width), BF16), ospec)],
                    (b, s // tm, n // tn), tm, tn, name)[0]


def _ep_residual(acc, e_refs, o_refs):
    x_ref, g_ref = e_refs
    o_refs[0][...] = x_ref[...] + g_ref[...] * acc


def _mm_residual(a_list, w, x, gate, name, layer=0, tm_target=1024, tn_target=512):
    b, s, n = x.shape
    w = _w3(w)
    tm = _pick_tile(s, tm_target)
    tn = _pick_tile(n, tn_target, V7X_LANES)
    kblk = functools.reduce(math.gcd, [a.shape[2] for a in a_list])
    a_ops, w_ops = [], []
    for a in a_list:
        for cb in range(a.shape[2] // kblk):
            a_ops.append((a, _a_spec(tm, kblk, cb)))
            w_ops.append((w, _w_spec(kblk, tn, layer, len(w_ops))))
    extras = [(x, _tile_spec(tm, tn)), (gate, _vec_spec(tn))]
    return _mm_call(a_ops, w_ops, extras, _ep_residual,
                    [(jax.ShapeDtypeStruct((b, s, n), F32), _tile_spec(tm, tn))],
                    (b, s // tm, n // tn), tm, tn, name)[0]


def _swiglu_kernel(a_ref, wg_ref, wu_ref, o_ref, wgb_ref, wub_ref):
    @pl.when((pl.program_id(1) == 0) & (pl.program_id(2) == 0))
    def _():
        wgb_ref[...] = wg_ref[...].astype(BF16)
        wub_ref[...] = wu_ref[...].astype(BF16)

    a = a_ref[...]
    g = jnp.dot(a, wgb_ref[...], preferred_element_type=F32)
    u = jnp.dot(a, wub_ref[...], preferred_element_type=F32)
    o_ref[...] = (g * (1.0 / (1.0 + jnp.exp(-g))) * u).astype(o_ref.dtype)


def _swiglu_up(a, wg, wu, layer, name, tm_target=1024, tn_target=512):
    b, s, k = a.shape
    n = wg.shape[2]
    tm = _pick_tile(s, tm_target)
    tn = _pick_tile(n, tn_target, V7X_LANES)
    blocks = _nbytes((tm, k), BF16) + 2 * _nbytes((k, tn), F32) + _nbytes((tm, tn), BF16)
    wspec = pl.BlockSpec((None, k, tn), lambda j, bi, i: (layer, 0, j))
    return pl.pallas_call(
        _swiglu_kernel,
        grid=(n // tn, b, s // tm),
        in_specs=[pl.BlockSpec((None, tm, k), lambda j, bi, i: (bi, i, 0)), wspec, wspec],
        out_specs=pl.BlockSpec((None, tm, tn), lambda j, bi, i: (bi, i, j)),
        out_shape=jax.ShapeDtypeStruct((b, s, n), BF16),
        scratch_shapes=[pltpu.VMEM((k, tn), BF16), pltpu.VMEM((k, tn), BF16)],
        compiler_params=_params(blocks, 2 * _nbytes((k, tn), BF16) + 4 * _nbytes((tm, tn), F32), 3),
        name=name,
    )(a, wg, wu)


def _rope_pair(hi, tab):
    z = hi * tab
    return z + pltpu.roll(z, QK_ROPE_DIM, axis=1)


def _ep_qrope(acc, e_refs, o_refs, *, heads):
    tab_ref, = e_refs
    o_ref = o_refs[0]
    tab = tab_ref[...]
    blk = QK_NOPE_DIM + 2 * QK_ROPE_DIM
    for h in range(heads):
        o_ref[h, :, 0:QK_NOPE_DIM] = acc[:, h * blk:h * blk + QK_NOPE_DIM].astype(o_ref.dtype)
        hi = acc[:, h * blk + QK_NOPE_DIM:(h + 1) * blk]
        o_ref[h, :, QK_NOPE_DIM:blk] = _rope_pair(hi, tab).astype(o_ref.dtype)


def _mm_qrope(a, w, tab, name, tm_target=1024, heads_per_tile=6):
    b, s, k = a.shape
    blk = QK_NOPE_DIM + 2 * QK_ROPE_DIM
    nh = w.shape[1] // blk
    hpt = math.gcd(nh, heads_per_tile)
    tn = hpt * blk
    tm = _pick_tile(s, tm_target)
    tspec = pl.BlockSpec((tm, 2 * QK_ROPE_DIM), lambda bi, i, j: (i, 0))
    ospec = pl.BlockSpec((None, hpt, tm, blk), lambda bi, i, j: (bi, j, i, 0))
    return _mm_call([(a, _a_spec(tm, k))], [(_w3(w), _w_spec(k, tn))], [(tab, tspec)],
                    functools.partial(_ep_qrope, heads=hpt),
                    [(jax.ShapeDtypeStruct((b, nh, s, blk), BF16), ospec)],
                    (b, s // tm, nh // hpt), tm, tn, name)[0]


def _ep_kv(acc, e_refs, o_refs, *, heads):
    kr_ref, = e_refs
    k_ref, v_ref = o_refs
    kr = kr_ref[...]
    blk = QK_NOPE_DIM + V_HEAD_DIM
    for h in range(heads):
        k_ref[h, :, 0:QK_NOPE_DIM] = acc[:, h * blk:h * blk + QK_NOPE_DIM].astype(k_ref.dtype)
        k_ref[h, :, QK_NOPE_DIM:QK_NOPE_DIM + kr.shape[1]] = kr
        v_ref[h] = acc[:, h * blk + QK_NOPE_DIM:(h + 1) * blk].astype(v_ref.dtype)


def _mm_kv(a, w, krot, name, tm_target=1024, heads_per_tile=6):
    b, s, k = a.shape
    blk = QK_NOPE_DIM + V_HEAD_DIM
    nh = w.shape[1] // blk
    hpt = math.gcd(nh, heads_per_tile)
    tn = hpt * blk
    tm = _pick_tile(s, tm_target)
    kd = QK_NOPE_DIM + krot.shape[2]
    krspec = pl.BlockSpec((None, tm, krot.shape[2]), lambda bi, i, j: (bi, i, 0))
    kspec = pl.BlockSpec((None, hpt, tm, kd), lambda bi, i, j: (bi, j, i, 0))
    vspec = pl.BlockSpec((None, hpt, tm, V_HEAD_DIM), lambda bi, i, j: (bi, j, i, 0))
    return _mm_call([(a, _a_spec(tm, k))], [(_w3(w), _w_spec(k, tn))], [(krot, krspec)],
                    functools.partial(_ep_kv, heads=hpt),
                    [(jax.ShapeDtypeStruct((b, nh, s, kd), BF16), kspec),
                     (jax.ShapeDtypeStruct((b, nh, s, V_HEAD_DIM), BF16), vspec)],
                    (b, s // tm, nh // hpt), tm, tn, name)


def _dft_matrix(n):
    m = jnp.arange(n, dtype=jnp.int32)
    side = 1
    while side * side < n:
        side *= 2
    if side * side != n:
        k = jnp.arange(n, dtype=jnp.int32)
        ang = (2.0 * math.pi / n) * ((k[:, None] * m[None, :]) % n).astype(F32)
        return jnp.concatenate([jnp.cos(ang), jnp.sin(ang)], axis=0).astype(BF16)[None]
    a = jnp.arange(side, dtype=jnp.int32)
    ang_x = (2.0 * math.pi / side) * ((a[:, None] * m[None, :]) % side).astype(F32)
    ang_y = (2.0 * math.pi / n) * ((a[:, None] * m[None, :]) % n).astype(F32)
    cx, sx = jnp.cos(ang_x), jnp.sin(ang_x)
    cy, sy = jnp.cos(ang_y)[None], jnp.sin(ang_y)[None]
    p = jnp.concatenate([cx, sx], axis=0)[:, None, :]
    q = jnp.concatenate([sx, -cx], axis=0)[:, None, :]
    return (p * cy - q * sy).astype(BF16).reshape(1, 2 * n, n)


def _dft_positions_dense(u):
    b, s, w = u.shape
    mat = _dft_matrix(s)
    tm = _pick_tile(2 * s, 1024)
    tn = _pick_tile(w, 512, V7X_LANES)
    aspec = pl.BlockSpec((None, tm, s), lambda bi, i, j: (0, i, 0))
    wspec = pl.BlockSpec((None, s, tn), lambda bi, i, j: (bi, 0, j))
    return _mm_call([(mat, aspec)], [(u, wspec)], [], _ep_store,
                    [(jax.ShapeDtypeStruct((b, 2 * s, w), BF16), _tile_spec(tm, tn))],
                    (b, 2 * s // tm, w // tn), tm, tn, "fourier_positions")[0]


DFT_ROWS_PER_STEP = 8


def _dft_stage1_kernel(x_ref, a_ref, tc_ref, ts_ref, o_ref, *, side, width, n2_per_step):
    g = jnp.dot(a_ref[...], x_ref[...], preferred_element_type=F32)
    gr, gs = g[0:side], g[side:2 * side]
    for n2 in range(n2_per_step):
        tcs = tc_ref[:, n2 * V7X_LANES:(n2 + 1) * V7X_LANES]
        tss = ts_ref[:, n2 * V7X_LANES:(n2 + 1) * V7X_LANES]
        for c0 in range(n2 * width, (n2 + 1) * width, V7X_LANES):
            cols = slice(c0, c0 + V7X_LANES)
            o_ref[0:side, cols] = (gr[:, cols] * tcs - gs[:, cols] * tss).astype(o_ref.dtype)
            o_ref[side:2 * side, cols] = (gr[:, cols] * tss + gs[:, cols] * tcs).astype(o_ref.dtype)


def _dft_stage2_kernel(h_ref, m_ref, o_ref, scr_ref, *, side):
    rows = h_ref.shape[1]
    ngrp = scr_ref.shape[0]
    for j in range(rows):
        hcat = jnp.concatenate([h_ref[0, j], h_ref[1, j]], axis=0)
        res = jnp.dot(m_ref[...], hcat, preferred_element_type=F32)
        for c in range(ngrp):
            cols = slice(c * V7X_LANES, (c + 1) * V7X_LANES)
            scr_ref[c, pl.ds(j, side, stride=rows), :] = res[0:side, cols]
            scr_ref[c, pl.ds(side * rows + j, side, stride=rows), :] = res[side:2 * side, cols]
    for c in range(ngrp):
        cols = slice(c * V7X_LANES, (c + 1) * V7X_LANES)
        for p in range(2):
            o_ref[p, :, :, cols] = scr_ref[c, p * side * rows:(p + 1) * side * rows, :].reshape(side, rows, V7X_LANES)


def _dft_positions(u):
    b, s, w = u.shape
    side = math.isqrt(s)
    if side * side != s or side % DFT_ROWS_PER_STEP or w % V7X_LANES:
        return _dft_positions_dense(u)
    idx = np.arange(side)
    ang = 2.0 * np.pi * ((idx[:, None] * idx[None, :]) % side) / side
    cs, sn = np.cos(ang), np.sin(ang)
    a1 = jnp.asarray(np.concatenate([cs, sn], axis=0), dtype=F32).astype(BF16)
    m2 = jnp.asarray(np.block([[cs, -sn], [sn, cs]]), dtype=F32).astype(BF16)
    tw = 2.0 * np.pi * (idx[:, None] * idx[None, :]) / s
    tcos = jnp.repeat(jnp.asarray(np.cos(tw), dtype=F32), V7X_LANES, axis=1)
    tsin = jnp.repeat(jnp.asarray(np.sin(tw), dtype=F32), V7X_LANES, axis=1)
    n2s = DFT_ROWS_PER_STEP
    tc = n2s * w
    blocks1 = _nbytes((side, tc), BF16) + _nbytes((2 * side, side), BF16) + 2 * _nbytes((side, n2s * V7X_LANES), F32) \
        + _nbytes((2 * side, tc), BF16)
    h = pl.pallas_call(
        functools.partial(_dft_stage1_kernel, side=side, width=w, n2_per_step=n2s),
        grid=(b, side // n2s),
        in_specs=[
            pl.BlockSpec((None, side, tc), lambda bi, j: (bi, 0, j)),
            pl.BlockSpec((2 * side, side), lambda bi, j: (0, 0)),
            pl.BlockSpec((side, n2s * V7X_LANES), lambda bi, j: (0, j)),
            pl.BlockSpec((side, n2s * V7X_LANES), lambda bi, j: (0, j)),
        ],
        out_specs=pl.BlockSpec((None, 2 * side, tc), lambda bi, j: (bi, 0, j)),
        out_shape=jax.ShapeDtypeStruct((b, 2 * side, side * w), BF16),
        compiler_params=_params(blocks1, 3 * _nbytes((2 * side, tc), F32), 2),
        name="fourier_positions_stage1",
    )(u.reshape(b, side, side * w), a1, tcos, tsin)
    rows = DFT_ROWS_PER_STEP
    blocks2 = _nbytes((2, rows, side, w), BF16) + _nbytes((2 * side, 2 * side), BF16) + _nbytes((2, side, rows, w), F32)
    f2 = pl.pallas_call(
        functools.partial(_dft_stage2_kernel, side=side),
        grid=(b, side // rows),
        in_specs=[
            pl.BlockSpec((None, 2, rows, side, w), lambda bi, i: (bi, 0, i, 0, 0)),
            pl.BlockSpec((2 * side, 2 * side), lambda bi, i: (0, 0)),
        ],
        out_specs=pl.BlockSpec((None, 2, side, rows, w), lambda bi, i: (bi, 0, 0, i, 0)),
        out_shape=jax.ShapeDtypeStruct((b, 2, side, side, w), F32),
        scratch_shapes=[pltpu.VMEM((w // V7X_LANES, 2 * side * rows, V7X_LANES), F32)],
        compiler_params=_params(blocks2, _nbytes((2 * side * rows, w), F32) + 4 * _nbytes((2 * side, w), F32), 2),
        name="fourier_positions_stage2",
    )(h.reshape(b, 2, side, side, w), m2)
    return f2.reshape(b, 2 * s, w)


def _four_channel_kernel(fr_ref, fi_ref, cs_ref, w_ref, o_ref, *, norm):
    gd = w_ref.shape[1]
    for g in range(w_ref.shape[0]):
        cols = slice(g * gd, (g + 1) * gd)
        f = jnp.dot(fr_ref[:, cols].astype(BF16), cs_ref[0:gd, :], preferred_element_type=F32)
        f = f + jnp.dot(fi_ref[:, cols].astype(BF16), cs_ref[gd:2 * gd, :], preferred_element_type=F32)
        f = (f * norm).astype(BF16)
        o_ref[:, cols] = jnp.dot(f, w_ref[g], preferred_element_type=F32).astype(o_ref.dtype)


def _four_channels(f2, w_four):
    b, s2, w = f2.shape
    s = s2 // 2
    g, gd, _ = w_four.shape
    idx = np.arange(gd)
    ang = 2.0 * np.pi * ((idx[:, None] * idx[None, :]) % gd) / gd
    cs = jnp.asarray(np.concatenate([np.cos(ang), -np.sin(ang)], axis=0), dtype=F32).astype(BF16)
    ts = _pick_tile(s, 1024)
    nblk = s // ts
    blocks = 2 * _nbytes((ts, w), f2.dtype) + _nbytes((ts, w), BF16) + _nbytes((2 * gd, gd), BF16) \
        + _nbytes((g, gd, gd), BF16)
    return pl.pallas_call(
        functools.partial(_four_channel_kernel, norm=1.0 / math.sqrt(s * gd)),
        grid=(b, nblk),
        in_specs=[
            pl.BlockSpec((None, ts, w), lambda bi, i: (bi, i, 0)),
            pl.BlockSpec((None, ts, w), lambda bi, i: (bi, nblk + i, 0)),
            pl.BlockSpec((2 * gd, gd), lambda bi, i: (0, 0)),
            pl.BlockSpec((g, gd, gd), lambda bi, i: (0, 0, 0)),
        ],
        out_specs=pl.BlockSpec((None, ts, w), lambda bi, i: (bi, i, 0)),
        out_shape=jax.ShapeDtypeStruct((b, s, w), BF16),
        compiler_params=_params(blocks, 6 * _nbytes((ts, gd), F32), 2),
        name="fourier_channels",
    )(f2, f2, cs, w_four.astype(BF16))


def _na_plan(rows):
    rpt = NA_ROWS_PER_TILE
    kh = min(NA_WIN_H, rows)
    wrows = min(kh + rpt - 1, rows)
    cases, ws_list, case_list = {}, [], []
    for t in range(rows // rpt):
        r0 = t * rpt
        ws = int(np.clip(r0 - kh // 2, 0, rows - wrows))
        plan = []
        for r in range(r0, r0 + rpt):
            rs = int(np.clip(r - kh // 2, 0, rows - kh))
            plan.append(tuple((ws + w) - r + NA_WIN_H - 1 if rs <= ws + w < rs + kh else None
                              for w in range(wrows)))
        plan = tuple(plan)
        cases.setdefault(plan, len(cases))
        ws_list.append(ws)
        case_list.append(cases[plan])
    return wrows, ws_list, case_list, list(cases)


def _na_block_table(rpb):
    qcol = np.arange(GRID_W)[:, None]
    kcol = np.arange(GRID_W)[None, :]
    cstart = np.clip(qcol - NA_WIN_W // 2, 0, GRID_W - NA_WIN_W)
    col_ok = (kcol >= cstart) & (kcol < cstart + NA_WIN_W)
    cidx = np.clip(kcol - qcol + NA_WIN_W - 1, 0, 2 * NA_WIN_W - 2)
    csel = jnp.asarray(np.eye(rpb.shape[2], dtype=np.float32)[cidx])
    blk = jnp.einsum("hrs,qks->hrqk", rpb, csel, precision=lax.Precision.HIGHEST)
    blk = jnp.where(jnp.asarray(col_ok), blk * LOG2E, NEG_INF)
    return jnp.concatenate([blk, blk], axis=-1)


def _ones_column(rows, width):
    lane = lax.broadcasted_iota(jnp.int32, (rows, width), 1)
    return jnp.where(lane == 0, 1.0, 0.0).astype(BF16)


def _na_kernel(q_ref, k_ref, v_ref, kc_ref, vc_ref, blk_ref, o_ref, bias_ref, vx_ref, vcx_ref,
               *, tiles, tq, tk, scale, plans):
    d = v_ref.shape[1]

    @pl.when(pl.program_id(1) == 0)
    def _():
        for c, plan in enumerate(plans):
            for qr, row in enumerate(plan):
                for kr, ridx in enumerate(row):
                    half = slice((kr % 2) * GRID_W, (kr % 2 + 1) * GRID_W)
                    if ridx is None:
                        blk = jnp.full((GRID_W, GRID_W), NEG_INF, F32)
                    else:
                        blk = blk_ref[ridx][:, half]
                    bias_ref[c, qr * GRID_W:(qr + 1) * GRID_W, kr * GRID_W:(kr + 1) * GRID_W] = blk

    vx_ref[:, 0:d] = v_ref[...]
    vx_ref[:, d:2 * d] = _ones_column(v_ref.shape[0], d)
    vcx_ref[:, 0:d] = vc_ref[...]
    vcx_ref[:, d:2 * d] = _ones_column(vc_ref.shape[0], d)
    kc = kc_ref[...]
    dn = (((1,), (1,)), ((), ()))
    c2 = scale * LOG2E

    for t, (ws, case) in enumerate(tiles):
        q0, k0 = t * tq, ws * GRID_W
        q = q_ref[q0:q0 + tq, :]
        s_loc = lax.dot_general(q, k_ref[k0:k0 + tk, :], dn, preferred_element_type=F32) * c2 + bias_ref[case]
        s_ctx = lax.dot_general(q, kc, dn, preferred_element_type=F32) * c2
        m = jnp.maximum(jnp.max(s_loc, axis=-1, keepdims=True), jnp.max(s_ctx, axis=-1, keepdims=True))
        p_loc = jnp.exp2(s_loc - m).astype(BF16)
        p_ctx = jnp.exp2(s_ctx - m).astype(BF16)
        acc = jnp.dot(p_loc, vx_ref[k0:k0 + tk, :], preferred_element_type=F32)
        acc = acc + jnp.dot(p_ctx, vcx_ref[...], preferred_element_type=F32)
        o_ref[q0:q0 + tq, :] = (acc[:, 0:d] / acc[:, d:d + 1]).astype(o_ref.dtype)


def _na_attention(qkv, qkvc, rpb):
    b, c3, s, d = qkv.shape
    nh = c3 // 3
    lc = qkvc.shape[2] // b
    rows = s // GRID_W
    wrows, ws_list, case_list, plans = _na_plan(rows)
    blk = _na_block_table(rpb)
    nrel = blk.shape[1]
    tq = NA_ROWS_PER_TILE * GRID_W
    tk = wrows * GRID_W
    blocks = 3 * _nbytes((s, d), BF16) + 2 * _nbytes((lc, d), BF16) + _nbytes((nrel, GRID_W, 2 * GRID_W), F32) \
        + _nbytes((s, d), BF16)
    scratch = _nbytes((len(plans), tq, tk), F32) + _nbytes((s + lc, 2 * d), BF16)
    return pl.pallas_call(
        functools.partial(_na_kernel, tiles=list(zip(ws_list, case_list)), tq=tq, tk=tk, scale=d ** -0.5,
                          plans=plans),
        grid=(nh, b),
        in_specs=[
            pl.BlockSpec((None, None, s, d), lambda h, bi: (bi, h, 0, 0)),
            pl.BlockSpec((None, None, s, d), lambda h, bi: (bi, nh + h, 0, 0)),
            pl.BlockSpec((None, None, s, d), lambda h, bi: (bi, 2 * nh + h, 0, 0)),
            pl.BlockSpec((None, None, lc, d), lambda h, bi: (0, nh + h, bi, 0)),
            pl.BlockSpec((None, None, lc, d), lambda h, bi: (0, 2 * nh + h, bi, 0)),
            pl.BlockSpec((None, nrel, GRID_W, 2 * GRID_W), lambda h, bi: (h, 0, 0, 0)),
        ],
        out_specs=pl.BlockSpec((None, s, d), lambda h, bi: (bi, 0, h)),
        out_shape=jax.ShapeDtypeStruct((b, s, nh * d), BF16),
        scratch_shapes=[pltpu.VMEM((len(plans), tq, tk), F32), pltpu.VMEM((s, 2 * d), BF16),
                        pltpu.VMEM((lc, 2 * d), BF16)],
        compiler_params=_params(blocks, scratch + 8 * _nbytes((tq, tk + lc), F32), 2),
        name="neighbourhood_attention",
    )(qkv, qkv, qkv, qkvc, qkvc, blk)


def _dense_attn_kernel(*refs, scale, n_src, chunks, heads):
    q_ref = refs[0]
    k_refs = refs[1:1 + n_src]
    v_refs = refs[1 + n_src:1 + 2 * n_src]
    o_ref = refs[1 + 2 * n_src]
    vx_refs = refs[2 + 2 * n_src:]
    dv = v_refs[0].shape[2]

    @pl.when(pl.program_id(2) == 0)
    def _():
        for v_ref, vx_ref in zip(v_refs, vx_refs):
            for hh in range(heads):
                vx_ref[hh, :, 0:dv] = v_ref[hh]
                vx_ref[hh, :, dv:2 * dv] = _ones_column(v_ref.shape[1], dv)

    c2 = scale * LOG2E
    for hh in range(heads):
        q = q_ref[hh]
        m = acc = None
        for src, start, size in chunks:
            s = lax.dot_general(q, k_refs[src][hh, start:start + size, :], (((1,), (1,)), ((), ())),
                                preferred_element_type=F32)
            m_chunk = jnp.max(s, axis=-1, keepdims=True)
            m_new = m_chunk if m is None else jnp.maximum(m, m_chunk)
            p = jnp.exp2((s - m_new) * c2)
            pv = jnp.dot(p.astype(BF16), vx_refs[src][hh, start:start + size, :], preferred_element_type=F32)
            acc = pv if acc is None else acc * jnp.exp2((m - m_new) * c2) + pv
            m = m_new
        o_ref[:, hh * dv:(hh + 1) * dv] = (acc[:, 0:dv] / acc[:, dv:dv + 1]).astype(o_ref.dtype)


def _head_major_spec(arr, batch, rows, off, tiled, heads):
    per_batch = arr.shape[2] * arr.shape[0] // batch // rows
    hb = off // heads
    if arr.shape[0] == 1:
        return pl.BlockSpec((None, heads, rows, arr.shape[3]),
                            lambda bi, h, i: (0, hb + h, bi * per_batch + (i if tiled else 0), 0))
    return pl.BlockSpec((None, heads, rows, arr.shape[3]), lambda bi, h, i: (bi, hb + h, i if tiled else 0, 0))


def _dense_attention(q, kv_list, nh, q_off, scale, name, batch, tq_target=1024, chunk_target=512, heads=2):
    dk = q.shape[3]
    sq = q.shape[0] * q.shape[2] // batch
    dv = kv_list[0][1].shape[3]
    offsets = [q_off] + [o for _, _, ko, vo in kv_list for o in (ko, vo)]
    heads = functools.reduce(math.gcd, [nh] + offsets, heads)
    tq = _pick_tile(sq, tq_target)
    nq = sq // tq
    chunks, arrays_k, arrays_v, scratch = [], [], [], []
    blocks = heads * (_nbytes((tq, dk), BF16) + _nbytes((tq, dv), BF16))
    tmp = 4 * _nbytes((tq, chunk_target), F32) + 2 * _nbytes((tq, 2 * dv), F32)
    kspecs, vspecs = [], []
    for src, (k, v, k_off, v_off) in enumerate(kv_list):
        sk = k.shape[0] * k.shape[2] // batch
        chunks += [(src, st, min(chunk_target, sk - st)) for st in range(0, sk, chunk_target)]
        kspecs.append(_head_major_spec(k, batch, sk, k_off, False, heads))
        vspecs.append(_head_major_spec(v, batch, sk, v_off, False, heads))
        arrays_k.append(k)
        arrays_v.append(v)
        scratch.append(pltpu.VMEM((heads, sk, 2 * dv), BF16))
        blocks += heads * (_nbytes((sk, dk), BF16) + _nbytes((sk, dv), BF16))
        tmp += heads * _nbytes((sk, 2 * dv), BF16)
    if q.shape[0] == 1:
        out_spec = pl.BlockSpec((None, tq, heads * dv), lambda bi, h, i: (0, bi * nq + i, h))
    else:
        out_spec = pl.BlockSpec((None, tq, heads * dv), lambda bi, h, i: (bi, i, h))
    return pl.pallas_call(
        functools.partial(_dense_attn_kernel, scale=scale, n_src=len(kv_list), chunks=chunks, heads=heads),
        grid=(batch, nh // heads, nq),
        in_specs=[_head_major_spec(q, batch, tq, q_off, True, heads)] + kspecs + vspecs,
        out_specs=out_spec,
        out_shape=jax.ShapeDtypeStruct((q.shape[0], q.shape[2], nh * dv), BF16),
        scratch_shapes=scratch,
        compiler_params=_params(blocks, tmp, 3),
        name=name,
    )(q, *arrays_k, *arrays_v)


def _pool_kernel(u_ref, w_ref, sc_ref, o_ref, *, windows):
    n, gd = u_ref.shape[0], w_ref.shape[1]
    t = lax.broadcasted_iota(jnp.int32, (n, gd), 0)
    for g, window in enumerate(windows):
        cols = slice(g * gd, (g + 1) * gd)
        u = u_ref[:, cols]
        half = window // 2
        ssum = u
        for j in range(1, half + 1):
            ssum = ssum + jnp.where(t >= j, pltpu.roll(u, j, axis=0), 0.0)
        for j in range(1, half):
            ssum = ssum + jnp.where(t < n - j, pltpu.roll(u, n - j, axis=0), 0.0)
        cnt = jnp.minimum(t + half, n) - jnp.maximum(t - half, 0)
        p = ssum / cnt.astype(F32) - u
        y = jnp.dot(p.astype(BF16), w_ref[g], preferred_element_type=F32)
        o_ref[:, cols] = (y * sc_ref[:, cols]).astype(o_ref.dtype)


def _pool_mix(u, w_pool, pool_scale):
    b, s, w = u.shape
    g, gd, _ = w_pool.shape
    blocks = _nbytes((s, w), F32) + _nbytes((g, gd, gd), BF16) + _nbytes((s, w), BF16)
    return pl.pallas_call(
        functools.partial(_pool_kernel, windows=POOL_WINDOWS[:g]),
        grid=(b,),
        in_specs=[
            pl.BlockSpec((None, s, w), lambda bi: (bi, 0, 0)),
            pl.BlockSpec((g, gd, gd), lambda bi: (0, 0, 0)),
            pl.BlockSpec((1, w), lambda bi: (0, 0)),
        ],
        out_specs=pl.BlockSpec((None, s, w), lambda bi: (bi, 0, 0)),
        out_shape=jax.ShapeDtypeStruct((b, s, w), BF16),
        compiler_params=_params(blocks, 6 * _nbytes((s, gd), F32), 1),
        name="pool_mix",
    )(u, w_pool.astype(BF16), pool_scale.reshape(1, w))


def _rope_table(n):
    half = QK_ROPE_DIM // 2
    nf = half // 2
    t = np.arange(n)
    inv = ROPE_BASE ** (-np.arange(nf, dtype=np.float64) / nf)
    ar = (t // GRID_W)[:, None] * inv[None, :]
    ac = (t % GRID_W)[:, None] * inv[None, :]
    cr, sr, cc, sc = np.cos(ar), np.sin(ar), np.cos(ac), np.sin(ac)
    return jnp.asarray(np.concatenate([cr, cr, cc, cc, -sr, sr, -sc, sc], axis=1), dtype=F32)


def _identity_rope_table(n):
    return jnp.concatenate([jnp.ones((n, QK_ROPE_DIM), F32), jnp.zeros((n, QK_ROPE_DIM), F32)], axis=1)


def _swap_rope_cols(w):
    nf = QK_ROPE_DIM // 4
    lead = w.shape[:-1]
    return jnp.flip(w.reshape(lead + (2, 2, nf)), axis=-2).reshape(lead + (QK_ROPE_DIM,))


def _even_mixer(h, hc, w_in, layer, w_four, rpb, with_ctx_out):
    fw = w_four.shape[0] * w_four.shape[1]
    nh = rpb.shape[0]
    u = _mm_plain(h, w_in, BF16, "even_in_fourier", layer, 0, fw, tm_target=2048)
    qkv = _mm_chunked(h, w_in, "even_in_qkv", layer, fw, tm_target=2048)
    qkvc = _mm_chunked(hc, w_in, "even_in_qkv_ctx", layer, fw)
    y_f = _four_channels(_dft_positions(u), w_four)
    y_a = _na_attention(qkv, qkvc, rpb)
    ctx_ops = None
    if with_ctx_out:
        b = h.shape[0]
        uc = _mm_plain(hc, w_in, BF16, "even_in_fourier_ctx", layer, 0, fw).reshape(b, -1, fw)
        yc_f = _four_channels(_dft_positions(uc), w_four).reshape(1, -1, fw)
        yc_a = _dense_attention(qkvc, [(qkvc, qkvc, nh, 2 * nh)], nh, 0, HEAD_DIM ** -0.5, "ctx_attention_even", b)
        ctx_ops = [yc_f, yc_a]
    return [y_f, y_a], ctx_ops


def _ep_odd_in(acc, e_refs, o_refs, *, splits):
    gq_ref, gkv_ref, tab_ref = e_refs
    u_ref, cq_ref, ckv_ref, kr_ref = o_refs
    o_q, o_kv, o_kr = splits

    def rms(v, g_ref):
        return (v * lax.rsqrt(jnp.mean(v * v, axis=-1, keepdims=True) + EPS) * g_ref[...]).astype(BF16)

    u_ref[...] = acc[:, 0:o_q]
    cq_ref[...] = rms(acc[:, o_q:o_kv], gq_ref)
    ckv_ref[...] = rms(acc[:, o_kv:o_kr], gkv_ref)
    r = _rope_pair(acc[:, o_kr:], tab_ref[...])
    lane = lax.broadcasted_iota(jnp.int32, r.shape, 1)
    kr_ref[...] = jnp.where(lane < QK_ROPE_DIM, r, 0.0).astype(kr_ref.dtype)


def _mm_odd_in(a, w, g_q, g_kv, tab, splits, name, tm_target=1024):
    b, s, k = a.shape
    n = w.shape[1]
    o_q, o_kv, o_kr = splits
    tm = _pick_tile(s, tm_target)

    def row(width):
        return pl.BlockSpec((None, tm, width), lambda bi, i, j: (bi, i, 0))

    def vec(width):
        return pl.BlockSpec((1, width), lambda bi, i, j: (0, 0))

    extras = [(g_q.reshape(1, -1), vec(o_kv - o_q)), (g_kv.reshape(1, -1), vec(o_kr - o_kv)),
              (tab, pl.BlockSpec((tm, n - o_kr), lambda bi, i, j: (i, 0)))]
    outs = [(jax.ShapeDtypeStruct((b, s, o_q), F32), row(o_q)),
            (jax.ShapeDtypeStruct((b, s, o_kv - o_q), BF16), row(o_kv - o_q)),
            (jax.ShapeDtypeStruct((b, s, o_kr - o_kv), BF16), row(o_kr - o_kv)),
            (jax.ShapeDtypeStruct((b, s, n - o_kr), BF16), row(n - o_kr))]
    return _mm_call([(a, _a_spec(tm, k))], [(_w3(w), _w_spec(k, n))], extras,
                    functools.partial(_ep_odd_in, splits=splits), outs, (b, s // tm, 1), tm, n, name)


def _odd_mixer(h, hc, w_in, w_pool, pool_scale, g_q, g_kv, w_uq, w_ukv, with_ctx_out):
    b, s, d = h.shape
    lc = hc.shape[1] // b
    pw = w_pool.shape[0] * w_pool.shape[1]
    q_rank, kv_rank = g_q.shape[0], g_kv.shape[0]
    nh = w_uq.shape[1] // (QK_NOPE_DIM + QK_ROPE_DIM)
    splits = (pw, pw + q_rank, pw + q_rank + kv_rank)
    w_in = w_in.astype(BF16)
    w_in2 = jnp.concatenate([w_in, _swap_rope_cols(w_in[:, splits[2]:])], axis=1)
    wq = w_uq.astype(BF16).reshape(q_rank, nh, QK_NOPE_DIM + QK_ROPE_DIM)
    wq_r = wq[..., QK_NOPE_DIM:]
    wq2 = jnp.concatenate([wq[..., :QK_NOPE_DIM], wq_r, _swap_rope_cols(wq_r)], axis=-1).reshape(q_rank, -1)
    w_ukv = w_ukv.astype(BF16)
    tab, tab_c = _rope_table(s), _identity_rope_table(b * lc)

    u, cq_n, ckv_n, k_rot = _mm_odd_in(h, w_in2, g_q, g_kv, tab, splits, "odd_in")
    uc, cqc_n, ckvc_n, k_rot_c = _mm_odd_in(hc, w_in2, g_q, g_kv, tab_c, splits, "odd_in_ctx")
    k_l, v_l = _mm_kv(ckv_n, w_ukv, k_rot, "odd_up_kv")
    k_c, v_c = _mm_kv(ckvc_n, w_ukv, k_rot_c, "odd_up_kv_ctx")
    q = _mm_qrope(cq_n, wq2, tab, "odd_up_q")
    scale = (QK_NOPE_DIM + QK_ROPE_DIM) ** -0.5
    attn = _dense_attention(q, [(k_l, v_l, 0, 0), (k_c, v_c, 0, 0)], nh, 0, scale, "mla_attention", b)
    pooled = _pool_mix(u, w_pool, pool_scale)
    ctx_ops = None
    if with_ctx_out:
        qc = _mm_qrope(cqc_n, wq2, tab_c, "odd_up_q_ctx")
        attn_c = _dense_attention(qc, [(k_c, v_c, 0, 0)], nh, 0, scale, "ctx_attention_odd", b)
        pooled_c = _pool_mix(uc.reshape(b, lc, pw), w_pool, pool_scale).reshape(1, b * lc, pw)
        ctx_ops = [pooled_c, attn_c]
    return [pooled, attn], ctx_ops


def _ffn(xs, norm_g, shift, scale, gate, wg, wu, wd, layer, tag):
    h2 = _norm_call(xs, norm_g, shift, scale)
    act = _swiglu_up(h2, wg, wu, layer, "ffn_up_" + tag)
    return _mm_residual([act], wd, xs, gate, "ffn_down_" + tag, layer)


def kernel(x, c, ctx, c_ctx, w_mod, b_mod, norm1_g, norm2_g, w_in_ab, w_four, na_rpb, w_out_ab, w_in_cd, w_pool,
           pool_scale, mla_gq, mla_gkv, w_uq, w_ukv, w_out_cd, w_ffn_gate, w_ffn_up, w_ffn_down, final_g):
    b, s, d = x.shape
    depth = w_mod.shape[0]
    nrow = -(-(b + 1) // 8) * 8
    cvec = jnp.concatenate([c, c_ctx[None, :], jnp.zeros((nrow - b - 1, d), F32)], axis=0)
    mods = _mod_call(cvec, w_mod, b_mod)
    w_in_ab, w_out_ab, w_out_cd = w_in_ab.astype(BF16), w_out_ab.astype(BF16), w_out_cd.astype(BF16)
    wg, wu, wd = w_ffn_gate, w_ffn_up, w_ffn_down.astype(BF16)
    xc = ctx.reshape(1, -1, d)
    for l in range(depth):
        last = l == depth - 1
        mod = mods[l, :b].reshape(b, 1, 6 * d)
        modc = mods[l, b].reshape(1, 1, 6 * d)
        sh1, sc1, g1, sh2, sc2, g2 = jnp.split(mod, 6, axis=-1)
        csh1, csc1, cg1, csh2, csc2, cg2 = jnp.split(modc, 6, axis=-1)
        h = _norm_call(x, norm1_g[l], sh1, sc1)
        hc = _norm_call(xc, norm1_g[l], csh1, csc1)
        i = l // 2
        if l % 2 == 0:
            ops, ctx_ops = _even_mixer(h, hc, w_in_ab, i, w_four[i], na_rpb[i], not last)
            w_out = w_out_ab
        else:
            ops, ctx_ops = _odd_mixer(h, hc, w_in_cd[i], w_pool[i], pool_scale[i], mla_gq[i], mla_gkv[i],
                                      w_uq[i], w_ukv[i], not last)
            w_out = w_out_cd
        x = _mm_residual(ops, w_out, x, g1, f"out_proj_{l}", i, tn_target=1024)
        x = _ffn(x, norm2_g[l], sh2, sc2, g2, wg, wu, wd, l, f"{l}")
        if not last:
            xc = _mm_residual(ctx_ops, w_out, xc, cg1, f"out_proj_ctx_{l}", i)
            xc = _ffn(xc, norm2_g[l], csh2, csc2, cg2, wg, wu, wd, l, f"ctx_{l}")
    return _norm_call(x, final_g, out_dtype=F32)
```

```python
import functools
import math

import numpy as np
import jax
import jax.numpy as jnp
from jax import lax
from jax.experimental import pallas as pl
from jax.experimental.pallas import tpu as pltpu

GRID_W = 64
HEAD_DIM = 128
EPS = 1e-6
NEG_INF = -1e30
NA_WIN_H = 8
NA_WIN_W = 16
POOL_WINDOWS = (2, 4, 8, 16)
QK_NOPE_DIM = 128
QK_ROPE_DIM = 64
V_HEAD_DIM = 128
ROPE_BASE = 10000.0
LOG2E = 1.4426950408889634

V7X_LANES = 128
V7X_VMEM_BYTES = 64 * 1024 * 1024
V7X_VMEM_CAP = 56 * 1024 * 1024
NA_ROWS_PER_TILE = 4

BF16 = jnp.bfloat16
F32 = jnp.float32


def _pick_tile(n, target, mult=16):
    if n <= target:
        return n
    for t in range(target, 0, -1):
        if n % t == 0 and t % mult == 0:
            return t
    return n


def _nbytes(shape, dtype):
    return int(np.prod(shape)) * jnp.dtype(dtype).itemsize


def _params(block_bytes, scratch_bytes=0, ndims=3):
    est = 2 * block_bytes + scratch_bytes
    limit = int(min(max(est * 5 // 4 + (4 << 20), 16 << 20), V7X_VMEM_CAP))
    return pltpu.CompilerParams(dimension_semantics=("arbitrary",) * ndims, vmem_limit_bytes=limit)


def _mod_kernel(c_ref, w_ref, b_ref, o_ref):
    cv = c_ref[...]
    s = cv * (1.0 / (1.0 + jnp.exp(-cv)))
    acc = jnp.dot(s.astype(BF16), w_ref[...].astype(BF16), preferred_element_type=F32)
    o_ref[...] = acc + b_ref[...]


def _mod_call(cvec, w_mod, b_mod):
    depth, d, n = w_mod.shape
    r = cvec.shape[0]
    tn = _pick_tile(n, 1024, V7X_LANES)
    blocks = _nbytes((r, d), F32) + _nbytes((d, tn), F32) + 2 * _nbytes((r, tn), F32)
    return pl.pallas_call(
        _mod_kernel,
        grid=(depth, n // tn),
        in_specs=[
            pl.BlockSpec((r, d), lambda l, j: (0, 0)),
            pl.BlockSpec((None, d, tn), lambda l, j: (l, 0, j)),
            pl.BlockSpec((None, 1, tn), lambda l, j: (l, 0, j)),
        ],
        out_specs=pl.BlockSpec((None, r, tn), lambda l, j: (l, 0, j)),
        out_shape=jax.ShapeDtypeStruct((depth, r, n), F32),
        compiler_params=_params(blocks, _nbytes((d, tn), BF16), 2),
        name="mod_vectors",
    )(cvec, w_mod, b_mod.reshape(depth, 1, n))


def _norm_mod_kernel(x_ref, g_ref, sh_ref, sc_ref, o_ref):
    xf = x_ref[...]
    y = xf * lax.rsqrt(jnp.mean(xf * xf, axis=-1, keepdims=True) + EPS) * g_ref[...]
    o_ref[...] = (y * (1.0 + sc_ref[...]) + sh_ref[...]).astype(o_ref.dtype)


def _norm_kernel(x_ref, g_ref, o_ref):
    xf = x_ref[...]
    y = xf * lax.rsqrt(jnp.mean(xf * xf, axis=-1, keepdims=True) + EPS) * g_ref[...]
    o_ref[...] = y.astype(o_ref.dtype)


def _norm_call(x, g, shift=None, scale=None, out_dtype=BF16):
    b, s, d = x.shape
    ts = _pick_tile(s, 1024 if b * s >= 8192 else 256)
    xspec = pl.BlockSpec((None, ts, d), lambda bi, i: (bi, i, 0))
    gspec = pl.BlockSpec((1, d), lambda bi, i: (0, 0))
    mspec = pl.BlockSpec((None, 1, d), lambda bi, i: (bi, 0, 0))
    blocks = _nbytes((ts, d), F32) + _nbytes((ts, d), out_dtype) + 3 * _nbytes((1, d), F32)
    common = dict(
        grid=(b, s // ts),
        out_specs=xspec,
        out_shape=jax.ShapeDtypeStruct((b, s, d), out_dtype),
        compiler_params=_params(blocks, 2 * _nbytes((ts, d), F32), 2),
    )
    if shift is None:
        return pl.pallas_call(_norm_kernel, in_specs=[xspec, gspec], name="rmsnorm", **common)(
            x, g.reshape(1, d))
    return pl.pallas_call(_norm_mod_kernel, in_specs=[xspec, gspec, mspec, mspec], name="rmsnorm_modulate",
                          **common)(x, g.reshape(1, d), shift, scale)


def _mm_kernel(*refs, n_ops, n_extra, epilogue):
    a_refs = refs[:n_ops]
    w_refs = refs[n_ops:2 * n_ops]
    e_refs = refs[2 * n_ops:2 * n_ops + n_extra]
    o_refs = refs[2 * n_ops + n_extra:]
    acc = None
    for a_ref, w_ref in zip(a_refs, w_refs):
        part = jnp.dot(a_ref[...], w_ref[...], preferred_element_type=F32)
        acc = part if acc is None else acc + part
    epilogue(acc, e_refs, o_refs)


def _mm_call(a_ops, w_ops, extras, epilogue, outs, grid, tm, tn, name, tmp_bytes=0):
    arrays = [a for a, _ in a_ops] + [w for w, _ in w_ops] + [e for e, _ in extras]
    in_specs = [s for _, s in a_ops] + [s for _, s in w_ops] + [s for _, s in extras]
    blocks = 0
    for arr, spec in a_ops + w_ops + extras:
        blocks += _nbytes([d for d in spec.block_shape if d is not None], arr.dtype)
    for sds, spec in outs:
        blocks += _nbytes([d for d in spec.block_shape if d is not None], sds.dtype)
    kern = functools.partial(_mm_kernel, n_ops=len(a_ops), n_extra=len(extras), epilogue=epilogue)
    res = pl.pallas_call(
        kern,
        grid=grid,
        in_specs=in_specs,
        out_specs=[s for _, s in outs],
        out_shape=[o for o, _ in outs],
        compiler_params=_params(blocks, 3 * _nbytes((tm, tn), F32) + tmp_bytes, len(grid)),
        name=name,
    )(*arrays)
    return res


def _a_spec(tm, k, col_blk=0):
    return pl.BlockSpec((None, tm, k), lambda b, i, j: (b, i, col_blk))


def _w3(w):
    return w if w.ndim == 3 else w[None]


def _w_spec(k, tn, layer=0, row_blk=0, col_off=0):
    return pl.BlockSpec((None, k, tn), lambda b, i, j: (layer, row_blk, j + col_off))


def _tile_spec(tm, tn):
    return pl.BlockSpec((None, tm, tn), lambda b, i, j: (b, i, j))


def _vec_spec(tn):
    return pl.BlockSpec((None, 1, tn), lambda b, i, j: (b, 0, j))


def _ep_store(acc, e_refs, o_refs):
    o_refs[0][...] = acc.astype(o_refs[0].dtype)


def _col_tile(n, col0, target, mult):
    return _pick_tile(math.gcd(n, col0) if col0 else n, target, mult)


def _mm_plain(a, w, out_dtype, name, layer=0, col0=0, n=None, tm_target=1024, tn_target=512):
    b, s, k = a.shape
    w = _w3(w)
    n = w.shape[2] - col0 if n is None else n
    tm = _pick_tile(s, tm_target)
    tn = _col_tile(n, col0, tn_target, V7X_LANES)
    return _mm_call([(a, _a_spec(tm, k))], [(w, _w_spec(k, tn, layer, 0, col0 // tn))], [], _ep_store,
                    [(jax.ShapeDtypeStruct((b, s, n), out_dtype), _tile_spec(tm, tn))],
                    (b, s // tm, n // tn), tm, tn, name)[0]


def _ep_chunks(acc, e_refs, o_refs, *, width):
    o_ref = o_refs[0]
    for c in range(acc.shape[1] // width):
        o_ref[c] = acc[:, c * width:(c + 1) * width].astype(o_ref.dtype)


def _mm_chunked(a, w, name, layer=0, col0=0, n=None, width=HEAD_DIM, tm_target=1024, tn_target=512):
    b, s, k = a.shape
    w = _w3(w)
    n = w.shape[2] - col0 if n is None else n
    tm = _pick_tile(s, tm_target)
    tn = _col_tile(n, col0, tn_target, width)
    cpt = tn // width
    ospec = pl.BlockSpec((None, cpt, tm, width), lambda bi, i, j: (bi, j, i, 0))
    return _mm_call([(a, _a_spec(tm, k))], [(w, _w_spec(k, tn, layer, 0, col0 // tn))], [],
                    functools.partial(_ep_chunks, width=width),
                    [(jax.ShapeDtypeStruct((b, n // width, s, width), BF16), ospec)],
                    (b, s // tm, n // tn), tm, tn, name)[0]


def _ep_residual(acc, e_refs, o_refs):
    x_ref, g_ref = e_refs
    o_refs[0][...] = x_ref[...] + g_ref[...] * acc


def _mm_residual(a_list, w, x, gate, name, layer=0, tm_target=1024, tn_target=512):
    b, s, n = x.shape
    w = _w3(w)
    tm = _pick_tile(s, tm_target)
    tn = _pick_tile(n, tn_target, V7X_LANES)
    kblk = functools.reduce(math.gcd, [a.shape[2] for a in a_list])
    a_ops, w_ops = [], []
    for a in a_list:
        for cb in range(a.shape[2] // kblk):
            a_ops.append((a, _a_spec(tm, kblk, cb)))
            w_ops.append((w, _w_spec(kblk, tn, layer, len(w_ops))))
    extras = [(x, _tile_spec(tm, tn)), (gate, _vec_spec(tn))]
    return _mm_call(a_ops, w_ops, extras, _ep_residual,
                    [(jax.ShapeDtypeStruct((b, s, n), F32), _tile_spec(tm, tn))],
                    (b, s // tm, n // tn), tm, tn, name)[0]


def _swiglu_kernel(*refs, round_down):
    if round_down:
        a_ref, wg_ref, wu_ref, wd_ref, o_ref, wd16_ref, wgb_ref, wub_ref = refs
    else:
        a_ref, wg_ref, wu_ref, o_ref, wgb_ref, wub_ref = refs

    @pl.when((pl.program_id(1) == 0) & (pl.program_id(2) == 0))
    def _():
        wgb_ref[...] = wg_ref[...].astype(BF16)
        wub_ref[...] = wu_ref[...].astype(BF16)
        if round_down:
            wd16_ref[...] = wd_ref[...].astype(BF16)

    a = a_ref[...]
    g = jnp.dot(a, wgb_ref[...], preferred_element_type=F32)
    u = jnp.dot(a, wub_ref[...], preferred_element_type=F32)
    o_ref[...] = (g * (1.0 / (1.0 + jnp.exp(-g))) * u).astype(o_ref.dtype)


def _swiglu_up(a, wg, wu, layer, name, wd=None, tm_target=1024, tn_target=512):
    b, s, k = a.shape
    n = wg.shape[2]
    tm = _pick_tile(s, tm_target)
    tn = _pick_tile(n, tn_target, V7X_LANES)
    blocks = _nbytes((tm, k), BF16) + 2 * _nbytes((k, tn), F32) + _nbytes((tm, tn), BF16)
    wspec = pl.BlockSpec((None, k, tn), lambda j, bi, i: (layer, 0, j))
    arrays = [a, wg, wu]
    in_specs = [pl.BlockSpec((None, tm, k), lambda j, bi, i: (bi, i, 0)), wspec, wspec]
    out_specs = [pl.BlockSpec((None, tm, tn), lambda j, bi, i: (bi, i, j))]
    out_shape = [jax.ShapeDtypeStruct((b, s, n), BF16)]
    if wd is not None:
        dout = wd.shape[2]
        arrays.append(wd)
        in_specs.append(pl.BlockSpec((None, tn, dout), lambda j, bi, i: (layer, j, 0)))
        out_specs.append(pl.BlockSpec((tn, dout), lambda j, bi, i: (j, 0)))
        out_shape.append(jax.ShapeDtypeStruct((n, dout), BF16))
        blocks += _nbytes((tn, dout), F32) + _nbytes((tn, dout), BF16)
    res = pl.pallas_call(
        functools.partial(_swiglu_kernel, round_down=wd is not None),
        grid=(n // tn, b, s // tm),
        in_specs=in_specs,
        out_specs=out_specs,
        out_shape=out_shape,
        scratch_shapes=[pltpu.VMEM((k, tn), BF16), pltpu.VMEM((k, tn), BF16)],
        compiler_params=_params(blocks, 2 * _nbytes((k, tn), BF16) + 4 * _nbytes((tm, tn), F32), 3),
        name=name,
    )(*arrays)
    return res if wd is not None else res[0]


def _rope_pair(hi, tab):
    z = hi * tab
    return z + pltpu.roll(z, QK_ROPE_DIM, axis=1)


def _ep_qrope(acc, e_refs, o_refs, *, heads):
    tab_ref, = e_refs
    o_ref = o_refs[0]
    tab = tab_ref[...]
    blk = QK_NOPE_DIM + 2 * QK_ROPE_DIM
    for h in range(heads):
        o_ref[h, :, 0:QK_NOPE_DIM] = acc[:, h * blk:h * blk + QK_NOPE_DIM].astype(o_ref.dtype)
        hi = acc[:, h * blk + QK_NOPE_DIM:(h + 1) * blk]
        o_ref[h, :, QK_NOPE_DIM:blk] = _rope_pair(hi, tab).astype(o_ref.dtype)


def _mm_qrope(a, w, tab, name, tm_target=1024, heads_per_tile=6):
    b, s, k = a.shape
    blk = QK_NOPE_DIM + 2 * QK_ROPE_DIM
    nh = w.shape[1] // blk
    hpt = math.gcd(nh, heads_per_tile)
    tn = hpt * blk
    tm = _pick_tile(s, tm_target)
    tspec = pl.BlockSpec((tm, 2 * QK_ROPE_DIM), lambda bi, i, j: (i, 0))
    ospec = pl.BlockSpec((None, hpt, tm, blk), lambda bi, i, j: (bi, j, i, 0))
    return _mm_call([(a, _a_spec(tm, k))], [(_w3(w), _w_spec(k, tn))], [(tab, tspec)],
                    functools.partial(_ep_qrope, heads=hpt),
                    [(jax.ShapeDtypeStruct((b, nh, s, blk), BF16), ospec)],
                    (b, s // tm, nh // hpt), tm, tn, name)[0]


def _ep_kv(acc, e_refs, o_refs, *, heads):
    kr_ref, = e_refs
    k_ref, v_ref = o_refs
    kr = kr_ref[...]
    blk = QK_NOPE_DIM + V_HEAD_DIM
    for h in range(heads):
        k_ref[h, :, 0:QK_NOPE_DIM] = acc[:, h * blk:h * blk + QK_NOPE_DIM].astype(k_ref.dtype)
        k_ref[h, :, QK_NOPE_DIM:QK_NOPE_DIM + kr.shape[1]] = kr
        v_ref[h] = acc[:, h * blk + QK_NOPE_DIM:(h + 1) * blk].astype(v_ref.dtype)


def _mm_kv(a, w, krot, name, tm_target=1024, heads_per_tile=6):
    b, s, k = a.shape
    blk = QK_NOPE_DIM + V_HEAD_DIM
    nh = w.shape[1] // blk
    hpt = math.gcd(nh, heads_per_tile)
    tn = hpt * blk
    tm = _pick_tile(s, tm_target)
    kd = QK_NOPE_DIM + krot.shape[2]
    krspec = pl.BlockSpec((None, tm, krot.shape[2]), lambda bi, i, j: (bi, i, 0))
    kspec = pl.BlockSpec((None, hpt, tm, kd), lambda bi, i, j: (bi, j, i, 0))
    vspec = pl.BlockSpec((None, hpt, tm, V_HEAD_DIM), lambda bi, i, j: (bi, j, i, 0))
    return _mm_call([(a, _a_spec(tm, k))], [(_w3(w), _w_spec(k, tn))], [(krot, krspec)],
                    functools.partial(_ep_kv, heads=hpt),
                    [(jax.ShapeDtypeStruct((b, nh, s, kd), BF16), kspec),
                     (jax.ShapeDtypeStruct((b, nh, s, V_HEAD_DIM), BF16), vspec)],
                    (b, s // tm, nh // hpt), tm, tn, name)


def _dft_matrix(n):
    m = jnp.arange(n, dtype=jnp.int32)
    side = 1
    while side * side < n:
        side *= 2
    if side * side != n:
        k = jnp.arange(n, dtype=jnp.int32)
        ang = (2.0 * math.pi / n) * ((k[:, None] * m[None, :]) % n).astype(F32)
        return jnp.concatenate([jnp.cos(ang), jnp.sin(ang)], axis=0).astype(BF16)[None]
    a = jnp.arange(side, dtype=jnp.int32)
    ang_x = (2.0 * math.pi / side) * ((a[:, None] * m[None, :]) % side).astype(F32)
    ang_y = (2.0 * math.pi / n) * ((a[:, None] * m[None, :]) % n).astype(F32)
    cx, sx = jnp.cos(ang_x), jnp.sin(ang_x)
    cy, sy = jnp.cos(ang_y)[None], jnp.sin(ang_y)[None]
    p = jnp.concatenate([cx, sx], axis=0)[:, None, :]
    q = jnp.concatenate([sx, -cx], axis=0)[:, None, :]
    return (p * cy - q * sy).astype(BF16).reshape(1, 2 * n, n)


def _dft_positions_dense(u):
    b, s, w = u.shape
    mat = _dft_matrix(s)
    tm = _pick_tile(2 * s, 1024)
    tn = _pick_tile(w, 512, V7X_LANES)
    aspec = pl.BlockSpec((None, tm, s), lambda bi, i, j: (0, i, 0))
    wspec = pl.BlockSpec((None, s, tn), lambda bi, i, j: (bi, 0, j))
    return _mm_call([(mat, aspec)], [(u, wspec)], [], _ep_store,
                    [(jax.ShapeDtypeStruct((b, 2 * s, w), BF16), _tile_spec(tm, tn))],
                    (b, 2 * s // tm, w // tn), tm, tn, "fourier_positions")[0]


DFT_ROWS_PER_STEP = 16


def _dft_stage1_kernel(x_ref, a_ref, tc_ref, ts_ref, o_ref, *, side, width, n2_per_step):
    g = jnp.dot(a_ref[...], x_ref[...], preferred_element_type=F32)
    gr, gs = g[0:side], g[side:2 * side]
    for n2 in range(n2_per_step):
        tcs = tc_ref[:, n2 * V7X_LANES:(n2 + 1) * V7X_LANES]
        tss = ts_ref[:, n2 * V7X_LANES:(n2 + 1) * V7X_LANES]
        for c0 in range(n2 * width, (n2 + 1) * width, V7X_LANES):
            cols = slice(c0, c0 + V7X_LANES)
            o_ref[0:side, cols] = (gr[:, cols] * tcs - gs[:, cols] * tss).astype(o_ref.dtype)
            o_ref[side:2 * side, cols] = (gr[:, cols] * tss + gs[:, cols] * tcs).astype(o_ref.dtype)


def _dft_stage2_kernel(h_ref, m_ref, o_ref, scr_ref, *, side):
    rows = h_ref.shape[1]
    ngrp = scr_ref.shape[0]
    for j in range(rows):
        hcat = jnp.concatenate([h_ref[0, j], h_ref[1, j]], axis=0)
        res = jnp.dot(m_ref[...], hcat, preferred_element_type=F32)
        for c in range(ngrp):
            cols = slice(c * V7X_LANES, (c + 1) * V7X_LANES)
            scr_ref[c, pl.ds(j, side, stride=rows), :] = res[0:side, cols]
            scr_ref[c, pl.ds(side * rows + j, side, stride=rows), :] = res[side:2 * side, cols]
    for c in range(ngrp):
        cols = slice(c * V7X_LANES, (c + 1) * V7X_LANES)
        for p in range(2):
            part = scr_ref[c, p * side * rows:(p + 1) * side * rows, :].reshape(side, rows, V7X_LANES)
            o_ref[p, :, :, cols] = part.astype(o_ref.dtype)


def _dft_positions(u):
    b, s, w = u.shape
    side = math.isqrt(s)
    if side * side != s or side % DFT_ROWS_PER_STEP or w % V7X_LANES:
        return _dft_positions_dense(u)
    idx = np.arange(side)
    ang = 2.0 * np.pi * ((idx[:, None] * idx[None, :]) % side) / side
    cs, sn = np.cos(ang), np.sin(ang)
    a1 = jnp.asarray(np.concatenate([cs, sn], axis=0), dtype=F32).astype(BF16)
    m2 = jnp.asarray(np.block([[cs, -sn], [sn, cs]]), dtype=F32).astype(BF16)
    tw = 2.0 * np.pi * (idx[:, None] * idx[None, :]) / s
    tcos = jnp.repeat(jnp.asarray(np.cos(tw), dtype=F32), V7X_LANES, axis=1)
    tsin = jnp.repeat(jnp.asarray(np.sin(tw), dtype=F32), V7X_LANES, axis=1)
    n2s = DFT_ROWS_PER_STEP
    tc = n2s * w
    blocks1 = _nbytes((side, tc), BF16) + _nbytes((2 * side, side), BF16) + 2 * _nbytes((side, n2s * V7X_LANES), F32) \
        + _nbytes((2 * side, tc), BF16)
    h = pl.pallas_call(
        functools.partial(_dft_stage1_kernel, side=side, width=w, n2_per_step=n2s),
        grid=(b, side // n2s),
        in_specs=[
            pl.BlockSpec((None, side, tc), lambda bi, j: (bi, 0, j)),
            pl.BlockSpec((2 * side, side), lambda bi, j: (0, 0)),
            pl.BlockSpec((side, n2s * V7X_LANES), lambda bi, j: (0, j)),
            pl.BlockSpec((side, n2s * V7X_LANES), lambda bi, j: (0, j)),
        ],
        out_specs=pl.BlockSpec((None, 2 * side, tc), lambda bi, j: (bi, 0, j)),
        out_shape=jax.ShapeDtypeStruct((b, 2 * side, side * w), BF16),
        compiler_params=_params(blocks1, 3 * _nbytes((2 * side, tc), F32), 2),
        name="fourier_positions_stage1",
    )(u.reshape(b, side, side * w), a1, tcos, tsin)
    rows = DFT_ROWS_PER_STEP
    blocks2 = 2 * _nbytes((2, rows, side, w), BF16) + _nbytes((2 * side, 2 * side), BF16)
    f2 = pl.pallas_call(
        functools.partial(_dft_stage2_kernel, side=side),
        grid=(b, side // rows),
        in_specs=[
            pl.BlockSpec((None, 2, rows, side, w), lambda bi, i: (bi, 0, i, 0, 0)),
            pl.BlockSpec((2 * side, 2 * side), lambda bi, i: (0, 0)),
        ],
        out_specs=pl.BlockSpec((None, 2, side, rows, w), lambda bi, i: (bi, 0, 0, i, 0)),
        out_shape=jax.ShapeDtypeStruct((b, 2, side, side, w), BF16),
        scratch_shapes=[pltpu.VMEM((w // V7X_LANES, 2 * side * rows, V7X_LANES), F32)],
        compiler_params=_params(blocks2, _nbytes((2 * side * rows, w), F32) + 4 * _nbytes((2 * side, w), F32), 2),
        name="fourier_positions_stage2",
    )(h.reshape(b, 2, side, side, w), m2)
    return f2.reshape(b, 2 * s, w)


def _four_channel_kernel(fr_ref, fi_ref, cs_ref, w_ref, o_ref, *, norm):
    gd = w_ref.shape[1]
    for g in range(w_ref.shape[0]):
        cols = slice(g * gd, (g + 1) * gd)
        f = jnp.dot(fr_ref[:, cols].astype(BF16), cs_ref[0:gd, :], preferred_element_type=F32)
        f = f + jnp.dot(fi_ref[:, cols].astype(BF16), cs_ref[gd:2 * gd, :], preferred_element_type=F32)
        f = (f * norm).astype(BF16)
        o_ref[:, cols] = jnp.dot(f, w_ref[g], preferred_element_type=F32).astype(o_ref.dtype)


def _four_channels(f2, w_four):
    b, s2, w = f2.shape
    s = s2 // 2
    g, gd, _ = w_four.shape
    idx = np.arange(gd)
    ang = 2.0 * np.pi * ((idx[:, None] * idx[None, :]) % gd) / gd
    cs = jnp.asarray(np.concatenate([np.cos(ang), -np.sin(ang)], axis=0), dtype=F32).astype(BF16)
    ts = _pick_tile(s, 1024)
    nblk = s // ts
    blocks = 2 * _nbytes((ts, w), f2.dtype) + _nbytes((ts, w), BF16) + _nbytes((2 * gd, gd), BF16) \
        + _nbytes((g, gd, gd), BF16)
    return pl.pallas_call(
        functools.partial(_four_channel_kernel, norm=1.0 / math.sqrt(s * gd)),
        grid=(b, nblk),
        in_specs=[
            pl.BlockSpec((None, ts, w), lambda bi, i: (bi, i, 0)),
            pl.BlockSpec((None, ts, w), lambda bi, i: (bi, nblk + i, 0)),
            pl.BlockSpec((2 * gd, gd), lambda bi, i: (0, 0)),
            pl.BlockSpec((g, gd, gd), lambda bi, i: (0, 0, 0)),
        ],
        out_specs=pl.BlockSpec((None, ts, w), lambda bi, i: (bi, i, 0)),
        out_shape=jax.ShapeDtypeStruct((b, s, w), BF16),
        compiler_params=_params(blocks, 6 * _nbytes((ts, gd), F32), 2),
        name="fourier_channels",
    )(f2, f2, cs, w_four.astype(BF16))


def _na_plan(rows):
    rpt = NA_ROWS_PER_TILE
    kh = min(NA_WIN_H, rows)
    wrows = min(kh + rpt - 1, rows)
    cases, ws_list, case_list = {}, [], []
    for t in range(rows // rpt):
        r0 = t * rpt
        ws = int(np.clip(r0 - kh // 2, 0, rows - wrows))
        plan = []
        for r in range(r0, r0 + rpt):
            rs = int(np.clip(r - kh // 2, 0, rows - kh))
            plan.append(tuple((ws + w) - r + NA_WIN_H - 1 if rs <= ws + w < rs + kh else None
                              for w in range(wrows)))
        plan = tuple(plan)
        cases.setdefault(plan, len(cases))
        ws_list.append(ws)
        case_list.append(cases[plan])
    return wrows, ws_list, case_list, list(cases)


def _na_block_table(rpb):
    qcol = np.arange(GRID_W)[:, None]
    kcol = np.arange(GRID_W)[None, :]
    cstart = np.clip(qcol - NA_WIN_W // 2, 0, GRID_W - NA_WIN_W)
    col_ok = (kcol >= cstart) & (kcol < cstart + NA_WIN_W)
    cidx = np.clip(kcol - qcol + NA_WIN_W - 1, 0, 2 * NA_WIN_W - 2)
    csel = jnp.asarray(np.eye(rpb.shape[2], dtype=np.float32)[cidx])
    blk = jnp.einsum("hrs,qks->hrqk", rpb, csel, precision=lax.Precision.HIGHEST)
    blk = jnp.where(jnp.asarray(col_ok), blk * LOG2E, NEG_INF)
    return jnp.concatenate([blk, blk], axis=-1)


def _ones_column(rows, width):
    lane = lax.broadcasted_iota(jnp.int32, (rows, width), 1)
    return jnp.where(lane == 0, 1.0, 0.0).astype(BF16)


def _na_kernel(q_ref, k_ref, v_ref, kc_ref, vc_ref, blk_ref, o_ref, bias_ref, vx_ref, vcx_ref,
               *, tiles, tq, tk, scale, plans):
    d = v_ref.shape[1]

    @pl.when(pl.program_id(1) == 0)
    def _():
        for c, plan in enumerate(plans):
            for qr, row in enumerate(plan):
                for kr, ridx in enumerate(row):
                    half = slice((kr % 2) * GRID_W, (kr % 2 + 1) * GRID_W)
                    if ridx is None:
                        blk = jnp.full((GRID_W, GRID_W), NEG_INF, F32)
                    else:
                        blk = blk_ref[ridx][:, half]
                    bias_ref[c, qr * GRID_W:(qr + 1) * GRID_W, kr * GRID_W:(kr + 1) * GRID_W] = blk

    vx_ref[:, 0:d] = v_ref[...]
    vx_ref[:, d:2 * d] = _ones_column(v_ref.shape[0], d)
    vcx_ref[:, 0:d] = vc_ref[...]
    vcx_ref[:, d:2 * d] = _ones_column(vc_ref.shape[0], d)
    kc = kc_ref[...]
    dn = (((1,), (1,)), ((), ()))
    c2 = scale * LOG2E

    for t, (ws, case) in enumerate(tiles):
        q0, k0 = t * tq, ws * GRID_W
        q = q_ref[q0:q0 + tq, :]
        s_loc = lax.dot_general(q, k_ref[k0:k0 + tk, :], dn, preferred_element_type=F32) * c2 + bias_ref[case]
        s_ctx = lax.dot_general(q, kc, dn, preferred_element_type=F32) * c2
        m = jnp.maximum(jnp.max(s_loc, axis=-1, keepdims=True), jnp.max(s_ctx, axis=-1, keepdims=True))
        p_loc = jnp.exp2(s_loc - m).astype(BF16)
        p_ctx = jnp.exp2(s_ctx - m).astype(BF16)
        acc = jnp.dot(p_loc, vx_ref[k0:k0 + tk, :], preferred_element_type=F32)
        acc = acc + jnp.dot(p_ctx, vcx_ref[...], preferred_element_type=F32)
        o_ref[q0:q0 + tq, :] = (acc[:, 0:d] / acc[:, d:d + 1]).astype(o_ref.dtype)


def _na_attention(qkv, qkvc, rpb):
    b, c3, s, d = qkv.shape
    nh = c3 // 3
    lc = qkvc.shape[2] // b
    rows = s // GRID_W
    wrows, ws_list, case_list, plans = _na_plan(rows)
    blk = _na_block_table(rpb)
    nrel = blk.shape[1]
    tq = NA_ROWS_PER_TILE * GRID_W
    tk = wrows * GRID_W
    blocks = 3 * _nbytes((s, d), BF16) + 2 * _nbytes((lc, d), BF16) + _nbytes((nrel, GRID_W, 2 * GRID_W), F32) \
        + _nbytes((s, d), BF16)
    scratch = _nbytes((len(plans), tq, tk), F32) + _nbytes((s + lc, 2 * d), BF16)
    return pl.pallas_call(
        functools.partial(_na_kernel, tiles=list(zip(ws_list, case_list)), tq=tq, tk=tk, scale=d ** -0.5,
                          plans=plans),
        grid=(nh, b),
        in_specs=[
            pl.BlockSpec((None, None, s, d), lambda h, bi: (bi, h, 0, 0)),
            pl.BlockSpec((None, None, s, d), lambda h, bi: (bi, nh + h, 0, 0)),
            pl.BlockSpec((None, None, s, d), lambda h, bi: (bi, 2 * nh + h, 0, 0)),
            pl.BlockSpec((None, None, lc, d), lambda h, bi: (0, nh + h, bi, 0)),
            pl.BlockSpec((None, None, lc, d), lambda h, bi: (0, 2 * nh + h, bi, 0)),
            pl.BlockSpec((None, nrel, GRID_W, 2 * GRID_W), lambda h, bi: (h, 0, 0, 0)),
        ],
        out_specs=pl.BlockSpec((None, s, d), lambda h, bi: (bi, 0, h)),
        out_shape=jax.ShapeDtypeStruct((b, s, nh * d), BF16),
        scratch_shapes=[pltpu.VMEM((len(plans), tq, tk), F32), pltpu.VMEM((s, 2 * d), BF16),
                        pltpu.VMEM((lc, 2 * d), BF16)],
        compiler_params=_params(blocks, scratch + 8 * _nbytes((tq, tk + lc), F32), 2),
        name="neighbourhood_attention",
    )(qkv, qkv, qkv, qkvc, qkvc, blk)


def _dense_attn_kernel(*refs, scale, n_src, chunks, heads):
    q_ref = refs[0]
    k_refs = refs[1:1 + n_src]
    v_refs = refs[1 + n_src:1 + 2 * n_src]
    o_ref = refs[1 + 2 * n_src]
    vx_refs = refs[2 + 2 * n_src:]
    dv = v_refs[0].shape[2]

    @pl.when(pl.program_id(2) == 0)
    def _():
        for v_ref, vx_ref in zip(v_refs, vx_refs):
            for hh in range(heads):
                vx_ref[hh, :, 0:dv] = v_ref[hh]
                vx_ref[hh, :, dv:2 * dv] = _ones_column(v_ref.shape[1], dv)

    c2 = scale * LOG2E
    for hh in range(heads):
        q = q_ref[hh]
        m = acc = None
        for src, start, size in chunks:
            s = lax.dot_general(q, k_refs[src][hh, start:start + size, :], (((1,), (1,)), ((), ())),
                                preferred_element_type=F32)
            m_chunk = jnp.max(s, axis=-1, keepdims=True)
            m_new = m_chunk if m is None else jnp.maximum(m, m_chunk)
            p = jnp.exp2((s - m_new) * c2)
            pv = jnp.dot(p.astype(BF16), vx_refs[src][hh, start:start + size, :], preferred_element_type=F32)
            acc = pv if acc is None else acc * jnp.exp2((m - m_new) * c2) + pv
            m = m_new
        o_ref[:, hh * dv:(hh + 1) * dv] = (acc[:, 0:dv] / acc[:, dv:dv + 1]).astype(o_ref.dtype)


def _head_major_spec(arr, batch, rows, off, tiled, heads):
    per_batch = arr.shape[2] * arr.shape[0] // batch // rows
    hb = off // heads
    if arr.shape[0] == 1:
        return pl.BlockSpec((None, heads, rows, arr.shape[3]),
                            lambda bi, h, i: (0, hb + h, bi * per_batch + (i if tiled else 0), 0))
    return pl.BlockSpec((None, heads, rows, arr.shape[3]), lambda bi, h, i: (bi, hb + h, i if tiled else 0, 0))


def _dense_attention(q, kv_list, nh, q_off, scale, name, batch, tq_target=1024, chunk_target=512, heads=2):
    dk = q.shape[3]
    sq = q.shape[0] * q.shape[2] // batch
    dv = kv_list[0][1].shape[3]
    offsets = [q_off] + [o for _, _, ko, vo in kv_list for o in (ko, vo)]
    heads = functools.reduce(math.gcd, [nh] + offsets, heads)
    tq = _pick_tile(sq, tq_target)
    nq = sq // tq
    chunks, arrays_k, arrays_v, scratch = [], [], [], []
    blocks = heads * (_nbytes((tq, dk), BF16) + _nbytes((tq, dv), BF16))
    tmp = 4 * _nbytes((tq, chunk_target), F32) + 2 * _nbytes((tq, 2 * dv), F32)
    kspecs, vspecs = [], []
    for src, (k, v, k_off, v_off) in enumerate(kv_list):
        sk = k.shape[0] * k.shape[2] // batch
        chunks += [(src, st, min(chunk_target, sk - st)) for st in range(0, sk, chunk_target)]
        kspecs.append(_head_major_spec(k, batch, sk, k_off, False, heads))
        vspecs.append(_head_major_spec(v, batch, sk, v_off, False, heads))
        arrays_k.append(k)
        arrays_v.append(v)
        scratch.append(pltpu.VMEM((heads, sk, 2 * dv), BF16))
        blocks += heads * (_nbytes((sk, dk), BF16) + _nbytes((sk, dv), BF16))
        tmp += heads * _nbytes((sk, 2 * dv), BF16)
    if q.shape[0] == 1:
        out_spec = pl.BlockSpec((None, tq, heads * dv), lambda bi, h, i: (0, bi * nq + i, h))
    else:
        out_spec = pl.BlockSpec((None, tq, heads * dv), lambda bi, h, i: (bi, i, h))
    return pl.pallas_call(
        functools.partial(_dense_attn_kernel, scale=scale, n_src=len(kv_list), chunks=chunks, heads=heads),
        grid=(batch, nh // heads, nq),
        in_specs=[_head_major_spec(q, batch, tq, q_off, True, heads)] + kspecs + vspecs,
        out_specs=out_spec,
        out_shape=jax.ShapeDtypeStruct((q.shape[0], q.shape[2], nh * dv), BF16),
        scratch_shapes=scratch,
        compiler_params=_params(blocks, tmp, 3),
        name=name,
    )(q, *arrays_k, *arrays_v)


def _pool_kernel(u_ref, w_ref, sc_ref, o_ref, *, windows):
    n, gd = u_ref.shape[0], w_ref.shape[1]
    t = lax.broadcasted_iota(jnp.int32, (n, gd), 0)
    for g, window in enumerate(windows):
        cols = slice(g * gd, (g + 1) * gd)
        u = u_ref[:, cols]
        half = window // 2
        ssum = u
        for j in range(1, half + 1):
            ssum = ssum + jnp.where(t >= j, pltpu.roll(u, j, axis=0), 0.0)
        for j in range(1, half):
            ssum = ssum + jnp.where(t < n - j, pltpu.roll(u, n - j, axis=0), 0.0)
        cnt = jnp.minimum(t + half, n) - jnp.maximum(t - half, 0)
        p = ssum / cnt.astype(F32) - u
        y = jnp.dot(p.astype(BF16), w_ref[g], preferred_element_type=F32)
        o_ref[:, cols] = (y * sc_ref[:, cols]).astype(o_ref.dtype)


def _pool_mix(u, w_pool, pool_scale):
    b, s, w = u.shape
    g, gd, _ = w_pool.shape
    blocks = _nbytes((s, w), F32) + _nbytes((g, gd, gd), BF16) + _nbytes((s, w), BF16)
    return pl.pallas_call(
        functools.partial(_pool_kernel, windows=POOL_WINDOWS[:g]),
        grid=(b,),
        in_specs=[
            pl.BlockSpec((None, s, w), lambda bi: (bi, 0, 0)),
            pl.BlockSpec((g, gd, gd), lambda bi: (0, 0, 0)),
            pl.BlockSpec((1, w), lambda bi: (0, 0)),
        ],
        out_specs=pl.BlockSpec((None, s, w), lambda bi: (bi, 0, 0)),
        out_shape=jax.ShapeDtypeStruct((b, s, w), BF16),
        compiler_params=_params(blocks, 6 * _nbytes((s, gd), F32), 1),
        name="pool_mix",
    )(u, w_pool.astype(BF16), pool_scale.reshape(1, w))


def _rope_table(n):
    half = QK_ROPE_DIM // 2
    nf = half // 2
    t = np.arange(n)
    inv = ROPE_BASE ** (-np.arange(nf, dtype=np.float64) / nf)
    ar = (t // GRID_W)[:, None] * inv[None, :]
    ac = (t % GRID_W)[:, None] * inv[None, :]
    cr, sr, cc, sc = np.cos(ar), np.sin(ar), np.cos(ac), np.sin(ac)
    return jnp.asarray(np.concatenate([cr, cr, cc, cc, -sr, sr, -sc, sc], axis=1), dtype=F32)


def _identity_rope_table(n):
    return jnp.concatenate([jnp.ones((n, QK_ROPE_DIM), F32), jnp.zeros((n, QK_ROPE_DIM), F32)], axis=1)


def _swap_rope_cols(w):
    nf = QK_ROPE_DIM // 4
    lead = w.shape[:-1]
    return jnp.flip(w.reshape(lead + (2, 2, nf)), axis=-2).reshape(lead + (QK_ROPE_DIM,))


def _even_mixer(h, hc, w_in, layer, w_four, rpb, with_ctx_out):
    fw = w_four.shape[0] * w_four.shape[1]
    nh = rpb.shape[0]
    u = _mm_plain(h, w_in, BF16, "even_in_fourier", layer, 0, fw, tm_target=2048)
    qkv = _mm_chunked(h, w_in, "even_in_qkv", layer, fw, tm_target=2048)
    qkvc = _mm_chunked(hc, w_in, "even_in_qkv_ctx", layer, fw)
    y_f = _four_channels(_dft_positions(u), w_four)
    y_a = _na_attention(qkv, qkvc, rpb)
    ctx_ops = None
    if with_ctx_out:
        b = h.shape[0]
        uc = _mm_plain(hc, w_in, BF16, "even_in_fourier_ctx", layer, 0, fw).reshape(b, -1, fw)
        yc_f = _four_channels(_dft_positions(uc), w_four).reshape(1, -1, fw)
        yc_a = _dense_attention(qkvc, [(qkvc, qkvc, nh, 2 * nh)], nh, 0, HEAD_DIM ** -0.5, "ctx_attention_even", b)
        ctx_ops = [yc_f, yc_a]
    return [y_f, y_a], ctx_ops


def _ep_odd_in(acc, e_refs, o_refs, *, splits):
    gq_ref, gkv_ref, tab_ref = e_refs
    u_ref, cq_ref, ckv_ref, kr_ref = o_refs
    o_q, o_kv, o_kr = splits

    def rms(v, g_ref):
        return (v * lax.rsqrt(jnp.mean(v * v, axis=-1, keepdims=True) + EPS) * g_ref[...]).astype(BF16)

    u_ref[...] = acc[:, 0:o_q]
    cq_ref[...] = rms(acc[:, o_q:o_kv], gq_ref)
    ckv_ref[...] = rms(acc[:, o_kv:o_kr], gkv_ref)
    r = _rope_pair(acc[:, o_kr:], tab_ref[...])
    lane = lax.broadcasted_iota(jnp.int32, r.shape, 1)
    kr_ref[...] = jnp.where(lane < QK_ROPE_DIM, r, 0.0).astype(kr_ref.dtype)


def _mm_odd_in(a, w, g_q, g_kv, tab, splits, name, tm_target=1024):
    b, s, k = a.shape
    n = w.shape[1]
    o_q, o_kv, o_kr = splits
    tm = _pick_tile(s, tm_target)

    def row(width):
        return pl.BlockSpec((None, tm, width), lambda bi, i, j: (bi, i, 0))

    def vec(width):
        return pl.BlockSpec((1, width), lambda bi, i, j: (0, 0))

    extras = [(g_q.reshape(1, -1), vec(o_kv - o_q)), (g_kv.reshape(1, -1), vec(o_kr - o_kv)),
              (tab, pl.BlockSpec((tm, n - o_kr), lambda bi, i, j: (i, 0)))]
    outs = [(jax.ShapeDtypeStruct((b, s, o_q), F32), row(o_q)),
            (jax.ShapeDtypeStruct((b, s, o_kv - o_q), BF16), row(o_kv - o_q)),
            (jax.ShapeDtypeStruct((b, s, o_kr - o_kv), BF16), row(o_kr - o_kv)),
            (jax.ShapeDtypeStruct((b, s, n - o_kr), BF16), row(n - o_kr))]
    return _mm_call([(a, _a_spec(tm, k))], [(_w3(w), _w_spec(k, n))], extras,
                    functools.partial(_ep_odd_in, splits=splits), outs, (b, s // tm, 1), tm, n, name)


def _odd_mixer(h, hc, w_in, w_pool, pool_scale, g_q, g_kv, w_uq, w_ukv, with_ctx_out):
    b, s, d = h.shape
    lc = hc.shape[1] // b
    pw = w_pool.shape[0] * w_pool.shape[1]
    q_rank, kv_rank = g_q.shape[0], g_kv.shape[0]
    nh = w_uq.shape[1] // (QK_NOPE_DIM + QK_ROPE_DIM)
    splits = (pw, pw + q_rank, pw + q_rank + kv_rank)
    w_in = w_in.astype(BF16)
    w_in2 = jnp.concatenate([w_in, _swap_rope_cols(w_in[:, splits[2]:])], axis=1)
    wq = w_uq.astype(BF16).reshape(q_rank, nh, QK_NOPE_DIM + QK_ROPE_DIM)
    wq_r = wq[..., QK_NOPE_DIM:]
    wq2 = jnp.concatenate([wq[..., :QK_NOPE_DIM], wq_r, _swap_rope_cols(wq_r)], axis=-1).reshape(q_rank, -1)
    w_ukv = w_ukv.astype(BF16)
    tab, tab_c = _rope_table(s), _identity_rope_table(b * lc)

    u, cq_n, ckv_n, k_rot = _mm_odd_in(h, w_in2, g_q, g_kv, tab, splits, "odd_in")
    uc, cqc_n, ckvc_n, k_rot_c = _mm_odd_in(hc, w_in2, g_q, g_kv, tab_c, splits, "odd_in_ctx")
    k_l, v_l = _mm_kv(ckv_n, w_ukv, k_rot, "odd_up_kv")
    k_c, v_c = _mm_kv(ckvc_n, w_ukv, k_rot_c, "odd_up_kv_ctx")
    q = _mm_qrope(cq_n, wq2, tab, "odd_up_q")
    scale = (QK_NOPE_DIM + QK_ROPE_DIM) ** -0.5
    attn = _dense_attention(q, [(k_l, v_l, 0, 0), (k_c, v_c, 0, 0)], nh, 0, scale, "mla_attention", b)
    pooled = _pool_mix(u, w_pool, pool_scale)
    ctx_ops = None
    if with_ctx_out:
        qc = _mm_qrope(cqc_n, wq2, tab_c, "odd_up_q_ctx")
        attn_c = _dense_attention(qc, [(k_c, v_c, 0, 0)], nh, 0, scale, "ctx_attention_odd", b)
        pooled_c = _pool_mix(uc.reshape(b, lc, pw), w_pool, pool_scale).reshape(1, b * lc, pw)
        ctx_ops = [pooled_c, attn_c]
    return [pooled, attn], ctx_ops


def _ffn(xs, norm_g, shift, scale, gate, wg, wu, wd, layer, tag, wd16=None):
    h2 = _norm_call(xs, norm_g, shift, scale)
    if wd16 is None:
        act, wd16 = _swiglu_up(h2, wg, wu, layer, "ffn_up_" + tag, wd)
    else:
        act = _swiglu_up(h2, wg, wu, layer, "ffn_up_" + tag)
    return _mm_residual([act], wd16, xs, gate, "ffn_down_" + tag), wd16


def kernel(x, c, ctx, c_ctx, w_mod, b_mod, norm1_g, norm2_g, w_in_ab, w_four, na_rpb, w_out_ab, w_in_cd, w_pool,
           pool_scale, mla_gq, mla_gkv, w_uq, w_ukv, w_out_cd, w_ffn_gate, w_ffn_up, w_ffn_down, final_g):
    b, s, d = x.shape
    depth = w_mod.shape[0]
    nrow = -(-(b + 1) // 8) * 8
    cvec = jnp.concatenate([c, c_ctx[None, :], jnp.zeros((nrow - b - 1, d), F32)], axis=0)
    mods = _mod_call(cvec, w_mod, b_mod)
    w_in_ab, w_out_ab, w_out_cd = w_in_ab.astype(BF16), w_out_ab.astype(BF16), w_out_cd.astype(BF16)
    wg, wu, wd = w_ffn_gate, w_ffn_up, w_ffn_down
    xc = ctx.reshape(1, -1, d)
    for l in range(depth):
        last = l == depth - 1
        mod = mods[l, :b].reshape(b, 1, 6 * d)
        modc = mods[l, b].reshape(1, 1, 6 * d)
        sh1, sc1, g1, sh2, sc2, g2 = jnp.split(mod, 6, axis=-1)
        csh1, csc1, cg1, csh2, csc2, cg2 = jnp.split(modc, 6, axis=-1)
        h = _norm_call(x, norm1_g[l], sh1, sc1)
        hc = _norm_call(xc, norm1_g[l], csh1, csc1)
        i = l // 2
        if l % 2 == 0:
            ops, ctx_ops = _even_mixer(h, hc, w_in_ab, i, w_four[i], na_rpb[i], not last)
            w_out = w_out_ab
        else:
            ops, ctx_ops = _odd_mixer(h, hc, w_in_cd[i], w_pool[i], pool_scale[i], mla_gq[i], mla_gkv[i],
                                      w_uq[i], w_ukv[i], not last)
            w_out = w_out_cd
        x = _mm_residual(ops, w_out, x, g1, f"out_proj_{l}", i, tn_target=1024)
        x, wd16 = _ffn(x, norm2_g[l], sh2, sc2, g2, wg, wu, wd, l, f"{l}")
        if not last:
            xc = _mm_residual(ctx_ops, w_out, xc, cg1, f"out_proj_ctx_{l}", i)
            xc, _ = _ffn(xc, norm2_g[l], csh2, csc2, cg2, wg, wu, wd, l, f"ctx_{l}", wd16)
    return _norm_call(x, final_g, out_dtype=F32)
```

```python
import functools
import math

import numpy as np
import jax
import jax.numpy as jnp
from jax import lax
from jax.experimental import pallas as pl
from jax.experimental.pallas import tpu as pltpu

GRID_W = 64
HEAD_DIM = 128
EPS = 1e-6
NEG_INF = -1e30
NA_WIN_H = 8
NA_WIN_W = 16
POOL_WINDOWS = (2, 4, 8, 16)
QK_NOPE_DIM = 128
QK_ROPE_DIM = 64
V_HEAD_DIM = 128
ROPE_BASE = 10000.0
LOG2E = 1.4426950408889634

V7X_LANES = 128
V7X_VMEM_BYTES = 64 * 1024 * 1024
V7X_VMEM_CAP = 56 * 1024 * 1024
NA_ROWS_PER_TILE = 4

BF16 = jnp.bfloat16
F32 = jnp.float32


def _pick_tile(n, target, mult=16):
    if n <= target:
        return n
    for t in range(target, 0, -1):
        if n % t == 0 and t % mult == 0:
            return t
    return n


def _nbytes(shape, dtype):
    return int(np.prod(shape)) * jnp.dtype(dtype).itemsize


def _params(block_bytes, scratch_bytes=0, ndims=3):
    est = 2 * block_bytes + scratch_bytes
    limit = int(min(max(est * 5 // 4 + (4 << 20), 16 << 20), V7X_VMEM_CAP))
    return pltpu.CompilerParams(dimension_semantics=("arbitrary",) * ndims, vmem_limit_bytes=limit)


def _mod_kernel(c_ref, w_ref, b_ref, o_ref):
    cv = c_ref[...]
    s = cv * (1.0 / (1.0 + jnp.exp(-cv)))
    acc = jnp.dot(s.astype(BF16), w_ref[...].astype(BF16), preferred_element_type=F32)
    o_ref[...] = acc + b_ref[...]


def _mod_call(cvec, w_mod, b_mod):
    depth, d, n = w_mod.shape
    r = cvec.shape[0]
    tn = _pick_tile(n, 1024, V7X_LANES)
    blocks = _nbytes((r, d), F32) + _nbytes((d, tn), F32) + 2 * _nbytes((r, tn), F32)
    return pl.pallas_call(
        _mod_kernel,
        grid=(depth, n // tn),
        in_specs=[
            pl.BlockSpec((r, d), lambda l, j: (0, 0)),
            pl.BlockSpec((None, d, tn), lambda l, j: (l, 0, j)),
            pl.BlockSpec((None, 1, tn), lambda l, j: (l, 0, j)),
        ],
        out_specs=pl.BlockSpec((None, r, tn), lambda l, j: (l, 0, j)),
        out_shape=jax.ShapeDtypeStruct((depth, r, n), F32),
        compiler_params=_params(blocks, _nbytes((d, tn), BF16), 2),
        name="mod_vectors",
    )(cvec, w_mod, b_mod.reshape(depth, 1, n))


def _norm_mod_kernel(x_ref, g_ref, sh_ref, sc_ref, o_ref):
    xf = x_ref[...]
    y = xf * lax.rsqrt(jnp.mean(xf * xf, axis=-1, keepdims=True) + EPS) * g_ref[...]
    o_ref[...] = (y * (1.0 + sc_ref[...]) + sh_ref[...]).astype(o_ref.dtype)


def _norm_kernel(x_ref, g_ref, o_ref):
    xf = x_ref[...]
    y = xf * lax.rsqrt(jnp.mean(xf * xf, axis=-1, keepdims=True) + EPS) * g_ref[...]
    o_ref[...] = y.astype(o_ref.dtype)


def _norm_call(x, g, shift=None, scale=None, out_dtype=BF16):
    b, s, d = x.shape
    ts = _pick_tile(s, 1024 if b * s >= 8192 else 256)
    xspec = pl.BlockSpec((None, ts, d), lambda bi, i: (bi, i, 0))
    gspec = pl.BlockSpec((1, d), lambda bi, i: (0, 0))
    mspec = pl.BlockSpec((None, 1, d), lambda bi, i: (bi, 0, 0))
    blocks = _nbytes((ts, d), F32) + _nbytes((ts, d), out_dtype) + 3 * _nbytes((1, d), F32)
    common = dict(
        grid=(b, s // ts),
        out_specs=xspec,
        out_shape=jax.ShapeDtypeStruct((b, s, d), out_dtype),
        compiler_params=_params(blocks, 2 * _nbytes((ts, d), F32), 2),
    )
    if shift is None:
        return pl.pallas_call(_norm_kernel, in_specs=[xspec, gspec], name="rmsnorm", **common)(
            x, g.reshape(1, d))
    return pl.pallas_call(_norm_mod_kernel, in_specs=[xspec, gspec, mspec, mspec], name="rmsnorm_modulate",
                          **common)(x, g.reshape(1, d), shift, scale)


def _mm_kernel(*refs, n_ops, n_extra, epilogue):
    a_refs = refs[:n_ops]
    w_refs = refs[n_ops:2 * n_ops]
    e_refs = refs[2 * n_ops:2 * n_ops + n_extra]
    o_refs = refs[2 * n_ops + n_extra:]
    acc = None
    for a_ref, w_ref in zip(a_refs, w_refs):
        part = jnp.dot(a_ref[...], w_ref[...], preferred_element_type=F32)
        acc = part if acc is None else acc + part
    epilogue(acc, e_refs, o_refs)


def _mm_call(a_ops, w_ops, extras, epilogue, outs, grid, tm, tn, name, tmp_bytes=0):
    arrays = [a for a, _ in a_ops] + [w for w, _ in w_ops] + [e for e, _ in extras]
    in_specs = [s for _, s in a_ops] + [s for _, s in w_ops] + [s for _, s in extras]
    blocks = 0
    for arr, spec in a_ops + w_ops + extras:
        blocks += _nbytes([d for d in spec.block_shape if d is not None], arr.dtype)
    for sds, spec in outs:
        blocks += _nbytes([d for d in spec.block_shape if d is not None], sds.dtype)
    kern = functools.partial(_mm_kernel, n_ops=len(a_ops), n_extra=len(extras), epilogue=epilogue)
    res = pl.pallas_call(
        kern,
        grid=grid,
        in_specs=in_specs,
        out_specs=[s for _, s in outs],
        out_shape=[o for o, _ in outs],
        compiler_params=_params(blocks, 3 * _nbytes((tm, tn), F32) + tmp_bytes, len(grid)),
        name=name,
    )(*arrays)
    return res


def _a_spec(tm, k, col_blk=0):
    return pl.BlockSpec((None, tm, k), lambda b, i, j: (b, i, col_blk))


def _w3(w):
    return w if w.ndim == 3 else w[None]


def _w_spec(k, tn, layer=0, row_blk=0, col_off=0):
    return pl.BlockSpec((None, k, tn), lambda b, i, j: (layer, row_blk, j + col_off))


def _tile_spec(tm, tn):
    return pl.BlockSpec((None, tm, tn), lambda b, i, j: (b, i, j))


def _vec_spec(tn):
    return pl.BlockSpec((None, 1, tn), lambda b, i, j: (b, 0, j))


def _ep_store(acc, e_refs, o_refs):
    o_refs[0][...] = acc.astype(o_refs[0].dtype)


def _col_tile(n, col0, target, mult):
    return _pick_tile(math.gcd(n, col0) if col0 else n, target, mult)


def _mm_plain(a, w, out_dtype, name, layer=0, col0=0, n=None, tm_target=1024, tn_target=512):
    b, s, k = a.shape
    w = _w3(w)
    n = w.shape[2] - col0 if n is None else n
    tm = _pick_tile(s, tm_target)
    tn = _col_tile(n, col0, tn_target, V7X_LANES)
    return _mm_call([(a, _a_spec(tm, k))], [(w, _w_spec(k, tn, layer, 0, col0 // tn))], [], _ep_store,
                    [(jax.ShapeDtypeStruct((b, s, n), out_dtype), _tile_spec(tm, tn))],
                    (b, s // tm, n // tn), tm, tn, name)[0]


def _ep_chunks(acc, e_refs, o_refs, *, width):
    o_ref = o_refs[0]
    for c in range(acc.shape[1] // width):
        o_ref[c] = acc[:, c * width:(c + 1) * width].astype(o_ref.dtype)


def _mm_chunked(a, w, name, layer=0, col0=0, n=None, width=HEAD_DIM, tm_target=1024, tn_target=512):
    b, s, k = a.shape
    w = _w3(w)
    n = w.shape[2] - col0 if n is None else n
    tm = _pick_tile(s, tm_target)
    tn = _col_tile(n, col0, tn_target, width)
    cpt = tn // width
    ospec = pl.BlockSpec((None, cpt, tm, width), lambda bi, i, j: (bi, j, i, 0))
    return _mm_call([(a, _a_spec(tm, k))], [(w, _w_spec(k, tn, layer, 0, col0 // tn))], [],
                    functools.partial(_ep_chunks, width=width),
                    [(jax.ShapeDtypeStruct((b, n // width, s, width), BF16), ospec)],
                    (b, s // tm, n // tn), tm, tn, name)[0]


def _ep_residual(acc, e_refs, o_refs):
    x_ref, g_ref = e_refs
    o_refs[0][...] = x_ref[...] + g_ref[...] * acc


def _mm_residual(a_list, w, x, gate, name, layer=0, tm_target=1024, tn_target=512):
    b, s, n = x.shape
    w = _w3(w)
    tm = _pick_tile(s, tm_target)
    tn = _pick_tile(n, tn_target, V7X_LANES)
    kblk = functools.reduce(math.gcd, [a.shape[2] for a in a_list])
    a_ops, w_ops = [], []
    for a in a_list:
        for cb in range(a.shape[2] // kblk):
            a_ops.append((a, _a_spec(tm, kblk, cb)))
            w_ops.append((w, _w_spec(kblk, tn, layer, len(w_ops))))
    extras = [(x, _tile_spec(tm, tn)), (gate, _vec_spec(tn))]
    return _mm_call(a_ops, w_ops, extras, _ep_residual,
                    [(jax.ShapeDtypeStruct((b, s, n), F32), _tile_spec(tm, tn))],
                    (b, s // tm, n // tn), tm, tn, name)[0]


def _swiglu_kernel(a_ref, wg_ref, wu_ref, o_ref, wgb_ref, wub_ref):
    @pl.when((pl.program_id(1) == 0) & (pl.program_id(2) == 0))
    def _():
        wgb_ref[...] = wg_ref[...].astype(BF16)
        wub_ref[...] = wu_ref[...].astype(BF16)

    a = a_ref[...]
    g = jnp.dot(a, wgb_ref[...], preferred_element_type=F32)
    u = jnp.dot(a, wub_ref[...], preferred_element_type=F32)
    o_ref[...] = (g * (1.0 / (1.0 + jnp.exp(-g))) * u).astype(o_ref.dtype)


def _swiglu_up(a, wg, wu, layer, name, tm_target=1024, tn_target=512):
    b, s, k = a.shape
    n = wg.shape[2]
    tm = _pick_tile(s, tm_target)
    tn = _pick_tile(n, tn_target, V7X_LANES)
    blocks = _nbytes((tm, k), BF16) + 2 * _nbytes((k, tn), F32) + _nbytes((tm, tn), BF16)
    wspec = pl.BlockSpec((None, k, tn), lambda j, bi, i: (layer, 0, j))
    return pl.pallas_call(
        _swiglu_kernel,
        grid=(n // tn, b, s // tm),
        in_specs=[pl.BlockSpec((None, tm, k), lambda j, bi, i: (bi, i, 0)), wspec, wspec],
        out_specs=pl.BlockSpec((None, tm, tn), lambda j, bi, i: (bi, i, j)),
        out_shape=jax.ShapeDtypeStruct((b, s, n), BF16),
        scratch_shapes=[pltpu.VMEM((k, tn), BF16), pltpu.VMEM((k, tn), BF16)],
        compiler_params=_params(blocks, 2 * _nbytes((k, tn), BF16) + 4 * _nbytes((tm, tn), F32), 3),
        name=name,
    )(a, wg, wu)


def _rope_pair(hi, tab):
    z = hi * tab
    return z + pltpu.roll(z, QK_ROPE_DIM, axis=1)


def _ep_qrope(acc, e_refs, o_refs, *, heads):
    tab_ref, = e_refs
    o_ref = o_refs[0]
    tab = tab_ref[...]
    blk = QK_NOPE_DIM + 2 * QK_ROPE_DIM
    for h in range(heads):
        o_ref[h, :, 0:QK_NOPE_DIM] = acc[:, h * blk:h * blk + QK_NOPE_DIM].astype(o_ref.dtype)
        hi = acc[:, h * blk + QK_NOPE_DIM:(h + 1) * blk]
        o_ref[h, :, QK_NOPE_DIM:blk] = _rope_pair(hi, tab).astype(o_ref.dtype)


def _mm_qrope(a, w, tab, name, tm_target=1024, heads_per_tile=6):
    b, s, k = a.shape
    blk = QK_NOPE_DIM + 2 * QK_ROPE_DIM
    nh = w.shape[1] // blk
    hpt = math.gcd(nh, heads_per_tile)
    tn = hpt * blk
    tm = _pick_tile(s, tm_target)
    tspec = pl.BlockSpec((tm, 2 * QK_ROPE_DIM), lambda bi, i, j: (i, 0))
    ospec = pl.BlockSpec((None, hpt, tm, blk), lambda bi, i, j: (bi, j, i, 0))
    return _mm_call([(a, _a_spec(tm, k))], [(_w3(w), _w_spec(k, tn))], [(tab, tspec)],
                    functools.partial(_ep_qrope, heads=hpt),
                    [(jax.ShapeDtypeStruct((b, nh, s, blk), BF16), ospec)],
                    (b, s // tm, nh // hpt), tm, tn, name)[0]


def _ep_kv(acc, e_refs, o_refs, *, heads):
    kr_ref, = e_refs
    k_ref, v_ref = o_refs
    kr = kr_ref[...]
    blk = QK_NOPE_DIM + V_HEAD_DIM
    for h in range(heads):
        k_ref[h, :, 0:QK_NOPE_DIM] = acc[:, h * blk:h * blk + QK_NOPE_DIM].astype(k_ref.dtype)
        k_ref[h, :, QK_NOPE_DIM:QK_NOPE_DIM + kr.shape[1]] = kr
        v_ref[h] = acc[:, h * blk + QK_NOPE_DIM:(h + 1) * blk].astype(v_ref.dtype)


def _mm_kv(a, w, krot, name, tm_target=1024, heads_per_tile=6):
    b, s, k = a.shape
    blk = QK_NOPE_DIM + V_HEAD_DIM
    nh = w.shape[1] // blk
    hpt = math.gcd(nh, heads_per_tile)
    tn = hpt * blk
    tm = _pick_tile(s, tm_target)
    kd = QK_NOPE_DIM + krot.shape[2]
    krspec = pl.BlockSpec((None, tm, krot.shape[2]), lambda bi, i, j: (bi, i, 0))
    kspec = pl.BlockSpec((None, hpt, tm, kd), lambda bi, i, j: (bi, j, i, 0))
    vspec = pl.BlockSpec((None, hpt, tm, V_HEAD_DIM), lambda bi, i, j: (bi, j, i, 0))
    return _mm_call([(a, _a_spec(tm, k))], [(_w3(w), _w_spec(k, tn))], [(krot, krspec)],
                    functools.partial(_ep_kv, heads=hpt),
                    [(jax.ShapeDtypeStruct((b, nh, s, kd), BF16), kspec),
                     (jax.ShapeDtypeStruct((b, nh, s, V_HEAD_DIM), BF16), vspec)],
                    (b, s // tm, nh // hpt), tm, tn, name)


def _dft_matrix(n):
    m = jnp.arange(n, dtype=jnp.int32)
    side = 1
    while side * side < n:
        side *= 2
    if side * side != n:
        k = jnp.arange(n, dtype=jnp.int32)
        ang = (2.0 * math.pi / n) * ((k[:, None] * m[None, :]) % n).astype(F32)
        return jnp.concatenate([jnp.cos(ang), jnp.sin(ang)], axis=0).astype(BF16)[None]
    a = jnp.arange(side, dtype=jnp.int32)
    ang_x = (2.0 * math.pi / side) * ((a[:, None] * m[None, :]) % side).astype(F32)
    ang_y = (2.0 * math.pi / n) * ((a[:, None] * m[None, :]) % n).astype(F32)
    cx, sx = jnp.cos(ang_x), jnp.sin(ang_x)
    cy, sy = jnp.cos(ang_y)[None], jnp.sin(ang_y)[None]
    p = jnp.concatenate([cx, sx], axis=0)[:, None, :]
    q = jnp.concatenate([sx, -cx], axis=0)[:, None, :]
    return (p * cy - q * sy).astype(BF16).reshape(1, 2 * n, n)


def _dft_positions_dense(u):
    b, s, w = u.shape
    mat = _dft_matrix(s)
    tm = _pick_tile(2 * s, 1024)
    tn = _pick_tile(w, 512, V7X_LANES)
    aspec = pl.BlockSpec((None, tm, s), lambda bi, i, j: (0, i, 0))
    wspec = pl.BlockSpec((None, s, tn), lambda bi, i, j: (bi, 0, j))
    return _mm_call([(mat, aspec)], [(u, wspec)], [], _ep_store,
                    [(jax.ShapeDtypeStruct((b, 2 * s, w), BF16), _tile_spec(tm, tn))],
                    (b, 2 * s // tm, w // tn), tm, tn, "fourier_positions")[0]


DFT_ROWS_PER_STEP = 16


def _dft_stage1_kernel(x_ref, a_ref, tc_ref, ts_ref, o_ref, *, side, width, n2_per_step):
    g = jnp.dot(a_ref[...], x_ref[...], preferred_element_type=F32)
    gr, gs = g[0:side], g[side:2 * side]
    for n2 in range(n2_per_step):
        tcs = tc_ref[:, n2 * V7X_LANES:(n2 + 1) * V7X_LANES]
        tss = ts_ref[:, n2 * V7X_LANES:(n2 + 1) * V7X_LANES]
        for c0 in range(n2 * width, (n2 + 1) * width, V7X_LANES):
            cols = slice(c0, c0 + V7X_LANES)
            o_ref[0:side, cols] = (gr[:, cols] * tcs - gs[:, cols] * tss).astype(o_ref.dtype)
            o_ref[side:2 * side, cols] = (gr[:, cols] * tss + gs[:, cols] * tcs).astype(o_ref.dtype)


def _dft_stage2_kernel(h_ref, m_ref, o_ref, scr_ref, *, side):
    rows = h_ref.shape[1]
    ngrp = scr_ref.shape[0]
    for j in range(rows):
        hcat = jnp.concatenate([h_ref[0, j], h_ref[1, j]], axis=0)
        res = jnp.dot(m_ref[...], hcat, preferred_element_type=F32)
        for c in range(ngrp):
            cols = slice(c * V7X_LANES, (c + 1) * V7X_LANES)
            scr_ref[c, pl.ds(j, side, stride=rows), :] = res[0:side, cols]
            scr_ref[c, pl.ds(side * rows + j, side, stride=rows), :] = res[side:2 * side, cols]
    for c in range(ngrp):
        cols = slice(c * V7X_LANES, (c + 1) * V7X_LANES)
        for p in range(2):
            part = scr_ref[c, p * side * rows:(p + 1) * side * rows, :].reshape(side, rows, V7X_LANES)
            o_ref[p, :, :, cols] = part.astype(o_ref.dtype)


def _dft_positions(u):
    b, s, w = u.shape
    side = math.isqrt(s)
    if side * side != s or side % DFT_ROWS_PER_STEP or w % V7X_LANES:
        return _dft_positions_dense(u)
    idx = np.arange(side)
    ang = 2.0 * np.pi * ((idx[:, None] * idx[None, :]) % side) / side
    cs, sn = np.cos(ang), np.sin(ang)
    a1 = jnp.asarray(np.concatenate([cs, sn], axis=0), dtype=F32).astype(BF16)
    m2 = jnp.asarray(np.block([[cs, -sn], [sn, cs]]), dtype=F32).astype(BF16)
    tw = 2.0 * np.pi * (idx[:, None] * idx[None, :]) / s
    tcos = jnp.repeat(jnp.asarray(np.cos(tw), dtype=F32), V7X_LANES, axis=1)
    tsin = jnp.repeat(jnp.asarray(np.sin(tw), dtype=F32), V7X_LANES, axis=1)
    n2s = DFT_ROWS_PER_STEP
    tc = n2s * w
    blocks1 = _nbytes((side, tc), BF16) + _nbytes((2 * side, side), BF16) + 2 * _nbytes((side, n2s * V7X_LANES), F32) \
        + _nbytes((2 * side, tc), BF16)
    h = pl.pallas_call(
        functools.partial(_dft_stage1_kernel, side=side, width=w, n2_per_step=n2s),
        grid=(b, side // n2s),
        in_specs=[
            pl.BlockSpec((None, side, tc), lambda bi, j: (bi, 0, j)),
            pl.BlockSpec((2 * side, side), lambda bi, j: (0, 0)),
            pl.BlockSpec((side, n2s * V7X_LANES), lambda bi, j: (0, j)),
            pl.BlockSpec((side, n2s * V7X_LANES), lambda bi, j: (0, j)),
        ],
        out_specs=pl.BlockSpec((None, 2 * side, tc), lambda bi, j: (bi, 0, j)),
        out_shape=jax.ShapeDtypeStruct((b, 2 * side, side * w), BF16),
        compiler_params=_params(blocks1, 3 * _nbytes((2 * side, tc), F32), 2),
        name="fourier_positions_stage1",
    )(u.reshape(b, side, side * w), a1, tcos, tsin)
    rows = DFT_ROWS_PER_STEP
    blocks2 = 2 * _nbytes((2, rows, side, w), BF16) + _nbytes((2 * side, 2 * side), BF16)
    f2 = pl.pallas_call(
        functools.partial(_dft_stage2_kernel, side=side),
        grid=(b, side // rows),
        in_specs=[
            pl.BlockSpec((None, 2, rows, side, w), lambda bi, i: (bi, 0, i, 0, 0)),
            pl.BlockSpec((2 * side, 2 * side), lambda bi, i: (0, 0)),
        ],
        out_specs=pl.BlockSpec((None, 2, side, rows, w), lambda bi, i: (bi, 0, 0, i, 0)),
        out_shape=jax.ShapeDtypeStruct((b, 2, side, side, w), BF16),
        scratch_shapes=[pltpu.VMEM((w // V7X_LANES, 2 * side * rows, V7X_LANES), F32)],
        compiler_params=_params(blocks2, _nbytes((2 * side * rows, w), F32) + 4 * _nbytes((2 * side, w), F32), 2),
        name="fourier_positions_stage2",
    )(h.reshape(b, 2, side, side, w), m2)
    return f2.reshape(b, 2 * s, w)


def _four_channel_kernel(fr_ref, fi_ref, cs_ref, w_ref, o_ref, *, norm):
    gd = w_ref.shape[1]
    for g in range(w_ref.shape[0]):
        cols = slice(g * gd, (g + 1) * gd)
        f = jnp.dot(fr_ref[:, cols].astype(BF16), cs_ref[0:gd, :], preferred_element_type=F32)
        f = f + jnp.dot(fi_ref[:, cols].astype(BF16), cs_ref[gd:2 * gd, :], preferred_element_type=F32)
        f = (f * norm).astype(BF16)
        o_ref[:, cols] = jnp.dot(f, w_ref[g], preferred_element_type=F32).astype(o_ref.dtype)


def _four_channels(f2, w_four):
    b, s2, w = f2.shape
    s = s2 // 2
    g, gd, _ = w_four.shape
    idx = np.arange(gd)
    ang = 2.0 * np.pi * ((idx[:, None] * idx[None, :]) % gd) / gd
    cs = jnp.asarray(np.concatenate([np.cos(ang), -np.sin(ang)], axis=0), dtype=F32).astype(BF16)
    ts = _pick_tile(s, 1024)
    nblk = s // ts
    blocks = 2 * _nbytes((ts, w), f2.dtype) + _nbytes((ts, w), BF16) + _nbytes((2 * gd, gd), BF16) \
        + _nbytes((g, gd, gd), BF16)
    return pl.pallas_call(
        functools.partial(_four_channel_kernel, norm=1.0 / math.sqrt(s * gd)),
        grid=(b, nblk),
        in_specs=[
            pl.BlockSpec((None, ts, w), lambda bi, i: (bi, i, 0)),
            pl.BlockSpec((None, ts, w), lambda bi, i: (bi, nblk + i, 0)),
            pl.BlockSpec((2 * gd, gd), lambda bi, i: (0, 0)),
            pl.BlockSpec((g, gd, gd), lambda bi, i: (0, 0, 0)),
        ],
        out_specs=pl.BlockSpec((None, ts, w), lambda bi, i: (bi, i, 0)),
        out_shape=jax.ShapeDtypeStruct((b, s, w), BF16),
        compiler_params=_params(blocks, 6 * _nbytes((ts, gd), F32), 2),
        name="fourier_channels",
    )(f2, f2, cs, w_four.astype(BF16))


def _na_plan(rows):
    rpt = NA_ROWS_PER_TILE
    kh = min(NA_WIN_H, rows)
    wrows = min(kh + rpt - 1, rows)
    cases, ws_list, case_list = {}, [], []
    for t in range(rows // rpt):
        r0 = t * rpt
        ws = int(np.clip(r0 - kh // 2, 0, rows - wrows))
        plan = []
        for r in range(r0, r0 + rpt):
            rs = int(np.clip(r - kh // 2, 0, rows - kh))
            plan.append(tuple((ws + w) - r + NA_WIN_H - 1 if rs <= ws + w < rs + kh else None
                              for w in range(wrows)))
        plan = tuple(plan)
        cases.setdefault(plan, len(cases))
        ws_list.append(ws)
        case_list.append(cases[plan])
    return wrows, ws_list, case_list, list(cases)


def _na_block_table(rpb):
    qcol = np.arange(GRID_W)[:, None]
    kcol = np.arange(GRID_W)[None, :]
    cstart = np.clip(qcol - NA_WIN_W // 2, 0, GRID_W - NA_WIN_W)
    col_ok = (kcol >= cstart) & (kcol < cstart + NA_WIN_W)
    cidx = np.clip(kcol - qcol + NA_WIN_W - 1, 0, 2 * NA_WIN_W - 2)
    csel = jnp.asarray(np.eye(rpb.shape[2], dtype=np.float32)[cidx])
    blk = jnp.einsum("hrs,qks->hrqk", rpb, csel, precision=lax.Precision.HIGHEST)
    blk = jnp.where(jnp.asarray(col_ok), blk * LOG2E, NEG_INF)
    return jnp.concatenate([blk, blk], axis=-1)


def _ones_column(rows, width):
    lane = lax.broadcasted_iota(jnp.int32, (rows, width), 1)
    return jnp.where(lane == 0, 1.0, 0.0).astype(BF16)


def _na_kernel(q_ref, k_ref, v_ref, kc_ref, vc_ref, blk_ref, o_ref, bias_ref, vx_ref, vcx_ref,
               *, tiles, tq, tk, scale, plans):
    d = v_ref.shape[1]

    @pl.when(pl.program_id(1) == 0)
    def _():
        for c, plan in enumerate(plans):
            for qr, row in enumerate(plan):
                for kr, ridx in enumerate(row):
                    half = slice((kr % 2) * GRID_W, (kr % 2 + 1) * GRID_W)
                    if ridx is None:
                        blk = jnp.full((GRID_W, GRID_W), NEG_INF, F32)
                    else:
                        blk = blk_ref[ridx][:, half]
                    bias_ref[c, qr * GRID_W:(qr + 1) * GRID_W, kr * GRID_W:(kr + 1) * GRID_W] = blk

    vx_ref[:, 0:d] = v_ref[...]
    vx_ref[:, d:2 * d] = _ones_column(v_ref.shape[0], d)
    vcx_ref[:, 0:d] = vc_ref[...]
    vcx_ref[:, d:2 * d] = _ones_column(vc_ref.shape[0], d)
    kc = kc_ref[...]
    dn = (((1,), (1,)), ((), ()))
    c2 = scale * LOG2E

    for t, (ws, case) in enumerate(tiles):
        q0, k0 = t * tq, ws * GRID_W
        q = q_ref[q0:q0 + tq, :]
        s_loc = lax.dot_general(q, k_ref[k0:k0 + tk, :], dn, preferred_element_type=F32) * c2 + bias_ref[case]
        s_ctx = lax.dot_general(q, kc, dn, preferred_element_type=F32) * c2
        m = jnp.maximum(jnp.max(s_loc, axis=-1, keepdims=True), jnp.max(s_ctx, axis=-1, keepdims=True))
        p_loc = jnp.exp2(s_loc - m).astype(BF16)
        p_ctx = jnp.exp2(s_ctx - m).astype(BF16)
        acc = jnp.dot(p_loc, vx_ref[k0:k0 + tk, :], preferred_element_type=F32)
        acc = acc + jnp.dot(p_ctx, vcx_ref[...], preferred_element_type=F32)
        o_ref[q0:q0 + tq, :] = (acc[:, 0:d] / acc[:, d:d + 1]).astype(o_ref.dtype)


def _na_attention(qkv, qkvc, rpb):
    b, c3, s, d = qkv.shape
    nh = c3 // 3
    lc = qkvc.shape[2] // b
    rows = s // GRID_W
    wrows, ws_list, case_list, plans = _na_plan(rows)
    blk = _na_block_table(rpb)
    nrel = blk.shape[1]
    tq = NA_ROWS_PER_TILE * GRID_W
    tk = wrows * GRID_W
    blocks = 3 * _nbytes((s, d), BF16) + 2 * _nbytes((lc, d), BF16) + _nbytes((nrel, GRID_W, 2 * GRID_W), F32) \
        + _nbytes((s, d), BF16)
    scratch = _nbytes((len(plans), tq, tk), F32) + _nbytes((s + lc, 2 * d), BF16)
    return pl.pallas_call(
        functools.partial(_na_kernel, tiles=list(zip(ws_list, case_list)), tq=tq, tk=tk, scale=d ** -0.5,
                          plans=plans),
        grid=(nh, b),
        in_specs=[
            pl.BlockSpec((None, None, s, d), lambda h, bi: (bi, h, 0, 0)),
            pl.BlockSpec((None, None, s, d), lambda h, bi: (bi, nh + h, 0, 0)),
            pl.BlockSpec((None, None, s, d), lambda h, bi: (bi, 2 * nh + h, 0, 0)),
            pl.BlockSpec((None, None, lc, d), lambda h, bi: (0, nh + h, bi, 0)),
            pl.BlockSpec((None, None, lc, d), lambda h, bi: (0, 2 * nh + h, bi, 0)),
            pl.BlockSpec((None, nrel, GRID_W, 2 * GRID_W), lambda h, bi: (h, 0, 0, 0)),
        ],
        out_specs=pl.BlockSpec((None, s, d), lambda h, bi: (bi, 0, h)),
        out_shape=jax.ShapeDtypeStruct((b, s, nh * d), BF16),
        scratch_shapes=[pltpu.VMEM((len(plans), tq, tk), F32), pltpu.VMEM((s, 2 * d), BF16),
                        pltpu.VMEM((lc, 2 * d), BF16)],
        compiler_params=_params(blocks, scratch + 8 * _nbytes((tq, tk + lc), F32), 2),
        name="neighbourhood_attention",
    )(qkv, qkv, qkv, qkvc, qkvc, blk)


def _dense_attn_kernel(*refs, scale, n_src, chunks, heads):
    q_ref = refs[0]
    k_refs = refs[1:1 + n_src]
    v_refs = refs[1 + n_src:1 + 2 * n_src]
    o_ref = refs[1 + 2 * n_src]
    vx_refs = refs[2 + 2 * n_src:]
    dv = v_refs[0].shape[2]

    @pl.when(pl.program_id(2) == 0)
    def _():
        for v_ref, vx_ref in zip(v_refs, vx_refs):
            for hh in range(heads):
                vx_ref[hh, :, 0:dv] = v_ref[hh]
                vx_ref[hh, :, dv:2 * dv] = _ones_column(v_ref.shape[1], dv)

    c2 = scale * LOG2E
    for hh in range(heads):
        q = q_ref[hh]
        m = acc = None
        for src, start, size in chunks:
            s = lax.dot_general(q, k_refs[src][hh, start:start + size, :], (((1,), (1,)), ((), ())),
                                preferred_element_type=F32)
            m_chunk = jnp.max(s, axis=-1, keepdims=True)
            m_new = m_chunk if m is None else jnp.maximum(m, m_chunk)
            p = jnp.exp2((s - m_new) * c2)
            pv = jnp.dot(p.astype(BF16), vx_refs[src][hh, start:start + size, :], preferred_element_type=F32)
            acc = pv if acc is None else acc * jnp.exp2((m - m_new) * c2) + pv
            m = m_new
        o_ref[:, hh * dv:(hh + 1) * dv] = (acc[:, 0:dv] / acc[:, dv:dv + 1]).astype(o_ref.dtype)


def _head_major_spec(arr, batch, rows, off, tiled, heads):
    per_batch = arr.shape[2] * arr.shape[0] // batch // rows
    hb = off // heads
    if arr.shape[0] == 1:
        return pl.BlockSpec((None, heads, rows, arr.shape[3]),
                            lambda bi, h, i: (0, hb + h, bi * per_batch + (i if tiled else 0), 0))
    return pl.BlockSpec((None, heads, rows, arr.shape[3]), lambda bi, h, i: (bi, hb + h, i if tiled else 0, 0))


def _dense_attention(q, kv_list, nh, q_off, scale, name, batch, tq_target=1024, chunk_target=512, heads=2):
    dk = q.shape[3]
    sq = q.shape[0] * q.shape[2] // batch
    dv = kv_list[0][1].shape[3]
    offsets = [q_off] + [o for _, _, ko, vo in kv_list for o in (ko, vo)]
    heads = functools.reduce(math.gcd, [nh] + offsets, heads)
    tq = _pick_tile(sq, tq_target)
    nq = sq // tq
    chunks, arrays_k, arrays_v, scratch = [], [], [], []
    blocks = heads * (_nbytes((tq, dk), BF16) + _nbytes((tq, dv), BF16))
    tmp = 4 * _nbytes((tq, chunk_target), F32) + 2 * _nbytes((tq, 2 * dv), F32)
    kspecs, vspecs = [], []
    for src, (k, v, k_off, v_off) in enumerate(kv_list):
        sk = k.shape[0] * k.shape[2] // batch
        chunks += [(src, st, min(chunk_target, sk - st)) for st in range(0, sk, chunk_target)]
        kspecs.append(_head_major_spec(k, batch, sk, k_off, False, heads))
        vspecs.append(_head_major_spec(v, batch, sk, v_off, False, heads))
        arrays_k.append(k)
        arrays_v.append(v)
        scratch.append(pltpu.VMEM((heads, sk, 2 * dv), BF16))
        blocks += heads * (_nbytes((sk, dk), BF16) + _nbytes((sk, dv), BF16))
        tmp += heads * _nbytes((sk, 2 * dv), BF16)
    if q.shape[0] == 1:
        out_spec = pl.BlockSpec((None, tq, heads * dv), lambda bi, h, i: (0, bi * nq + i, h))
    else:
        out_spec = pl.BlockSpec((None, tq, heads * dv), lambda bi, h, i: (bi, i, h))
    return pl.pallas_call(
        functools.partial(_dense_attn_kernel, scale=scale, n_src=len(kv_list), chunks=chunks, heads=heads),
        grid=(batch, nh // heads, nq),
        in_specs=[_head_major_spec(q, batch, tq, q_off, True, heads)] + kspecs + vspecs,
        out_specs=out_spec,
        out_shape=jax.ShapeDtypeStruct((q.shape[0], q.shape[2], nh * dv), BF16),
        scratch_shapes=scratch,
        compiler_params=_params(blocks, tmp, 3),
        name=name,
    )(q, *arrays_k, *arrays_v)


def _pool_kernel(u_ref, w_ref, sc_ref, o_ref, *, windows):
    n, gd = u_ref.shape[0], w_ref.shape[1]
    t = lax.broadcasted_iota(jnp.int32, (n, gd), 0)
    for g, window in enumerate(windows):
        cols = slice(g * gd, (g + 1) * gd)
        u = u_ref[:, cols]
        half = window // 2
        ssum = u
        for j in range(1, half + 1):
            ssum = ssum + jnp.where(t >= j, pltpu.roll(u, j, axis=0), 0.0)
        for j in range(1, half):
            ssum = ssum + jnp.where(t < n - j, pltpu.roll(u, n - j, axis=0), 0.0)
        cnt = jnp.minimum(t + half, n) - jnp.maximum(t - half, 0)
        p = ssum / cnt.astype(F32) - u
        y = jnp.dot(p.astype(BF16), w_ref[g], preferred_element_type=F32)
        o_ref[:, cols] = (y * sc_ref[:, cols]).astype(o_ref.dtype)


def _pool_mix(u, w_pool, pool_scale):
    b, s, w = u.shape
    g, gd, _ = w_pool.shape
    blocks = _nbytes((s, w), F32) + _nbytes((g, gd, gd), BF16) + _nbytes((s, w), BF16)
    return pl.pallas_call(
        functools.partial(_pool_kernel, windows=POOL_WINDOWS[:g]),
        grid=(b,),
        in_specs=[
            pl.BlockSpec((None, s, w), lambda bi: (bi, 0, 0)),
            pl.BlockSpec((g, gd, gd), lambda bi: (0, 0, 0)),
            pl.BlockSpec((1, w), lambda bi: (0, 0)),
        ],
        out_specs=pl.BlockSpec((None, s, w), lambda bi: (bi, 0, 0)),
        out_shape=jax.ShapeDtypeStruct((b, s, w), BF16),
        compiler_params=_params(blocks, 6 * _nbytes((s, gd), F32), 1),
        name="pool_mix",
    )(u, w_pool.astype(BF16), pool_scale.reshape(1, w))


def _rope_table(n):
    half = QK_ROPE_DIM // 2
    nf = half // 2
    t = np.arange(n)
    inv = ROPE_BASE ** (-np.arange(nf, dtype=np.float64) / nf)
    ar = (t // GRID_W)[:, None] * inv[None, :]
    ac = (t % GRID_W)[:, None] * inv[None, :]
    cr, sr, cc, sc = np.cos(ar), np.sin(ar), np.cos(ac), np.sin(ac)
    return jnp.asarray(np.concatenate([cr, cr, cc, cc, -sr, sr, -sc, sc], axis=1), dtype=F32)


def _identity_rope_table(n):
    return jnp.concatenate([jnp.ones((n, QK_ROPE_DIM), F32), jnp.zeros((n, QK_ROPE_DIM), F32)], axis=1)


def _swap_rope_cols(w):
    nf = QK_ROPE_DIM // 4
    lead = w.shape[:-1]
    return jnp.flip(w.reshape(lead + (2, 2, nf)), axis=-2).reshape(lead + (QK_ROPE_DIM,))


def _even_mixer(h, hc, w_in, layer, w_four, rpb, with_ctx_out):
    fw = w_four.shape[0] * w_four.shape[1]
    nh = rpb.shape[0]
    u = _mm_plain(h, w_in, BF16, "even_in_fourier", layer, 0, fw, tm_target=2048)
    qkv = _mm_chunked(h, w_in, "even_in_qkv", layer, fw, tm_target=2048)
    qkvc = _mm_chunked(hc, w_in, "even_in_qkv_ctx", layer, fw)
    y_f = _four_channels(_dft_positions(u), w_four)
    y_a = _na_attention(qkv, qkvc, rpb)
    ctx_ops = None
    if with_ctx_out:
        b = h.shape[0]
        uc = _mm_plain(hc, w_in, BF16, "even_in_fourier_ctx", layer, 0, fw).reshape(b, -1, fw)
        yc_f = _four_channels(_dft_positions(uc), w_four).reshape(1, -1, fw)
        yc_a = _dense_attention(qkvc, [(qkvc, qkvc, nh, 2 * nh)], nh, 0, HEAD_DIM ** -0.5, "ctx_attention_even", b)
        ctx_ops = [yc_f, yc_a]
    return [y_f, y_a], ctx_ops


def _ep_odd_in(acc, e_refs, o_refs, *, splits):
    gq_ref, gkv_ref, tab_ref = e_refs
    u_ref, cq_ref, ckv_ref, kr_ref = o_refs
    o_q, o_kv, o_kr = splits

    def rms(v, g_ref):
        return (v * lax.rsqrt(jnp.mean(v * v, axis=-1, keepdims=True) + EPS) * g_ref[...]).astype(BF16)

    u_ref[...] = acc[:, 0:o_q]
    cq_ref[...] = rms(acc[:, o_q:o_kv], gq_ref)
    ckv_ref[...] = rms(acc[:, o_kv:o_kr], gkv_ref)
    r = _rope_pair(acc[:, o_kr:], tab_ref[...])
    lane = lax.broadcasted_iota(jnp.int32, r.shape, 1)
    kr_ref[...] = jnp.where(lane < QK_ROPE_DIM, r, 0.0).astype(kr_ref.dtype)


def _mm_odd_in(a, w, g_q, g_kv, tab, splits, name, tm_target=1024):
    b, s, k = a.shape
    n = w.shape[1]
    o_q, o_kv, o_kr = splits
    tm = _pick_tile(s, tm_target)

    def row(width):
        return pl.BlockSpec((None, tm, width), lambda bi, i, j: (bi, i, 0))

    def vec(width):
        return pl.BlockSpec((1, width), lambda bi, i, j: (0, 0))

    extras = [(g_q.reshape(1, -1), vec(o_kv - o_q)), (g_kv.reshape(1, -1), vec(o_kr - o_kv)),
              (tab, pl.BlockSpec((tm, n - o_kr), lambda bi, i, j: (i, 0)))]
    outs = [(jax.ShapeDtypeStruct((b, s, o_q), F32), row(o_q)),
            (jax.ShapeDtypeStruct((b, s, o_kv - o_q), BF16), row(o_kv - o_q)),
            (jax.ShapeDtypeStruct((b, s, o_kr - o_kv), BF16), row(o_kr - o_kv)),
            (jax.ShapeDtypeStruct((b, s, n - o_kr), BF16), row(n - o_kr))]
    return _mm_call([(a, _a_spec(tm, k))], [(_w3(w), _w_spec(k, n))], extras,
                    functools.partial(_ep_odd_in, splits=splits), outs, (b, s // tm, 1), tm, n, name)


def _odd_mixer(h, hc, w_in, w_pool, pool_scale, g_q, g_kv, w_uq, w_ukv, with_ctx_out):
    b, s, d = h.shape
    lc = hc.shape[1] // b
    pw = w_pool.shape[0] * w_pool.shape[1]
    q_rank, kv_rank = g_q.shape[0], g_kv.shape[0]
    nh = w_uq.shape[1] // (QK_NOPE_DIM + QK_ROPE_DIM)
    splits = (pw, pw + q_rank, pw + q_rank + kv_rank)
    w_in = w_in.astype(BF16)
    w_in2 = jnp.concatenate([w_in, _swap_rope_cols(w_in[:, splits[2]:])], axis=1)
    wq = w_uq.astype(BF16).reshape(q_rank, nh, QK_NOPE_DIM + QK_ROPE_DIM)
    wq_r = wq[..., QK_NOPE_DIM:]
    wq2 = jnp.concatenate([wq[..., :QK_NOPE_DIM], wq_r, _swap_rope_cols(wq_r)], axis=-1).reshape(q_rank, -1)
    w_ukv = w_ukv.astype(BF16)
    tab, tab_c = _rope_table(s), _identity_rope_table(b * lc)

    u, cq_n, ckv_n, k_rot = _mm_odd_in(h, w_in2, g_q, g_kv, tab, splits, "odd_in")
    uc, cqc_n, ckvc_n, k_rot_c = _mm_odd_in(hc, w_in2, g_q, g_kv, tab_c, splits, "odd_in_ctx")
    k_l, v_l = _mm_kv(ckv_n, w_ukv, k_rot, "odd_up_kv")
    k_c, v_c = _mm_kv(ckvc_n, w_ukv, k_rot_c, "odd_up_kv_ctx")
    q = _mm_qrope(cq_n, wq2, tab, "odd_up_q")
    scale = (QK_NOPE_DIM + QK_ROPE_DIM) ** -0.5
    attn = _dense_attention(q, [(k_l, v_l, 0, 0), (k_c, v_c, 0, 0)], nh, 0, scale, "mla_attention", b)
    pooled = _pool_mix(u, w_pool, pool_scale)
    ctx_ops = None
    if with_ctx_out:
        qc = _mm_qrope(cqc_n, wq2, tab_c, "odd_up_q_ctx")
        attn_c = _dense_attention(qc, [(k_c, v_c, 0, 0)], nh, 0, scale, "ctx_attention_odd", b)
        pooled_c = _pool_mix(uc.reshape(b, lc, pw), w_pool, pool_scale).reshape(1, b * lc, pw)
        ctx_ops = [pooled_c, attn_c]
    return [pooled, attn], ctx_ops


def _ffn(xs, norm_g, shift, scale, gate, wg, wu, wd, layer, tag):
    h2 = _norm_call(xs, norm_g, shift, scale)
    act = _swiglu_up(h2, wg, wu, layer, "ffn_up_" + tag)
    return _mm_residual([act], wd, xs, gate, "ffn_down_" + tag, layer)


def kernel(x, c, ctx, c_ctx, w_mod, b_mod, norm1_g, norm2_g, w_in_ab, w_four, na_rpb, w_out_ab, w_in_cd, w_pool,
           pool_scale, mla_gq, mla_gkv, w_uq, w_ukv, w_out_cd, w_ffn_gate, w_ffn_up, w_ffn_down, final_g):
    b, s, d = x.shape
    depth = w_mod.shape[0]
    nrow = -(-(b + 1) // 8) * 8
    cvec = jnp.concatenate([c, c_ctx[None, :], jnp.zeros((nrow - b - 1, d), F32)], axis=0)
    mods = _mod_call(cvec, w_mod, b_mod)
    w_in_ab, w_out_ab, w_out_cd = w_in_ab.astype(BF16), w_out_ab.astype(BF16), w_out_cd.astype(BF16)
    wg, wu, wd = w_ffn_gate, w_ffn_up, w_ffn_down.astype(BF16)
    xc = ctx.reshape(1, -1, d)
    for l in range(depth):
        last = l == depth - 1
        mod = mods[l, :b].reshape(b, 1, 6 * d)
        modc = mods[l, b].reshape(1, 1, 6 * d)
        sh1, sc1, g1, sh2, sc2, g2 = jnp.split(mod, 6, axis=-1)
        csh1, csc1, cg1, csh2, csc2, cg2 = jnp.split(modc, 6, axis=-1)
        h = _norm_call(x, norm1_g[l], sh1, sc1)
        hc = _norm_call(xc, norm1_g[l], csh1, csc1)
        i = l // 2
        if l % 2 == 0:
            ops, ctx_ops = _even_mixer(h, hc, w_in_ab, i, w_four[i], na_rpb[i], not last)
            w_out = w_out_ab
        else:
            ops, ctx_ops = _odd_mixer(h, hc, w_in_cd[i], w_pool[i], pool_scale[i], mla_gq[i], mla_gkv[i],
                                      w_uq[i], w_ukv[i], not last)
            w_out = w_out_cd
        x = _mm_residual(ops, w_out, x, g1, f"out_proj_{l}", i, tn_target=1024)
        x = _ffn(x, norm2_g[l], sh2, sc2, g2, wg, wu, wd, l, f"{l}")
        if not last:
            xc = _mm_residual(ctx_ops, w_out, xc, cg1, f"out_proj_ctx_{l}", i)
            xc = _ffn(xc, norm2_g[l], csh2, csc2, cg2, wg, wu, wd, l, f"ctx_{l}")
    return _norm_call(x, final_g, out_dtype=F32)
```

```python
import functools
import math

import numpy as np
import jax
import jax.numpy as jnp
from jax import lax
from jax.experimental import pallas as pl
from jax.experimental.pallas import tpu as pltpu

GRID_W = 64
HEAD_DIM = 128
EPS = 1e-6
NEG_INF = -1e30
NA_WIN_H = 8
NA_WIN_W = 16
POOL_WINDOWS = (2, 4, 8, 16)
QK_NOPE_DIM = 128
QK_ROPE_DIM = 64
V_HEAD_DIM = 128
ROPE_BASE = 10000.0
LOG2E = 1.4426950408889634

V7X_LANES = 128
V7X_VMEM_BYTES = 64 * 1024 * 1024
V7X_VMEM_CAP = 56 * 1024 * 1024
NA_ROWS_PER_TILE = 4

BF16 = jnp.bfloat16
F32 = jnp.float32


def _pick_tile(n, target, mult=16):
    if n <= target:
        return n
    for t in range(target, 0, -1):
        if n % t == 0 and t % mult == 0:
            return t
    return n


def _nbytes(shape, dtype):
    return int(np.prod(shape)) * jnp.dtype(dtype).itemsize


def _params(block_bytes, scratch_bytes=0, ndims=3):
    est = 2 * block_bytes + scratch_bytes
    limit = int(min(max(est * 5 // 4 + (4 << 20), 16 << 20), V7X_VMEM_CAP))
    return pltpu.CompilerParams(dimension_semantics=("arbitrary",) * ndims, vmem_limit_bytes=limit)


def _mod_kernel(c_ref, w_ref, b_ref, o_ref):
    cv = c_ref[...]
    s = cv * (1.0 / (1.0 + jnp.exp(-cv)))
    acc = jnp.dot(s.astype(BF16), w_ref[...].astype(BF16), preferred_element_type=F32)
    o_ref[...] = acc + b_ref[...]


def _mod_call(cvec, w_mod, b_mod):
    depth, d, n = w_mod.shape
    r = cvec.shape[0]
    tn = _pick_tile(n, 1024, V7X_LANES)
    blocks = _nbytes((r, d), F32) + _nbytes((d, tn), F32) + 2 * _nbytes((r, tn), F32)
    return pl.pallas_call(
        _mod_kernel,
        grid=(depth, n // tn),
        in_specs=[
            pl.BlockSpec((r, d), lambda l, j: (0, 0)),
            pl.BlockSpec((None, d, tn), lambda l, j: (l, 0, j)),
            pl.BlockSpec((None, 1, tn), lambda l, j: (l, 0, j)),
        ],
        out_specs=pl.BlockSpec((None, r, tn), lambda l, j: (l, 0, j)),
        out_shape=jax.ShapeDtypeStruct((depth, r, n), F32),
        compiler_params=_params(blocks, _nbytes((d, tn), BF16), 2),
        name="mod_vectors",
    )(cvec, w_mod, b_mod.reshape(depth, 1, n))


def _norm_mod_kernel(x_ref, g_ref, sh_ref, sc_ref, o_ref):
    xf = x_ref[...]
    y = xf * lax.rsqrt(jnp.mean(xf * xf, axis=-1, keepdims=True) + EPS) * g_ref[...]
    o_ref[...] = (y * (1.0 + sc_ref[...]) + sh_ref[...]).astype(o_ref.dtype)


def _norm_kernel(x_ref, g_ref, o_ref):
    xf = x_ref[...]
    y = xf * lax.rsqrt(jnp.mean(xf * xf, axis=-1, keepdims=True) + EPS) * g_ref[...]
    o_ref[...] = y.astype(o_ref.dtype)


def _norm_call(x, g, shift=None, scale=None, out_dtype=BF16):
    b, s, d = x.shape
    ts = _pick_tile(s, 1024 if b * s >= 8192 else 256)
    xspec = pl.BlockSpec((None, ts, d), lambda bi, i: (bi, i, 0))
    gspec = pl.BlockSpec((1, d), lambda bi, i: (0, 0))
    mspec = pl.BlockSpec((None, 1, d), lambda bi, i: (bi, 0, 0))
    blocks = _nbytes((ts, d), F32) + _nbytes((ts, d), out_dtype) + 3 * _nbytes((1, d), F32)
    common = dict(
        grid=(b, s // ts),
        out_specs=xspec,
        out_shape=jax.ShapeDtypeStruct((b, s, d), out_dtype),
        compiler_params=_params(blocks, 2 * _nbytes((ts, d), F32), 2),
    )
    if shift is None:
        return pl.pallas_call(_norm_kernel, in_specs=[xspec, gspec], name="rmsnorm", **common)(
            x, g.reshape(1, d))
    return pl.pallas_call(_norm_mod_kernel, in_specs=[xspec, gspec, mspec, mspec], name="rmsnorm_modulate",
                          **common)(x, g.reshape(1, d), shift, scale)


def _mm_kernel(*refs, n_ops, n_extra, epilogue):
    a_refs = refs[:n_ops]
    w_refs = refs[n_ops:2 * n_ops]
    e_refs = refs[2 * n_ops:2 * n_ops + n_extra]
    o_refs = refs[2 * n_ops + n_extra:]
    acc = None
    for a_ref, w_ref in zip(a_refs, w_refs):
        part = jnp.dot(a_ref[...], w_ref[...], preferred_element_type=F32)
        acc = part if acc is None else acc + part
    epilogue(acc, e_refs, o_refs)


def _mm_call(a_ops, w_ops, extras, epilogue, outs, grid, tm, tn, name, tmp_bytes=0):
    arrays = [a for a, _ in a_ops] + [w for w, _ in w_ops] + [e for e, _ in extras]
    in_specs = [s for _, s in a_ops] + [s for _, s in w_ops] + [s for _, s in extras]
    blocks = 0
    for arr, spec in a_ops + w_ops + extras:
        blocks += _nbytes([d for d in spec.block_shape if d is not None], arr.dtype)
    for sds, spec in outs:
        blocks += _nbytes([d for d in spec.block_shape if d is not None], sds.dtype)
    kern = functools.partial(_mm_kernel, n_ops=len(a_ops), n_extra=len(extras), epilogue=epilogue)
    res = pl.pallas_call(
        kern,
        grid=grid,
        in_specs=in_specs,
        out_specs=[s for _, s in outs],
        out_shape=[o for o, _ in outs],
        compiler_params=_params(blocks, 3 * _nbytes((tm, tn), F32) + tmp_bytes, len(grid)),
        name=name,
    )(*arrays)
    return res


def _a_spec(tm, k, col_blk=0):
    return pl.BlockSpec((None, tm, k), lambda b, i, j: (b, i, col_blk))


def _w3(w):
    return w if w.ndim == 3 else w[None]


def _w_spec(k, tn, layer=0, row_blk=0, col_off=0):
    return pl.BlockSpec((None, k, tn), lambda b, i, j: (layer, row_blk, j + col_off))


def _tile_spec(tm, tn):
    return pl.BlockSpec((None, tm, tn), lambda b, i, j: (b, i, j))


def _vec_spec(tn):
    return pl.BlockSpec((None, 1, tn), lambda b, i, j: (b, 0, j))


def _ep_store(acc, e_refs, o_refs):
    o_refs[0][...] = acc.astype(o_refs[0].dtype)


def _col_tile(n, col0, target, mult):
    return _pick_tile(math.gcd(n, col0) if col0 else n, target, mult)


def _mm_plain(a, w, out_dtype, name, layer=0, col0=0, n=None, tm_target=1024, tn_target=512):
    b, s, k = a.shape
    w = _w3(w)
    n = w.shape[2] - col0 if n is None else n
    tm = _pick_tile(s, tm_target)
    tn = _col_tile(n, col0, tn_target, V7X_LANES)
    return _mm_call([(a, _a_spec(tm, k))], [(w, _w_spec(k, tn, layer, 0, col0 // tn))], [], _ep_store,
                    [(jax.ShapeDtypeStruct((b, s, n), out_dtype), _tile_spec(tm, tn))],
                    (b, s // tm, n // tn), tm, tn, name)[0]


def _ep_chunks(acc, e_refs, o_refs, *, width):
    o_ref = o_refs[0]
    for c in range(acc.shape[1] // width):
        o_ref[c] = acc[:, c * width:(c + 1) * width].astype(o_ref.dtype)


def _mm_chunked(a, w, name, layer=0, col0=0, n=None, width=HEAD_DIM, tm_target=1024, tn_target=512):
    b, s, k = a.shape
    w = _w3(w)
    n = w.shape[2] - col0 if n is None else n
    tm = _pick_tile(s, tm_target)
    tn = _col_tile(n, col0, tn_target, width)
    cpt = tn // width
    ospec = pl.BlockSpec((None, cpt, tm, width), lambda bi, i, j: (bi, j, i, 0))
    return _mm_call([(a, _a_spec(tm, k))], [(w, _w_spec(k, tn, layer, 0, col0 // tn))], [],
                    functools.partial(_ep_chunks, width=width),
                    [(jax.ShapeDtypeStruct((b, n // width, s, width), BF16), ospec)],
                    (b, s // tm, n // tn), tm, tn, name)[0]


def _ep_residual(acc, e_refs, o_refs):
    x_ref, g_ref = e_refs
    o_refs[0][...] = x_ref[...] + g_ref[...] * acc


def _mm_residual(a_list, w, x, gate, name, layer=0, tm_target=1024, tn_target=512):
    b, s, n = x.shape
    w = _w3(w)
    tm = _pick_tile(s, tm_target)
    tn = _pick_tile(n, tn_target, V7X_LANES)
    kblk = functools.reduce(math.gcd, [a.shape[2] for a in a_list])
    a_ops, w_ops = [], []
    for a in a_list:
        for cb in range(a.shape[2] // kblk):
            a_ops.append((a, _a_spec(tm, kblk, cb)))
            w_ops.append((w, _w_spec(kblk, tn, layer, len(w_ops))))
    extras = [(x, _tile_spec(tm, tn)), (gate, _vec_spec(tn))]
    return _mm_call(a_ops, w_ops, extras, _ep_residual,
                    [(jax.ShapeDtypeStruct((b, s, n), F32), _tile_spec(tm, tn))],
                    (b, s // tm, n // tn), tm, tn, name)[0]


def _swiglu_kernel(a_ref, wg_ref, wu_ref, o_ref, wgb_ref, wub_ref):
    @pl.when((pl.program_id(1) == 0) & (pl.program_id(2) == 0))
    def _():
        wgb_ref[...] = wg_ref[...].astype(BF16)
        wub_ref[...] = wu_ref[...].astype(BF16)

    a = a_ref[...]
    g = jnp.dot(a, wgb_ref[...], preferred_element_type=F32)
    u = jnp.dot(a, wub_ref[...], preferred_element_type=F32)
    o_ref[...] = (g * (1.0 / (1.0 + jnp.exp(-g))) * u).astype(o_ref.dtype)


def _swiglu_up(a, wg, wu, layer, name, tm_target=1024, tn_target=512):
    b, s, k = a.shape
    n = wg.shape[2]
    tm = _pick_tile(s, tm_target)
    tn = _pick_tile(n, tn_target, V7X_LANES)
    blocks = _nbytes((tm, k), BF16) + 2 * _nbytes((k, tn), F32) + _nbytes((tm, tn), BF16)
    wspec = pl.BlockSpec((None, k, tn), lambda j, bi, i: (layer, 0, j))
    return pl.pallas_call(
        _swiglu_kernel,
        grid=(n // tn, b, s // tm),
        in_specs=[pl.BlockSpec((None, tm, k), lambda j, bi, i: (bi, i, 0)), wspec, wspec],
        out_specs=pl.BlockSpec((None, tm, tn), lambda j, bi, i: (bi, i, j)),
        out_shape=jax.ShapeDtypeStruct((b, s, n), BF16),
        scratch_shapes=[pltpu.VMEM((k, tn), BF16), pltpu.VMEM((k, tn), BF16)],
        compiler_params=_params(blocks, 2 * _nbytes((k, tn), BF16) + 4 * _nbytes((tm, tn), F32), 3),
        name=name,
    )(a, wg, wu)


def _rope_pair(hi, tab):
    z = hi * tab
    return z + pltpu.roll(z, QK_ROPE_DIM, axis=1)


def _ep_qrope(acc, e_refs, o_refs, *, heads):
    tab_ref, = e_refs
    o_ref = o_refs[0]
    tab = tab_ref[...]
    blk = QK_NOPE_DIM + 2 * QK_ROPE_DIM
    for h in range(heads):
        o_ref[h, :, 0:QK_NOPE_DIM] = acc[:, h * blk:h * blk + QK_NOPE_DIM].astype(o_ref.dtype)
        hi = acc[:, h * blk + QK_NOPE_DIM:(h + 1) * blk]
        o_ref[h, :, QK_NOPE_DIM:blk] = _rope_pair(hi, tab).astype(o_ref.dtype)


def _mm_qrope(a, w, tab, name, tm_target=1024, heads_per_tile=6):
    b, s, k = a.shape
    blk = QK_NOPE_DIM + 2 * QK_ROPE_DIM
    nh = w.shape[1] // blk
    hpt = math.gcd(nh, heads_per_tile)
    tn = hpt * blk
    tm = _pick_tile(s, tm_target)
    tspec = pl.BlockSpec((tm, 2 * QK_ROPE_DIM), lambda bi, i, j: (i, 0))
    ospec = pl.BlockSpec((None, hpt, tm, blk), lambda bi, i, j: (bi, j, i, 0))
    return _mm_call([(a, _a_spec(tm, k))], [(_w3(w), _w_spec(k, tn))], [(tab, tspec)],
                    functools.partial(_ep_qrope, heads=hpt),
                    [(jax.ShapeDtypeStruct((b, nh, s, blk), BF16), ospec)],
                    (b, s // tm, nh // hpt), tm, tn, name)[0]


def _ep_kv(acc, e_refs, o_refs, *, heads):
    kr_ref, = e_refs
    k_ref, v_ref = o_refs
    kr = kr_ref[...]
    blk = QK_NOPE_DIM + V_HEAD_DIM
    for h in range(heads):
        k_ref[h, :, 0:QK_NOPE_DIM] = acc[:, h * blk:h * blk + QK_NOPE_DIM].astype(k_ref.dtype)
        k_ref[h, :, QK_NOPE_DIM:QK_NOPE_DIM + kr.shape[1]] = kr
        v_ref[h] = acc[:, h * blk + QK_NOPE_DIM:(h + 1) * blk].astype(v_ref.dtype)


def _mm_kv(a, w, krot, name, tm_target=1024, heads_per_tile=6):
    b, s, k = a.shape
    blk = QK_NOPE_DIM + V_HEAD_DIM
    nh = w.shape[1] // blk
    hpt = math.gcd(nh, heads_per_tile)
    tn = hpt * blk
    tm = _pick_tile(s, tm_target)
    kd = QK_NOPE_DIM + krot.shape[2]
    krspec = pl.BlockSpec((None, tm, krot.shape[2]), lambda bi, i, j: (bi, i, 0))
    kspec = pl.BlockSpec((None, hpt, tm, kd), lambda bi, i, j: (bi, j, i, 0))
    vspec = pl.BlockSpec((None, hpt, tm, V_HEAD_DIM), lambda bi, i, j: (bi, j, i, 0))
    return _mm_call([(a, _a_spec(tm, k))], [(_w3(w), _w_spec(k, tn))], [(krot, krspec)],
                    functools.partial(_ep_kv, heads=hpt),
                    [(jax.ShapeDtypeStruct((b, nh, s, kd), BF16), kspec),
                     (jax.ShapeDtypeStruct((b, nh, s, V_HEAD_DIM), BF16), vspec)],
                    (b, s // tm, nh // hpt), tm, tn, name)


def _dft_matrix(n):
    m = jnp.arange(n, dtype=jnp.int32)
    side = 1
    while side * side < n:
        side *= 2
    if side * side != n:
        k = jnp.arange(n, dtype=jnp.int32)
        ang = (2.0 * math.pi / n) * ((k[:, None] * m[None, :]) % n).astype(F32)
        return jnp.concatenate([jnp.cos(ang), jnp.sin(ang)], axis=0).astype(BF16)[None]
    a = jnp.arange(side, dtype=jnp.int32)
    ang_x = (2.0 * math.pi / side) * ((a[:, None] * m[None, :]) % side).astype(F32)
    ang_y = (2.0 * math.pi / n) * ((a[:, None] * m[None, :]) % n).astype(F32)
    cx, sx = jnp.cos(ang_x), jnp.sin(ang_x)
    cy, sy = jnp.cos(ang_y)[None], jnp.sin(ang_y)[None]
    p = jnp.concatenate([cx, sx], axis=0)[:, None, :]
    q = jnp.concatenate([sx, -cx], axis=0)[:, None, :]
    return (p * cy - q * sy).astype(BF16).reshape(1, 2 * n, n)


def _dft_positions_dense(u):
    b, s, w = u.shape
    mat = _dft_matrix(s)
    tm = _pick_tile(2 * s, 1024)
    tn = _pick_tile(w, 512, V7X_LANES)
    aspec = pl.BlockSpec((None, tm, s), lambda bi, i, j: (0, i, 0))
    wspec = pl.BlockSpec((None, s, tn), lambda bi, i, j: (bi, 0, j))
    return _mm_call([(mat, aspec)], [(u, wspec)], [], _ep_store,
                    [(jax.ShapeDtypeStruct((b, 2 * s, w), BF16), _tile_spec(tm, tn))],
                    (b, 2 * s // tm, w // tn), tm, tn, "fourier_positions")[0]


DFT_ROWS_PER_STEP = 16


def _dft_stage1_kernel(x_ref, a_ref, tc_ref, ts_ref, o_ref, *, side, width, n2_per_step):
    g = jnp.dot(a_ref[...], x_ref[...], preferred_element_type=F32)
    gr, gs = g[0:side], g[side:2 * side]
    for n2 in range(n2_per_step):
        tcs = tc_ref[:, n2 * V7X_LANES:(n2 + 1) * V7X_LANES]
        tss = ts_ref[:, n2 * V7X_LANES:(n2 + 1) * V7X_LANES]
        for c0 in range(n2 * width, (n2 + 1) * width, V7X_LANES):
            cols = slice(c0, c0 + V7X_LANES)
            o_ref[0:side, cols] = (gr[:, cols] * tcs - gs[:, cols] * tss).astype(o_ref.dtype)
            o_ref[side:2 * side, cols] = (gr[:, cols] * tss + gs[:, cols] * tcs).astype(o_ref.dtype)


def _dft_stage2_kernel(h_ref, m_ref, o_ref, scr_ref, *, side):
    rows = h_ref.shape[1]
    ngrp = scr_ref.shape[0]
    for j in range(rows):
        hcat = jnp.concatenate([h_ref[0, j], h_ref[1, j]], axis=0)
        res = jnp.dot(m_ref[...], hcat, preferred_element_type=F32)
        for c in range(ngrp):
            cols = slice(c * V7X_LANES, (c + 1) * V7X_LANES)
            scr_ref[c, pl.ds(j, side, stride=rows), :] = res[0:side, cols]
            scr_ref[c, pl.ds(side * rows + j, side, stride=rows), :] = res[side:2 * side, cols]
    for c in range(ngrp):
        cols = slice(c * V7X_LANES, (c + 1) * V7X_LANES)
        for p in range(2):
            part = scr_ref[c, p * side * rows:(p + 1) * side * rows, :].reshape(side, rows, V7X_LANES)
            o_ref[p, :, :, cols] = part.astype(o_ref.dtype)


def _dft_positions(u):
    b, s, w = u.shape
    side = math.isqrt(s)
    if side * side != s or side % DFT_ROWS_PER_STEP or w % V7X_LANES:
        return _dft_positions_dense(u)
    idx = np.arange(side)
    ang = 2.0 * np.pi * ((idx[:, None] * idx[None, :]) % side) / side
    cs, sn = np.cos(ang), np.sin(ang)
    a1 = jnp.asarray(np.concatenate([cs, sn], axis=0), dtype=F32).astype(BF16)
    m2 = jnp.asarray(np.block([[cs, -sn], [sn, cs]]), dtype=F32).astype(BF16)
    tw = 2.0 * np.pi * (idx[:, None] * idx[None, :]) / s
    tcos = jnp.repeat(jnp.asarray(np.cos(tw), dtype=F32), V7X_LANES, axis=1)
    tsin = jnp.repeat(jnp.asarray(np.sin(tw), dtype=F32), V7X_LANES, axis=1)
    n2s = DFT_ROWS_PER_STEP
    tc = n2s * w
    blocks1 = _nbytes((side, tc), BF16) + _nbytes((2 * side, side), BF16) + 2 * _nbytes((side, n2s * V7X_LANES), F32) \
        + _nbytes((2 * side, tc), BF16)
    h = pl.pallas_call(
        functools.partial(_dft_stage1_kernel, side=side, width=w, n2_per_step=n2s),
        grid=(b, side // n2s),
        in_specs=[
            pl.BlockSpec((None, side, tc), lambda bi, j: (bi, 0, j)),
            pl.BlockSpec((2 * side, side), lambda bi, j: (0, 0)),
            pl.BlockSpec((side, n2s * V7X_LANES), lambda bi, j: (0, j)),
            pl.BlockSpec((side, n2s * V7X_LANES), lambda bi, j: (0, j)),
        ],
        out_specs=pl.BlockSpec((None, 2 * side, tc), lambda bi, j: (bi, 0, j)),
        out_shape=jax.ShapeDtypeStruct((b, 2 * side, side * w), BF16),
        compiler_params=_params(blocks1, 3 * _nbytes((2 * side, tc), F32), 2),
        name="fourier_positions_stage1",
    )(u.reshape(b, side, side * w), a1, tcos, tsin)
    rows = DFT_ROWS_PER_STEP
    blocks2 = 2 * _nbytes((2, rows, side, w), BF16) + _nbytes((2 * side, 2 * side), BF16)
    f2 = pl.pallas_call(
        functools.partial(_dft_stage2_kernel, side=side),
        grid=(b, side // rows),
        in_specs=[
            pl.BlockSpec((None, 2, rows, side, w), lambda bi, i: (bi, 0, i, 0, 0)),
            pl.BlockSpec((2 * side, 2 * side), lambda bi, i: (0, 0)),
        ],
        out_specs=pl.BlockSpec((None, 2, side, rows, w), lambda bi, i: (bi, 0, 0, i, 0)),
        out_shape=jax.ShapeDtypeStruct((b, 2, side, side, w), BF16),
        scratch_shapes=[pltpu.VMEM((w // V7X_LANES, 2 * side * rows, V7X_LANES), F32)],
        compiler_params=_params(blocks2, _nbytes((2 * side * rows, w), F32) + 4 * _nbytes((2 * side, w), F32), 2),
        name="fourier_positions_stage2",
    )(h.reshape(b, 2, side, side, w), m2)
    return f2.reshape(b, 2 * s, w)


def _four_channel_kernel(fr_ref, fi_ref, cs_ref, w_ref, o_ref, *, norm):
    gd = w_ref.shape[1]
    for g in range(w_ref.shape[0]):
        cols = slice(g * gd, (g + 1) * gd)
        f = jnp.dot(fr_ref[:, cols].astype(BF16), cs_ref[0:gd, :], preferred_element_type=F32)
        f = f + jnp.dot(fi_ref[:, cols].astype(BF16), cs_ref[gd:2 * gd, :], preferred_element_type=F32)
        f = (f * norm).astype(BF16)
        o_ref[:, cols] = jnp.dot(f, w_ref[g], preferred_element_type=F32).astype(o_ref.dtype)


def _four_channels(f2, w_four):
    b, s2, w = f2.shape
    s = s2 // 2
    g, gd, _ = w_four.shape
    idx = np.arange(gd)
    ang = 2.0 * np.pi * ((idx[:, None] * idx[None, :]) % gd) / gd
    cs = jnp.asarray(np.concatenate([np.cos(ang), -np.sin(ang)], axis=0), dtype=F32).astype(BF16)
    ts = _pick_tile(s, 1024)
    nblk = s // ts
    blocks = 2 * _nbytes((ts, w), f2.dtype) + _nbytes((ts, w), BF16) + _nbytes((2 * gd, gd), BF16) \
        + _nbytes((g, gd, gd), BF16)
    return pl.pallas_call(
        functools.partial(_four_channel_kernel, norm=1.0 / math.sqrt(s * gd)),
        grid=(b, nblk),
        in_specs=[
            pl.BlockSpec((None, ts, w), lambda bi, i: (bi, i, 0)),
            pl.BlockSpec((None, ts, w), lambda bi, i: (bi, nblk + i, 0)),
            pl.BlockSpec((2 * gd, gd), lambda bi, i: (0, 0)),
            pl.BlockSpec((g, gd, gd), lambda bi, i: (0, 0, 0)),
        ],
        out_specs=pl.BlockSpec((None, ts, w), lambda bi, i: (bi, i, 0)),
        out_shape=jax.ShapeDtypeStruct((b, s, w), BF16),
        compiler_params=_params(blocks, 6 * _nbytes((ts, gd), F32), 2),
        name="fourier_channels",
    )(f2, f2, cs, w_four.astype(BF16))


def _na_plan(rows):
    rpt = NA_ROWS_PER_TILE
    kh = min(NA_WIN_H, rows)
    wrows = min(kh + rpt - 1, rows)
    cases, ws_list, case_list = {}, [], []
    for t in range(rows // rpt):
        r0 = t * rpt
        ws = int(np.clip(r0 - kh // 2, 0, rows - wrows))
        plan = []
        for r in range(r0, r0 + rpt):
            rs = int(np.clip(r - kh // 2, 0, rows - kh))
            plan.append(tuple((ws + w) - r + NA_WIN_H - 1 if rs <= ws + w < rs + kh else None
                              for w in range(wrows)))
        plan = tuple(plan)
        cases.setdefault(plan, len(cases))
        ws_list.append(ws)
        case_list.append(cases[plan])
    return wrows, ws_list, case_list, list(cases)


def _na_block_table(rpb):
    qcol = np.arange(GRID_W)[:, None]
    kcol = np.arange(GRID_W)[None, :]
    cstart = np.clip(qcol - NA_WIN_W // 2, 0, GRID_W - NA_WIN_W)
    col_ok = (kcol >= cstart) & (kcol < cstart + NA_WIN_W)
    cidx = np.clip(kcol - qcol + NA_WIN_W - 1, 0, 2 * NA_WIN_W - 2)
    csel = jnp.asarray(np.eye(rpb.shape[2], dtype=np.float32)[cidx])
    blk = jnp.einsum("hrs,qks->hrqk", rpb, csel, precision=lax.Precision.HIGHEST)
    blk = jnp.where(jnp.asarray(col_ok), blk * LOG2E, NEG_INF)
    return jnp.concatenate([blk, blk], axis=-1)


def _ones_column(rows, width):
    lane = lax.broadcasted_iota(jnp.int32, (rows, width), 1)
    return jnp.where(lane == 0, 1.0, 0.0).astype(BF16)


def _na_kernel(q_ref, k_ref, v_ref, kc_ref, vc_ref, blk_ref, o_ref, bias_ref, vx_ref, vcx_ref,
               *, tiles, tq, tk, scale, plans):
    d = v_ref.shape[1]

    @pl.when(pl.program_id(1) == 0)
    def _():
        for c, plan in enumerate(plans):
            for qr, row in enumerate(plan):
                for kr, ridx in enumerate(row):
                    half = slice((kr % 2) * GRID_W, (kr % 2 + 1) * GRID_W)
                    if ridx is None:
                        blk = jnp.full((GRID_W, GRID_W), NEG_INF, F32)
                    else:
                        blk = blk_ref[ridx][:, half]
                    bias_ref[c, qr * GRID_W:(qr + 1) * GRID_W, kr * GRID_W:(kr + 1) * GRID_W] = blk

    vx_ref[:, 0:d] = v_ref[...]
    vx_ref[:, d:2 * d] = _ones_column(v_ref.shape[0], d)
    vcx_ref[:, 0:d] = vc_ref[...]
    vcx_ref[:, d:2 * d] = _ones_column(vc_ref.shape[0], d)
    kc = kc_ref[...]
    dn = (((1,), (1,)), ((), ()))
    c2 = scale * LOG2E

    for t, (ws, case) in enumerate(tiles):
        q0, k0 = t * tq, ws * GRID_W
        q = q_ref[q0:q0 + tq, :]
        s_loc = lax.dot_general(q, k_ref[k0:k0 + tk, :], dn, preferred_element_type=F32) * c2 + bias_ref[case]
        s_ctx = lax.dot_general(q, kc, dn, preferred_element_type=F32) * c2
        m = jnp.maximum(jnp.max(s_loc, axis=-1, keepdims=True), jnp.max(s_ctx, axis=-1, keepdims=True))
        p_loc = jnp.exp2(s_loc - m).astype(BF16)
        p_ctx = jnp.exp2(s_ctx - m).astype(BF16)
        acc = jnp.dot(p_loc, vx_ref[k0:k0 + tk, :], preferred_element_type=F32)
        acc = acc + jnp.dot(p_ctx, vcx_ref[...], preferred_element_type=F32)
        o_ref[q0:q0 + tq, :] = (acc[:, 0:d] / acc[:, d:d + 1]).astype(o_ref.dtype)


def _na_attention(qkv, qkvc, rpb):
    b, c3, s, d = qkv.shape
    nh = c3 // 3
    lc = qkvc.shape[2] // b
    rows = s // GRID_W
    wrows, ws_list, case_list, plans = _na_plan(rows)
    blk = _na_block_table(rpb)
    nrel = blk.shape[1]
    tq = NA_ROWS_PER_TILE * GRID_W
    tk = wrows * GRID_W
    blocks = 3 * _nbytes((s, d), BF16) + 2 * _nbytes((lc, d), BF16) + _nbytes((nrel, GRID_W, 2 * GRID_W), F32) \
        + _nbytes((s, d), BF16)
    scratch = _nbytes((len(plans), tq, tk), F32) + _nbytes((s + lc, 2 * d), BF16)
    return pl.pallas_call(
        functools.partial(_na_kernel, tiles=list(zip(ws_list, case_list)), tq=tq, tk=tk, scale=d ** -0.5,
                          plans=plans),
        grid=(nh, b),
        in_specs=[
            pl.BlockSpec((None, None, s, d), lambda h, bi: (bi, h, 0, 0)),
            pl.BlockSpec((None, None, s, d), lambda h, bi: (bi, nh + h, 0, 0)),
            pl.BlockSpec((None, None, s, d), lambda h, bi: (bi, 2 * nh + h, 0, 0)),
            pl.BlockSpec((None, None, lc, d), lambda h, bi: (0, nh + h, bi, 0)),
            pl.BlockSpec((None, None, lc, d), lambda h, bi: (0, 2 * nh + h, bi, 0)),
            pl.BlockSpec((None, nrel, GRID_W, 2 * GRID_W), lambda h, bi: (h, 0, 0, 0)),
        ],
        out_specs=pl.BlockSpec((None, s, d), lambda h, bi: (bi, 0, h)),
        out_shape=jax.ShapeDtypeStruct((b, s, nh * d), BF16),
        scratch_shapes=[pltpu.VMEM((len(plans), tq, tk), F32), pltpu.VMEM((s, 2 * d), BF16),
                        pltpu.VMEM((lc, 2 * d), BF16)],
        compiler_params=_params(blocks, scratch + 8 * _nbytes((tq, tk + lc), F32), 2),
        name="neighbourhood_attention",
    )(qkv, qkv, qkv, qkvc, qkvc, blk)


def _dense_attn_kernel(*refs, scale, n_src, chunks, heads):
    q_ref = refs[0]
    k_refs = refs[1:1 + n_src]
    v_refs = refs[1 + n_src:1 + 2 * n_src]
    o_ref = refs[1 + 2 * n_src]
    vx_refs = refs[2 + 2 * n_src:]
    dv = v_refs[0].shape[2]

    @pl.when(pl.program_id(2) == 0)
    def _():
        for v_ref, vx_ref in zip(v_refs, vx_refs):
            for hh in range(heads):
                vx_ref[hh, :, 0:dv] = v_ref[hh]
                vx_ref[hh, :, dv:2 * dv] = _ones_column(v_ref.shape[1], dv)

    c2 = scale * LOG2E
    for hh in range(heads):
        q = q_ref[hh]
        m = acc = None
        for src, start, size in chunks:
            s = lax.dot_general(q, k_refs[src][hh, start:start + size, :], (((1,), (1,)), ((), ())),
                                preferred_element_type=F32)
            m_chunk = jnp.max(s, axis=-1, keepdims=True)
            m_new = m_chunk if m is None else jnp.maximum(m, m_chunk)
            p = jnp.exp2((s - m_new) * c2)
            pv = jnp.dot(p.astype(BF16), vx_refs[src][hh, start:start + size, :], preferred_element_type=F32)
            acc = pv if acc is None else acc * jnp.exp2((m - m_new) * c2) + pv
            m = m_new
        o_ref[:, hh * dv:(hh + 1) * dv] = (acc[:, 0:dv] / acc[:, dv:dv + 1]).astype(o_ref.dtype)


def _head_major_spec(arr, batch, rows, off, tiled, heads):
    per_batch = arr.shape[2] * arr.shape[0] // batch // rows
    hb = off // heads
    if arr.shape[0] == 1:
        return pl.BlockSpec((None, heads, rows, arr.shape[3]),
                            lambda bi, h, i: (0, hb + h, bi * per_batch + (i if tiled else 0), 0))
    return pl.BlockSpec((None, heads, rows, arr.shape[3]), lambda bi, h, i: (bi, hb + h, i if tiled else 0, 0))


def _dense_attention(q, kv_list, nh, q_off, scale, name, batch, tq_target=1024, chunk_target=512, heads=2):
    dk = q.shape[3]
    sq = q.shape[0] * q.shape[2] // batch
    dv = kv_list[0][1].shape[3]
    offsets = [q_off] + [o for _, _, ko, vo in kv_list for o in (ko, vo)]
    heads = functools.reduce(math.gcd, [nh] + offsets, heads)
    tq = _pick_tile(sq, tq_target)
    nq = sq // tq
    chunks, arrays_k, arrays_v, scratch = [], [], [], []
    blocks = heads * (_nbytes((tq, dk), BF16) + _nbytes((tq, dv), BF16))
    tmp = 4 * _nbytes((tq, chunk_target), F32) + 2 * _nbytes((tq, 2 * dv), F32)
    kspecs, vspecs = [], []
    for src, (k, v, k_off, v_off) in enumerate(kv_list):
        sk = k.shape[0] * k.shape[2] // batch
        chunks += [(src, st, min(chunk_target, sk - st)) for st in range(0, sk, chunk_target)]
        kspecs.append(_head_major_spec(k, batch, sk, k_off, False, heads))
        vspecs.append(_head_major_spec(v, batch, sk, v_off, False, heads))
        arrays_k.append(k)
        arrays_v.append(v)
        scratch.append(pltpu.VMEM((heads, sk, 2 * dv), BF16))
        blocks += heads * (_nbytes((sk, dk), BF16) + _nbytes((sk, dv), BF16))
        tmp += heads * _nbytes((sk, 2 * dv), BF16)
    if q.shape[0] == 1:
        out_spec = pl.BlockSpec((None, tq, heads * dv), lambda bi, h, i: (0, bi * nq + i, h))
    else:
        out_spec = pl.BlockSpec((None, tq, heads * dv), lambda bi, h, i: (bi, i, h))
    return pl.pallas_call(
        functools.partial(_dense_attn_kernel, scale=scale, n_src=len(kv_list), chunks=chunks, heads=heads),
        grid=(batch, nh // heads, nq),
        in_specs=[_head_major_spec(q, batch, tq, q_off, True, heads)] + kspecs + vspecs,
        out_specs=out_spec,
        out_shape=jax.ShapeDtypeStruct((q.shape[0], q.shape[2], nh * dv), BF16),
        scratch_shapes=scratch,
        compiler_params=_params(blocks, tmp, 3),
        name=name,
    )(q, *arrays_k, *arrays_v)


def _pool_kernel(u_ref, w_ref, sc_ref, o_ref, *, windows):
    n, gd = u_ref.shape[0], w_ref.shape[1]
    t = lax.broadcasted_iota(jnp.int32, (n, gd), 0)
    for g, window in enumerate(windows):
        cols = slice(g * gd, (g + 1) * gd)
        u = u_ref[:, cols]
        half = window // 2
        ssum = u
        for j in range(1, half + 1):
            ssum = ssum + jnp.where(t >= j, pltpu.roll(u, j, axis=0), 0.0)
        for j in range(1, half):
            ssum = ssum + jnp.where(t < n - j, pltpu.roll(u, n - j, axis=0), 0.0)
        cnt = jnp.minimum(t + half, n) - jnp.maximum(t - half, 0)
        p = ssum / cnt.astype(F32) - u
        y = jnp.dot(p.astype(BF16), w_ref[g], preferred_element_type=F32)
        o_ref[:, cols] = (y * sc_ref[:, cols]).astype(o_ref.dtype)


def _pool_mix(u, w_pool, pool_scale):
    b, s, w = u.shape
    g, gd, _ = w_pool.shape
    blocks = _nbytes((s, w), F32) + _nbytes((g, gd, gd), BF16) + _nbytes((s, w), BF16)
    return pl.pallas_call(
        functools.partial(_pool_kernel, windows=POOL_WINDOWS[:g]),
        grid=(b,),
        in_specs=[
            pl.BlockSpec((None, s, w), lambda bi: (bi, 0, 0)),
            pl.BlockSpec((g, gd, gd), lambda bi: (0, 0, 0)),
            pl.BlockSpec((1, w), lambda bi: (0, 0)),
        ],
        out_specs=pl.BlockSpec((None, s, w), lambda bi: (bi, 0, 0)),
        out_shape=jax.ShapeDtypeStruct((b, s, w), BF16),
        compiler_params=_params(blocks, 6 * _nbytes((s, gd), F32), 1),
        name="pool_mix",
    )(u, w_pool.astype(BF16), pool_scale.reshape(1, w))


def _rope_table(n):
    half = QK_ROPE_DIM // 2
    nf = half // 2
    t = np.arange(n)
    inv = ROPE_BASE ** (-np.arange(nf, dtype=np.float64) / nf)
    ar = (t // GRID_W)[:, None] * inv[None, :]
    ac = (t % GRID_W)[:, None] * inv[None, :]
    cr, sr, cc, sc = np.cos(ar), np.sin(ar), np.cos(ac), np.sin(ac)
    return jnp.asarray(np.concatenate([cr, cr, cc, cc, -sr, sr, -sc, sc], axis=1), dtype=F32)


def _identity_rope_table(n):
    return jnp.concatenate([jnp.ones((n, QK_ROPE_DIM), F32), jnp.zeros((n, QK_ROPE_DIM), F32)], axis=1)


def _swap_rope_cols(w):
    nf = QK_ROPE_DIM // 4
    lead = w.shape[:-1]
    return jnp.flip(w.reshape(lead + (2, 2, nf)), axis=-2).reshape(lead + (QK_ROPE_DIM,))


def _even_mixer(h, hc, w_in, layer, w_four, rpb, with_ctx_out):
    fw = w_four.shape[0] * w_four.shape[1]
    nh = rpb.shape[0]
    u = _mm_plain(h, w_in, BF16, "even_in_fourier", layer, 0, fw, tm_target=2048)
    qkv = _mm_chunked(h, w_in, "even_in_qkv", layer, fw, tm_target=2048)
    qkvc = _mm_chunked(hc, w_in, "even_in_qkv_ctx", layer, fw)
    y_f = _four_channels(_dft_positions(u), w_four)
    y_a = _na_attention(qkv, qkvc, rpb)
    ctx_ops = None
    if with_ctx_out:
        b = h.shape[0]
        uc = _mm_plain(hc, w_in, BF16, "even_in_fourier_ctx", layer, 0, fw).reshape(b, -1, fw)
        yc_f = _four_channels(_dft_positions(uc), w_four).reshape(1, -1, fw)
        yc_a = _dense_attention(qkvc, [(qkvc, qkvc, nh, 2 * nh)], nh, 0, HEAD_DIM ** -0.5, "ctx_attention_even", b)
        ctx_ops = [yc_f, yc_a]
    return [y_f, y_a], ctx_ops


def _ep_odd_in(acc, e_refs, o_refs, *, splits):
    gq_ref, gkv_ref, tab_ref = e_refs
    u_ref, cq_ref, ckv_ref, kr_ref = o_refs
    o_q, o_kv, o_kr = splits

    def rms(v, g_ref):
        return (v * lax.rsqrt(jnp.mean(v * v, axis=-1, keepdims=True) + EPS) * g_ref[...]).astype(BF16)

    u_ref[...] = acc[:, 0:o_q]
    cq_ref[...] = rms(acc[:, o_q:o_kv], gq_ref)
    ckv_ref[...] = rms(acc[:, o_kv:o_kr], gkv_ref)
    r = _rope_pair(acc[:, o_kr:], tab_ref[...])
    lane = lax.broadcasted_iota(jnp.int32, r.shape, 1)
    kr_ref[...] = jnp.where(lane < QK_ROPE_DIM, r, 0.0).astype(kr_ref.dtype)


def _mm_odd_in(a, w, g_q, g_kv, tab, splits, name, tm_target=1024):
    b, s, k = a.shape
    n = w.shape[1]
    o_q, o_kv, o_kr = splits
    tm = _pick_tile(s, tm_target)

    def row(width):
        return pl.BlockSpec((None, tm, width), lambda bi, i, j: (bi, i, 0))

    def vec(width):
        return pl.BlockSpec((1, width), lambda bi, i, j: (0, 0))

    extras = [(g_q.reshape(1, -1), vec(o_kv - o_q)), (g_kv.reshape(1, -1), vec(o_kr - o_kv)),
              (tab, pl.BlockSpec((tm, n - o_kr), lambda bi, i, j: (i, 0)))]
    outs = [(jax.ShapeDtypeStruct((b, s, o_q), F32), row(o_q)),
            (jax.ShapeDtypeStruct((b, s, o_kv - o_q), BF16), row(o_kv - o_q)),
            (jax.ShapeDtypeStruct((b, s, o_kr - o_kv), BF16), row(o_kr - o_kv)),
            (jax.ShapeDtypeStruct((b, s, n - o_kr), BF16), row(n - o_kr))]
    return _mm_call([(a, _a_spec(tm, k))], [(_w3(w), _w_spec(k, n))], extras,
                    functools.partial(_ep_odd_in, splits=splits), outs, (b, s // tm, 1), tm, n, name)


def _odd_mixer(h, hc, w_in, w_pool, pool_scale, g_q, g_kv, w_uq, w_ukv, with_ctx_out):
    b, s, d = h.shape
    lc = hc.shape[1] // b
    pw = w_pool.shape[0] * w_pool.shape[1]
    q_rank, kv_rank = g_q.shape[0], g_kv.shape[0]
    nh = w_uq.shape[1] // (QK_NOPE_DIM + QK_ROPE_DIM)
    splits = (pw, pw + q_rank, pw + q_rank + kv_rank)
    w_in = w_in.astype(BF16)
    w_in2 = jnp.concatenate([w_in, _swap_rope_cols(w_in[:, splits[2]:])], axis=1)
    wq = w_uq.astype(BF16).reshape(q_rank, nh, QK_NOPE_DIM + QK_ROPE_DIM)
    wq_r = wq[..., QK_NOPE_DIM:]
    wq2 = jnp.concatenate([wq[..., :QK_NOPE_DIM], wq_r, _swap_rope_cols(wq_r)], axis=-1).reshape(q_rank, -1)
    w_ukv = w_ukv.astype(BF16)
    tab, tab_c = _rope_table(s), _identity_rope_table(b * lc)

    u, cq_n, ckv_n, k_rot = _mm_odd_in(h, w_in2, g_q, g_kv, tab, splits, "odd_in")
    uc, cqc_n, ckvc_n, k_rot_c = _mm_odd_in(hc, w_in2, g_q, g_kv, tab_c, splits, "odd_in_ctx")
    k_l, v_l = _mm_kv(ckv_n, w_ukv, k_rot, "odd_up_kv")
    k_c, v_c = _mm_kv(ckvc_n, w_ukv, k_rot_c, "odd_up_kv_ctx")
    q = _mm_qrope(cq_n, wq2, tab, "odd_up_q")
    scale = (QK_NOPE_DIM + QK_ROPE_DIM) ** -0.5
    attn = _dense_attention(q, [(k_l, v_l, 0, 0), (k_c, v_c, 0, 0)], nh, 0, scale, "mla_attention", b)
    pooled = _pool_mix(u, w_pool, pool_scale)
    ctx_ops = None
    if with_ctx_out:
        qc = _mm_qrope(cqc_n, wq2, tab_c, "odd_up_q_ctx")
        attn_c = _dense_attention(qc, [(k_c, v_c, 0, 0)], nh, 0, scale, "ctx_attention_odd", b)
        pooled_c = _pool_mix(uc.reshape(b, lc, pw), w_pool, pool_scale).reshape(1, b * lc, pw)
        ctx_ops = [pooled_c, attn_c]
    return [pooled, attn], ctx_ops


def _ffn(xs, norm_g, shift, scale, gate, wg, wu, wd, layer, tag):
    h2 = _norm_call(xs, norm_g, shift, scale)
    act = _swiglu_up(h2, wg, wu, layer, "ffn_up_" + tag)
    return _mm_residual([act], wd, xs, gate, "ffn_down_" + tag, layer)


def kernel(x, c, ctx, c_ctx, w_mod, b_mod, norm1_g, norm2_g, w_in_ab, w_four, na_rpb, w_out_ab, w_in_cd, w_pool,
           pool_scale, mla_gq, mla_gkv, w_uq, w_ukv, w_out_cd, w_ffn_gate, w_ffn_up, w_ffn_down, final_g):
    b, s, d = x.shape
    depth = w_mod.shape[0]
    nrow = -(-(b + 1) // 8) * 8
    cvec = jnp.concatenate([c, c_ctx[None, :], jnp.zeros((nrow - b - 1, d), F32)], axis=0)
    mods = _mod_call(cvec, w_mod, b_mod)
    w_in_ab, w_out_ab, w_out_cd = w_in_ab.astype(BF16), w_out_ab.astype(BF16), w_out_cd.astype(BF16)
    wg, wu, wd = w_ffn_gate, w_ffn_up, w_ffn_down.astype(BF16)
    xc = ctx.reshape(1, -1, d)
    for l in range(depth):
        last = l == depth - 1
        mod = mods[l, :b].reshape(b, 1, 6 * d)
        modc = mods[l, b].reshape(1, 1, 6 * d)
        sh1, sc1, g1, sh2, sc2, g2 = jnp.split(mod, 6, axis=-1)
        csh1, csc1, cg1, csh2, csc2, cg2 = jnp.split(modc, 6, axis=-1)
        h = _norm_call(x, norm1_g[l], sh1, sc1)
        hc = _norm_call(xc, norm1_g[l], csh1, csc1)
        i = l // 2
        if l % 2 == 0:
            ops, ctx_ops = _even_mixer(h, hc, w_in_ab, i, w_four[i], na_rpb[i], not last)
            w_out = w_out_ab
        else:
            ops, ctx_ops = _odd_mixer(h, hc, w_in_cd[i], w_pool[i], pool_scale[i], mla_gq[i], mla_gkv[i],
                                      w_uq[i], w_ukv[i], not last)
            w_out = w_out_cd
        x = _mm_residual(ops, w_out, x, g1, f"out_proj_{l}", i, tm_target=512, tn_target=d)
        x = _ffn(x, norm2_g[l], sh2, sc2, g2, wg, wu, wd, l, f"{l}")
        if not last:
            xc = _mm_residual(ctx_ops, w_out, xc, cg1, f"out_proj_ctx_{l}", i)
            xc = _ffn(xc, norm2_g[l], csh2, csc2, cg2, wg, wu, wd, l, f"ctx_{l}")
    return _norm_call(x, final_g, out_dtype=F32)
```

```python
import functools
import math

import numpy as np
import jax
import jax.numpy as jnp
from jax import lax
from jax.experimental import pallas as pl
from jax.experimental.pallas import tpu as pltpu

GRID_W = 64
HEAD_DIM = 128
EPS = 1e-6
NEG_INF = -1e30
NA_WIN_H = 8
NA_WIN_W = 16
POOL_WINDOWS = (2, 4, 8, 16)
QK_NOPE_DIM = 128
QK_ROPE_DIM = 64
V_HEAD_DIM = 128
ROPE_BASE = 10000.0
LOG2E = 1.4426950408889634

V7X_LANES = 128
V7X_VMEM_BYTES = 64 * 1024 * 1024
V7X_VMEM_CAP = 56 * 1024 * 1024
NA_ROWS_PER_TILE = 4

BF16 = jnp.bfloat16
F32 = jnp.float32


def _pick_tile(n, target, mult=16):
    if n <= target:
        return n
    for t in range(target, 0, -1):
        if n % t == 0 and t % mult == 0:
            return t
    return n


def _nbytes(shape, dtype):
    return int(np.prod(shape)) * jnp.dtype(dtype).itemsize


def _params(block_bytes, scratch_bytes=0, ndims=3):
    est = 2 * block_bytes + scratch_bytes
    limit = int(min(max(est * 5 // 4 + (4 << 20), 16 << 20), V7X_VMEM_CAP))
    return pltpu.CompilerParams(dimension_semantics=("arbitrary",) * ndims, vmem_limit_bytes=limit)


def _mod_kernel(c_ref, w_ref, b_ref, o_ref):
    cv = c_ref[...]
    s = cv * (1.0 / (1.0 + jnp.exp(-cv)))
    acc = jnp.dot(s.astype(BF16), w_ref[...].astype(BF16), preferred_element_type=F32)
    o_ref[...] = acc + b_ref[...]


def _mod_call(cvec, w_mod, b_mod):
    depth, d, n = w_mod.shape
    r = cvec.shape[0]
    tn = _pick_tile(n, 1024, V7X_LANES)
    blocks = _nbytes((r, d), F32) + _nbytes((d, tn), F32) + 2 * _nbytes((r, tn), F32)
    return pl.pallas_call(
        _mod_kernel,
        grid=(depth, n // tn),
        in_specs=[
            pl.BlockSpec((r, d), lambda l, j: (0, 0)),
            pl.BlockSpec((None, d, tn), lambda l, j: (l, 0, j)),
            pl.BlockSpec((None, 1, tn), lambda l, j: (l, 0, j)),
        ],
        out_specs=pl.BlockSpec((None, r, tn), lambda l, j: (l, 0, j)),
        out_shape=jax.ShapeDtypeStruct((depth, r, n), F32),
        compiler_params=_params(blocks, _nbytes((d, tn), BF16), 2),
        name="mod_vectors",
    )(cvec, w_mod, b_mod.reshape(depth, 1, n))


def _norm_mod_kernel(x_ref, g_ref, sh_ref, sc_ref, o_ref):
    xf = x_ref[...]
    col = g_ref[...] * (1.0 + sc_ref[...])
    y = xf * lax.rsqrt(jnp.mean(xf * xf, axis=-1, keepdims=True) + EPS)
    o_ref[...] = (y * col + sh_ref[...]).astype(o_ref.dtype)


def _norm_kernel(x_ref, g_ref, o_ref):
    xf = x_ref[...]
    y = xf * lax.rsqrt(jnp.mean(xf * xf, axis=-1, keepdims=True) + EPS) * g_ref[...]
    o_ref[...] = y.astype(o_ref.dtype)


def _norm_call(x, g, shift=None, scale=None, out_dtype=BF16):
    b, s, d = x.shape
    ts = _pick_tile(s, 1024 if b * s >= 8192 else 256)
    xspec = pl.BlockSpec((None, ts, d), lambda bi, i: (bi, i, 0))
    gspec = pl.BlockSpec((1, d), lambda bi, i: (0, 0))
    mspec = pl.BlockSpec((None, 1, d), lambda bi, i: (bi, 0, 0))
    blocks = _nbytes((ts, d), F32) + _nbytes((ts, d), out_dtype) + 3 * _nbytes((1, d), F32)
    common = dict(
        grid=(b, s // ts),
        out_specs=xspec,
        out_shape=jax.ShapeDtypeStruct((b, s, d), out_dtype),
        compiler_params=_params(blocks, 2 * _nbytes((ts, d), F32), 2),
    )
    if shift is None:
        return pl.pallas_call(_norm_kernel, in_specs=[xspec, gspec], name="rmsnorm", **common)(
            x, g.reshape(1, d))
    return pl.pallas_call(_norm_mod_kernel, in_specs=[xspec, gspec, mspec, mspec], name="rmsnorm_modulate",
                          **common)(x, g.reshape(1, d), shift, scale)


def _mm_kernel(*refs, n_ops, n_extra, epilogue):
    a_refs = refs[:n_ops]
    w_refs = refs[n_ops:2 * n_ops]
    e_refs = refs[2 * n_ops:2 * n_ops + n_extra]
    o_refs = refs[2 * n_ops + n_extra:]
    acc = None
    for a_ref, w_ref in zip(a_refs, w_refs):
        part = jnp.dot(a_ref[...], w_ref[...], preferred_element_type=F32)
        acc = part if acc is None else acc + part
    epilogue(acc, e_refs, o_refs)


def _mm_call(a_ops, w_ops, extras, epilogue, outs, grid, tm, tn, name, tmp_bytes=0):
    arrays = [a for a, _ in a_ops] + [w for w, _ in w_ops] + [e for e, _ in extras]
    in_specs = [s for _, s in a_ops] + [s for _, s in w_ops] + [s for _, s in extras]
    blocks = 0
    for arr, spec in a_ops + w_ops + extras:
        blocks += _nbytes([d for d in spec.block_shape if d is not None], arr.dtype)
    for sds, spec in outs:
        blocks += _nbytes([d for d in spec.block_shape if d is not None], sds.dtype)
    kern = functools.partial(_mm_kernel, n_ops=len(a_ops), n_extra=len(extras), epilogue=epilogue)
    res = pl.pallas_call(
        kern,
        grid=grid,
        in_specs=in_specs,
        out_specs=[s for _, s in outs],
        out_shape=[o for o, _ in outs],
        compiler_params=_params(blocks, 3 * _nbytes((tm, tn), F32) + tmp_bytes, len(grid)),
        name=name,
    )(*arrays)
    return res


def _a_spec(tm, k, col_blk=0):
    return pl.BlockSpec((None, tm, k), lambda b, i, j: (b, i, col_blk))


def _w3(w):
    return w if w.ndim == 3 else w[None]


def _w_spec(k, tn, layer=0, row_blk=0, col_off=0):
    return pl.BlockSpec((None, k, tn), lambda b, i, j: (layer, row_blk, j + col_off))


def _tile_spec(tm, tn):
    return pl.BlockSpec((None, tm, tn), lambda b, i, j: (b, i, j))


def _vec_spec(tn):
    return pl.BlockSpec((None, 1, tn), lambda b, i, j: (b, 0, j))


def _ep_store(acc, e_refs, o_refs):
    o_refs[0][...] = acc.astype(o_refs[0].dtype)


def _col_tile(n, col0, target, mult):
    return _pick_tile(math.gcd(n, col0) if col0 else n, target, mult)


def _mm_plain(a, w, out_dtype, name, layer=0, col0=0, n=None, tm_target=1024, tn_target=512):
    b, s, k = a.shape
    w = _w3(w)
    n = w.shape[2] - col0 if n is None else n
    tm = _pick_tile(s, tm_target)
    tn = _col_tile(n, col0, tn_target, V7X_LANES)
    return _mm_call([(a, _a_spec(tm, k))], [(w, _w_spec(k, tn, layer, 0, col0 // tn))], [], _ep_store,
                    [(jax.ShapeDtypeStruct((b, s, n), out_dtype), _tile_spec(tm, tn))],
                    (b, s // tm, n // tn), tm, tn, name)[0]


def _ep_chunks(acc, e_refs, o_refs, *, width):
    o_ref = o_refs[0]
    for c in range(acc.shape[1] // width):
        o_ref[c] = acc[:, c * width:(c + 1) * width].astype(o_ref.dtype)


def _mm_chunked(a, w, name, layer=0, col0=0, n=None, width=HEAD_DIM, tm_target=1024, tn_target=512):
    b, s, k = a.shape
    w = _w3(w)
    n = w.shape[2] - col0 if n is None else n
    tm = _pick_tile(s, tm_target)
    tn = _col_tile(n, col0, tn_target, width)
    cpt = tn // width
    ospec = pl.BlockSpec((None, cpt, tm, width), lambda bi, i, j: (bi, j, i, 0))
    return _mm_call([(a, _a_spec(tm, k))], [(w, _w_spec(k, tn, layer, 0, col0 // tn))], [],
                    functools.partial(_ep_chunks, width=width),
                    [(jax.ShapeDtypeStruct((b, n // width, s, width), BF16), ospec)],
                    (b, s // tm, n // tn), tm, tn, name)[0]


def _ep_residual(acc, e_refs, o_refs):
    x_ref, g_ref = e_refs
    o_refs[0][...] = x_ref[...] + g_ref[...] * acc


def _mm_residual(a_list, w, x, gate, name, layer=0, tm_target=1024, tn_target=512):
    b, s, n = x.shape
    w = _w3(w)
    tm = _pick_tile(s, tm_target)
    tn = _pick_tile(n, tn_target, V7X_LANES)
    kblk = functools.reduce(math.gcd, [a.shape[2] for a in a_list])
    a_ops, w_ops = [], []
    for a in a_list:
        for cb in range(a.shape[2] // kblk):
            a_ops.append((a, _a_spec(tm, kblk, cb)))
            w_ops.append((w, _w_spec(kblk, tn, layer, len(w_ops))))
    extras = [(x, _tile_spec(tm, tn)), (gate, _vec_spec(tn))]
    return _mm_call(a_ops, w_ops, extras, _ep_residual,
                    [(jax.ShapeDtypeStruct((b, s, n), F32), _tile_spec(tm, tn))],
                    (b, s // tm, n // tn), tm, tn, name)[0]


def _swiglu_kernel(a_ref, wg_ref, wu_ref, o_ref, wgb_ref, wub_ref):
    @pl.when((pl.program_id(1) == 0) & (pl.program_id(2) == 0))
    def _():
        wgb_ref[...] = wg_ref[...].astype(BF16)
        wub_ref[...] = wu_ref[...].astype(BF16)

    a = a_ref[...]
    g = jnp.dot(a, wgb_ref[...], preferred_element_type=F32)
    u = jnp.dot(a, wub_ref[...], preferred_element_type=F32)
    o_ref[...] = (g * (1.0 / (1.0 + jnp.exp(-g))) * u).astype(o_ref.dtype)


def _swiglu_up(a, wg, wu, layer, name, tm_target=1024, tn_target=512):
    b, s, k = a.shape
    n = wg.shape[2]
    tm = _pick_tile(s, tm_target)
    tn = _pick_tile(n, tn_target, V7X_LANES)
    blocks = _nbytes((tm, k), BF16) + 2 * _nbytes((k, tn), F32) + _nbytes((tm, tn), BF16)
    wspec = pl.BlockSpec((None, k, tn), lambda j, bi, i: (layer, 0, j))
    return pl.pallas_call(
        _swiglu_kernel,
        grid=(n // tn, b, s // tm),
        in_specs=[pl.BlockSpec((None, tm, k), lambda j, bi, i: (bi, i, 0)), wspec, wspec],
        out_specs=pl.BlockSpec((None, tm, tn), lambda j, bi, i: (bi, i, j)),
        out_shape=jax.ShapeDtypeStruct((b, s, n), BF16),
        scratch_shapes=[pltpu.VMEM((k, tn), BF16), pltpu.VMEM((k, tn), BF16)],
        compiler_params=_params(blocks, 2 * _nbytes((k, tn), BF16) + 4 * _nbytes((tm, tn), F32), 3),
        name=name,
    )(a, wg, wu)


def _rope_pair(hi, tab):
    z = hi * tab
    return z + pltpu.roll(z, QK_ROPE_DIM, axis=1)


def _ep_qrope(acc, e_refs, o_refs, *, heads):
    tab_ref, = e_refs
    o_ref = o_refs[0]
    tab = tab_ref[...]
    blk = QK_NOPE_DIM + 2 * QK_ROPE_DIM
    for h in range(heads):
        o_ref[h, :, 0:QK_NOPE_DIM] = acc[:, h * blk:h * blk + QK_NOPE_DIM].astype(o_ref.dtype)
        hi = acc[:, h * blk + QK_NOPE_DIM:(h + 1) * blk]
        o_ref[h, :, QK_NOPE_DIM:blk] = _rope_pair(hi, tab).astype(o_ref.dtype)


def _mm_qrope(a, w, tab, name, tm_target=1024, heads_per_tile=6):
    b, s, k = a.shape
    blk = QK_NOPE_DIM + 2 * QK_ROPE_DIM
    nh = w.shape[1] // blk
    hpt = math.gcd(nh, heads_per_tile)
    tn = hpt * blk
    tm = _pick_tile(s, tm_target)
    tspec = pl.BlockSpec((tm, 2 * QK_ROPE_DIM), lambda bi, i, j: (i, 0))
    ospec = pl.BlockSpec((None, hpt, tm, blk), lambda bi, i, j: (bi, j, i, 0))
    return _mm_call([(a, _a_spec(tm, k))], [(_w3(w), _w_spec(k, tn))], [(tab, tspec)],
                    functools.partial(_ep_qrope, heads=hpt),
                    [(jax.ShapeDtypeStruct((b, nh, s, blk), BF16), ospec)],
                    (b, s // tm, nh // hpt), tm, tn, name)[0]


def _ep_kv(acc, e_refs, o_refs, *, heads):
    kr_ref, = e_refs
    k_ref, v_ref = o_refs
    kr = kr_ref[...]
    blk = QK_NOPE_DIM + V_HEAD_DIM
    for h in range(heads):
        k_ref[h, :, 0:QK_NOPE_DIM] = acc[:, h * blk:h * blk + QK_NOPE_DIM].astype(k_ref.dtype)
        k_ref[h, :, QK_NOPE_DIM:QK_NOPE_DIM + kr.shape[1]] = kr
        v_ref[h] = acc[:, h * blk + QK_NOPE_DIM:(h + 1) * blk].astype(v_ref.dtype)


def _mm_kv(a, w, krot, name, tm_target=1024, heads_per_tile=6):
    b, s, k = a.shape
    blk = QK_NOPE_DIM + V_HEAD_DIM
    nh = w.shape[1] // blk
    hpt = math.gcd(nh, heads_per_tile)
    tn = hpt * blk
    tm = _pick_tile(s, tm_target)
    kd = QK_NOPE_DIM + krot.shape[2]
    krspec = pl.BlockSpec((None, tm, krot.shape[2]), lambda bi, i, j: (bi, i, 0))
    kspec = pl.BlockSpec((None, hpt, tm, kd), lambda bi, i, j: (bi, j, i, 0))
    vspec = pl.BlockSpec((None, hpt, tm, V_HEAD_DIM), lambda bi, i, j: (bi, j, i, 0))
    return _mm_call([(a, _a_spec(tm, k))], [(_w3(w), _w_spec(k, tn))], [(krot, krspec)],
                    functools.partial(_ep_kv, heads=hpt),
                    [(jax.ShapeDtypeStruct((b, nh, s, kd), BF16), kspec),
                     (jax.ShapeDtypeStruct((b, nh, s, V_HEAD_DIM), BF16), vspec)],
                    (b, s // tm, nh // hpt), tm, tn, name)


def _dft_matrix(n):
    m = jnp.arange(n, dtype=jnp.int32)
    side = 1
    while side * side < n:
        side *= 2
    if side * side != n:
        k = jnp.arange(n, dtype=jnp.int32)
        ang = (2.0 * math.pi / n) * ((k[:, None] * m[None, :]) % n).astype(F32)
        return jnp.concatenate([jnp.cos(ang), jnp.sin(ang)], axis=0).astype(BF16)[None]
    a = jnp.arange(side, dtype=jnp.int32)
    ang_x = (2.0 * math.pi / side) * ((a[:, None] * m[None, :]) % side).astype(F32)
    ang_y = (2.0 * math.pi / n) * ((a[:, None] * m[None, :]) % n).astype(F32)
    cx, sx = jnp.cos(ang_x), jnp.sin(ang_x)
    cy, sy = jnp.cos(ang_y)[None], jnp.sin(ang_y)[None]
    p = jnp.concatenate([cx, sx], axis=0)[:, None, :]
    q = jnp.concatenate([sx, -cx], axis=0)[:, None, :]
    return (p * cy - q * sy).astype(BF16).reshape(1, 2 * n, n)


def _dft_positions_dense(u):
    b, s, w = u.shape
    mat = _dft_matrix(s)
    tm = _pick_tile(2 * s, 1024)
    tn = _pick_tile(w, 512, V7X_LANES)
    aspec = pl.BlockSpec((None, tm, s), lambda bi, i, j: (0, i, 0))
    wspec = pl.BlockSpec((None, s, tn), lambda bi, i, j: (bi, 0, j))
    return _mm_call([(mat, aspec)], [(u, wspec)], [], _ep_store,
                    [(jax.ShapeDtypeStruct((b, 2 * s, w), BF16), _tile_spec(tm, tn))],
                    (b, 2 * s // tm, w // tn), tm, tn, "fourier_positions")[0]


DFT_ROWS_PER_STEP = 16


def _dft_stage1_kernel(x_ref, a_ref, tc_ref, ts_ref, o_ref, *, side, width, n2_per_step):
    g = jnp.dot(a_ref[...], x_ref[...], preferred_element_type=F32)
    gr, gs = g[0:side], g[side:2 * side]
    for n2 in range(n2_per_step):
        tcs = tc_ref[:, n2 * V7X_LANES:(n2 + 1) * V7X_LANES]
        tss = ts_ref[:, n2 * V7X_LANES:(n2 + 1) * V7X_LANES]
        for c0 in range(n2 * width, (n2 + 1) * width, V7X_LANES):
            cols = slice(c0, c0 + V7X_LANES)
            o_ref[0:side, cols] = (gr[:, cols] * tcs - gs[:, cols] * tss).astype(o_ref.dtype)
            o_ref[side:2 * side, cols] = (gr[:, cols] * tss + gs[:, cols] * tcs).astype(o_ref.dtype)


def _dft_stage2_kernel(h_ref, m_ref, o_ref, scr_ref, *, side):
    rows = h_ref.shape[1]
    ngrp = scr_ref.shape[0]
    for j in range(rows):
        hcat = jnp.concatenate([h_ref[0, j], h_ref[1, j]], axis=0)
        res = jnp.dot(m_ref[...], hcat, preferred_element_type=F32)
        for c in range(ngrp):
            cols = slice(c * V7X_LANES, (c + 1) * V7X_LANES)
            scr_ref[c, pl.ds(j, side, stride=rows), :] = res[0:side, cols]
            scr_ref[c, pl.ds(side * rows + j, side, stride=rows), :] = res[side:2 * side, cols]
    for c in range(ngrp):
        cols = slice(c * V7X_LANES, (c + 1) * V7X_LANES)
        for p in range(2):
            part = scr_ref[c, p * side * rows:(p + 1) * side * rows, :].reshape(side, rows, V7X_LANES)
            o_ref[p, :, :, cols] = part.astype(o_ref.dtype)


def _dft_positions(u):
    b, s, w = u.shape
    side = math.isqrt(s)
    if side * side != s or side % DFT_ROWS_PER_STEP or w % V7X_LANES:
        return _dft_positions_dense(u)
    idx = np.arange(side)
    ang = 2.0 * np.pi * ((idx[:, None] * idx[None, :]) % side) / side
    cs, sn = np.cos(ang), np.sin(ang)
    a1 = jnp.asarray(np.concatenate([cs, sn], axis=0), dtype=F32).astype(BF16)
    m2 = jnp.asarray(np.block([[cs, -sn], [sn, cs]]), dtype=F32).astype(BF16)
    tw = 2.0 * np.pi * (idx[:, None] * idx[None, :]) / s
    tcos = jnp.repeat(jnp.asarray(np.cos(tw), dtype=F32), V7X_LANES, axis=1)
    tsin = jnp.repeat(jnp.asarray(np.sin(tw), dtype=F32), V7X_LANES, axis=1)
    n2s = DFT_ROWS_PER_STEP
    tc = n2s * w
    blocks1 = _nbytes((side, tc), BF16) + _nbytes((2 * side, side), BF16) + 2 * _nbytes((side, n2s * V7X_LANES), F32) \
        + _nbytes((2 * side, tc), BF16)
    h = pl.pallas_call(
        functools.partial(_dft_stage1_kernel, side=side, width=w, n2_per_step=n2s),
        grid=(b, side // n2s),
        in_specs=[
            pl.BlockSpec((None, side, tc), lambda bi, j: (bi, 0, j)),
            pl.BlockSpec((2 * side, side), lambda bi, j: (0, 0)),
            pl.BlockSpec((side, n2s * V7X_LANES), lambda bi, j: (0, j)),
            pl.BlockSpec((side, n2s * V7X_LANES), lambda bi, j: (0, j)),
        ],
        out_specs=pl.BlockSpec((None, 2 * side, tc), lambda bi, j: (bi, 0, j)),
        out_shape=jax.ShapeDtypeStruct((b, 2 * side, side * w), BF16),
        compiler_params=_params(blocks1, 3 * _nbytes((2 * side, tc), F32), 2),
        name="fourier_positions_stage1",
    )(u.reshape(b, side, side * w), a1, tcos, tsin)
    rows = DFT_ROWS_PER_STEP
    blocks2 = 2 * _nbytes((2, rows, side, w), BF16) + _nbytes((2 * side, 2 * side), BF16)
    f2 = pl.pallas_call(
        functools.partial(_dft_stage2_kernel, side=side),
        grid=(b, side // rows),
        in_specs=[
            pl.BlockSpec((None, 2, rows, side, w), lambda bi, i: (bi, 0, i, 0, 0)),
            pl.BlockSpec((2 * side, 2 * side), lambda bi, i: (0, 0)),
        ],
        out_specs=pl.BlockSpec((None, 2, side, rows, w), lambda bi, i: (bi, 0, 0, i, 0)),
        out_shape=jax.ShapeDtypeStruct((b, 2, side, side, w), BF16),
        scratch_shapes=[pltpu.VMEM((w // V7X_LANES, 2 * side * rows, V7X_LANES), F32)],
        compiler_params=_params(blocks2, _nbytes((2 * side * rows, w), F32) + 4 * _nbytes((2 * side, w), F32), 2),
        name="fourier_positions_stage2",
    )(h.reshape(b, 2, side, side, w), m2)
    return f2.reshape(b, 2 * s, w)


def _four_channel_kernel(fr_ref, fi_ref, cs_ref, w_ref, o_ref, *, norm):
    gd = w_ref.shape[1]
    for g in range(w_ref.shape[0]):
        cols = slice(g * gd, (g + 1) * gd)
        f = jnp.dot(fr_ref[:, cols].astype(BF16), cs_ref[0:gd, :], preferred_element_type=F32)
        f = f + jnp.dot(fi_ref[:, cols].astype(BF16), cs_ref[gd:2 * gd, :], preferred_element_type=F32)
        f = (f * norm).astype(BF16)
        o_ref[:, cols] = jnp.dot(f, w_ref[g], preferred_element_type=F32).astype(o_ref.dtype)


def _four_channels(f2, w_four):
    b, s2, w = f2.shape
    s = s2 // 2
    g, gd, _ = w_four.shape
    idx = np.arange(gd)
    ang = 2.0 * np.pi * ((idx[:, None] * idx[None, :]) % gd) / gd
    cs = jnp.asarray(np.concatenate([np.cos(ang), -np.sin(ang)], axis=0), dtype=F32).astype(BF16)
    ts = _pick_tile(s, 1024)
    nblk = s // ts
    blocks = 2 * _nbytes((ts, w), f2.dtype) + _nbytes((ts, w), BF16) + _nbytes((2 * gd, gd), BF16) \
        + _nbytes((g, gd, gd), BF16)
    return pl.pallas_call(
        functools.partial(_four_channel_kernel, norm=1.0 / math.sqrt(s * gd)),
        grid=(b, nblk),
        in_specs=[
            pl.BlockSpec((None, ts, w), lambda bi, i: (bi, i, 0)),
            pl.BlockSpec((None, ts, w), lambda bi, i: (bi, nblk + i, 0)),
            pl.BlockSpec((2 * gd, gd), lambda bi, i: (0, 0)),
            pl.BlockSpec((g, gd, gd), lambda bi, i: (0, 0, 0)),
        ],
        out_specs=pl.BlockSpec((None, ts, w), lambda bi, i: (bi, i, 0)),
        out_shape=jax.ShapeDtypeStruct((b, s, w), BF16),
        compiler_params=_params(blocks, 6 * _nbytes((ts, gd), F32), 2),
        name="fourier_channels",
    )(f2, f2, cs, w_four.astype(BF16))


def _na_plan(rows):
    rpt = NA_ROWS_PER_TILE
    kh = min(NA_WIN_H, rows)
    wrows = min(kh + rpt - 1, rows)
    cases, ws_list, case_list = {}, [], []
    for t in range(rows // rpt):
        r0 = t * rpt
        ws = int(np.clip(r0 - kh // 2, 0, rows - wrows))
        plan = []
        for r in range(r0, r0 + rpt):
            rs = int(np.clip(r - kh // 2, 0, rows - kh))
            plan.append(tuple((ws + w) - r + NA_WIN_H - 1 if rs <= ws + w < rs + kh else None
                              for w in range(wrows)))
        plan = tuple(plan)
        cases.setdefault(plan, len(cases))
        ws_list.append(ws)
        case_list.append(cases[plan])
    return wrows, ws_list, case_list, list(cases)


def _na_block_table(rpb):
    qcol = np.arange(GRID_W)[:, None]
    kcol = np.arange(GRID_W)[None, :]
    cstart = np.clip(qcol - NA_WIN_W // 2, 0, GRID_W - NA_WIN_W)
    col_ok = (kcol >= cstart) & (kcol < cstart + NA_WIN_W)
    cidx = np.clip(kcol - qcol + NA_WIN_W - 1, 0, 2 * NA_WIN_W - 2)
    csel = jnp.asarray(np.eye(rpb.shape[2], dtype=np.float32)[cidx])
    blk = jnp.einsum("hrs,qks->hrqk", rpb, csel, precision=lax.Precision.HIGHEST)
    blk = jnp.where(jnp.asarray(col_ok), blk * LOG2E, NEG_INF)
    return jnp.concatenate([blk, blk], axis=-1)


def _ones_column(rows, width):
    lane = lax.broadcasted_iota(jnp.int32, (rows, width), 1)
    return jnp.where(lane == 0, 1.0, 0.0).astype(BF16)


def _na_kernel(q_ref, k_ref, v_ref, kc_ref, vc_ref, blk_ref, o_ref, bias_ref, vx_ref, vcx_ref,
               *, tiles, tq, tk, scale, plans):
    d = v_ref.shape[1]

    @pl.when(pl.program_id(1) == 0)
    def _():
        for c, plan in enumerate(plans):
            for qr, row in enumerate(plan):
                for kr, ridx in enumerate(row):
                    half = slice((kr % 2) * GRID_W, (kr % 2 + 1) * GRID_W)
                    if ridx is None:
                        blk = jnp.full((GRID_W, GRID_W), NEG_INF, F32)
                    else:
                        blk = blk_ref[ridx][:, half]
                    bias_ref[c, qr * GRID_W:(qr + 1) * GRID_W, kr * GRID_W:(kr + 1) * GRID_W] = blk

    vx_ref[:, 0:d] = v_ref[...]
    vx_ref[:, d:2 * d] = _ones_column(v_ref.shape[0], d)
    vcx_ref[:, 0:d] = vc_ref[...]
    vcx_ref[:, d:2 * d] = _ones_column(vc_ref.shape[0], d)
    kc = kc_ref[...]
    dn = (((1,), (1,)), ((), ()))
    c2 = scale * LOG2E

    for t, (ws, case) in enumerate(tiles):
        q0, k0 = t * tq, ws * GRID_W
        q = q_ref[q0:q0 + tq, :]
        s_loc = lax.dot_general(q, k_ref[k0:k0 + tk, :], dn, preferred_element_type=F32) * c2 + bias_ref[case]
        s_ctx = lax.dot_general(q, kc, dn, preferred_element_type=F32) * c2
        m = jnp.maximum(jnp.max(s_loc, axis=-1, keepdims=True), jnp.max(s_ctx, axis=-1, keepdims=True))
        p_loc = jnp.exp2(s_loc - m).astype(BF16)
        p_ctx = jnp.exp2(s_ctx - m).astype(BF16)
        acc = jnp.dot(p_loc, vx_ref[k0:k0 + tk, :], preferred_element_type=F32)
        acc = acc + jnp.dot(p_ctx, vcx_ref[...], preferred_element_type=F32)
        o_ref[q0:q0 + tq, :] = (acc[:, 0:d] / acc[:, d:d + 1]).astype(o_ref.dtype)


def _na_attention(qkv, qkvc, rpb):
    b, c3, s, d = qkv.shape
    nh = c3 // 3
    lc = qkvc.shape[2] // b
    rows = s // GRID_W
    wrows, ws_list, case_list, plans = _na_plan(rows)
    blk = _na_block_table(rpb)
    nrel = blk.shape[1]
    tq = NA_ROWS_PER_TILE * GRID_W
    tk = wrows * GRID_W
    blocks = 3 * _nbytes((s, d), BF16) + 2 * _nbytes((lc, d), BF16) + _nbytes((nrel, GRID_W, 2 * GRID_W), F32) \
        + _nbytes((s, d), BF16)
    scratch = _nbytes((len(plans), tq, tk), F32) + _nbytes((s + lc, 2 * d), BF16)
    return pl.pallas_call(
        functools.partial(_na_kernel, tiles=list(zip(ws_list, case_list)), tq=tq, tk=tk, scale=d ** -0.5,
                          plans=plans),
        grid=(nh, b),
        in_specs=[
            pl.BlockSpec((None, None, s, d), lambda h, bi: (bi, h, 0, 0)),
            pl.BlockSpec((None, None, s, d), lambda h, bi: (bi, nh + h, 0, 0)),
            pl.BlockSpec((None, None, s, d), lambda h, bi: (bi, 2 * nh + h, 0, 0)),
            pl.BlockSpec((None, None, lc, d), lambda h, bi: (0, nh + h, bi, 0)),
            pl.BlockSpec((None, None, lc, d), lambda h, bi: (0, 2 * nh + h, bi, 0)),
            pl.BlockSpec((None, nrel, GRID_W, 2 * GRID_W), lambda h, bi: (h, 0, 0, 0)),
        ],
        out_specs=pl.BlockSpec((None, s, d), lambda h, bi: (bi, 0, h)),
        out_shape=jax.ShapeDtypeStruct((b, s, nh * d), BF16),
        scratch_shapes=[pltpu.VMEM((len(plans), tq, tk), F32), pltpu.VMEM((s, 2 * d), BF16),
                        pltpu.VMEM((lc, 2 * d), BF16)],
        compiler_params=_params(blocks, scratch + 8 * _nbytes((tq, tk + lc), F32), 2),
        name="neighbourhood_attention",
    )(qkv, qkv, qkv, qkvc, qkvc, blk)


def _dense_attn_kernel(*refs, scale, n_src, chunks, heads):
    q_ref = refs[0]
    k_refs = refs[1:1 + n_src]
    v_refs = refs[1 + n_src:1 + 2 * n_src]
    o_ref = refs[1 + 2 * n_src]
    vx_refs = refs[2 + 2 * n_src:]
    dv = v_refs[0].shape[2]

    @pl.when(pl.program_id(2) == 0)
    def _():
        for v_ref, vx_ref in zip(v_refs, vx_refs):
            for hh in range(heads):
                vx_ref[hh, :, 0:dv] = v_ref[hh]
                vx_ref[hh, :, dv:2 * dv] = _ones_column(v_ref.shape[1], dv)

    c2 = scale * LOG2E
    for hh in range(heads):
        q = q_ref[hh]
        m = acc = None
        for src, start, size in chunks:
            s = lax.dot_general(q, k_refs[src][hh, start:start + size, :], (((1,), (1,)), ((), ())),
                                preferred_element_type=F32)
            m_chunk = jnp.max(s, axis=-1, keepdims=True)
            m_new = m_chunk if m is None else jnp.maximum(m, m_chunk)
            p = jnp.exp2((s - m_new) * c2)
            pv = jnp.dot(p.astype(BF16), vx_refs[src][hh, start:start + size, :], preferred_element_type=F32)
            acc = pv if acc is None else acc * jnp.exp2((m - m_new) * c2) + pv
            m = m_new
        o_ref[:, hh * dv:(hh + 1) * dv] = (acc[:, 0:dv] / acc[:, dv:dv + 1]).astype(o_ref.dtype)


def _head_major_spec(arr, batch, rows, off, tiled, heads):
    per_batch = arr.shape[2] * arr.shape[0] // batch // rows
    hb = off // heads
    if arr.shape[0] == 1:
        return pl.BlockSpec((None, heads, rows, arr.shape[3]),
                            lambda bi, h, i: (0, hb + h, bi * per_batch + (i if tiled else 0), 0))
    return pl.BlockSpec((None, heads, rows, arr.shape[3]), lambda bi, h, i: (bi, hb + h, i if tiled else 0, 0))


def _dense_attention(q, kv_list, nh, q_off, scale, name, batch, tq_target=1024, chunk_target=512, heads=2):
    dk = q.shape[3]
    sq = q.shape[0] * q.shape[2] // batch
    dv = kv_list[0][1].shape[3]
    offsets = [q_off] + [o for _, _, ko, vo in kv_list for o in (ko, vo)]
    heads = functools.reduce(math.gcd, [nh] + offsets, heads)
    tq = _pick_tile(sq, tq_target)
    nq = sq // tq
    chunks, arrays_k, arrays_v, scratch = [], [], [], []
    blocks = heads * (_nbytes((tq, dk), BF16) + _nbytes((tq, dv), BF16))
    tmp = 4 * _nbytes((tq, chunk_target), F32) + 2 * _nbytes((tq, 2 * dv), F32)
    kspecs, vspecs = [], []
    for src, (k, v, k_off, v_off) in enumerate(kv_list):
        sk = k.shape[0] * k.shape[2] // batch
        chunks += [(src, st, min(chunk_target, sk - st)) for st in range(0, sk, chunk_target)]
        kspecs.append(_head_major_spec(k, batch, sk, k_off, False, heads))
        vspecs.append(_head_major_spec(v, batch, sk, v_off, False, heads))
        arrays_k.append(k)
        arrays_v.append(v)
        scratch.append(pltpu.VMEM((heads, sk, 2 * dv), BF16))
        blocks += heads * (_nbytes((sk, dk), BF16) + _nbytes((sk, dv), BF16))
        tmp += heads * _nbytes((sk, 2 * dv), BF16)
    if q.shape[0] == 1:
        out_spec = pl.BlockSpec((None, tq, heads * dv), lambda bi, h, i: (0, bi * nq + i, h))
    else:
        out_spec = pl.BlockSpec((None, tq, heads * dv), lambda bi, h, i: (bi, i, h))
    return pl.pallas_call(
        functools.partial(_dense_attn_kernel, scale=scale, n_src=len(kv_list), chunks=chunks, heads=heads),
        grid=(batch, nh // heads, nq),
        in_specs=[_head_major_spec(q, batch, tq, q_off, True, heads)] + kspecs + vspecs,
        out_specs=out_spec,
        out_shape=jax.ShapeDtypeStruct((q.shape[0], q.shape[2], nh * dv), BF16),
        scratch_shapes=scratch,
        compiler_params=_params(blocks, tmp, 3),
        name=name,
    )(q, *arrays_k, *arrays_v)


def _pool_kernel(u_ref, w_ref, sc_ref, o_ref, *, windows):
    n, gd = u_ref.shape[0], w_ref.shape[1]
    t = lax.broadcasted_iota(jnp.int32, (n, gd), 0)
    for g, window in enumerate(windows):
        cols = slice(g * gd, (g + 1) * gd)
        u = u_ref[:, cols]
        half = window // 2
        ssum = u
        for j in range(1, half + 1):
            ssum = ssum + jnp.where(t >= j, pltpu.roll(u, j, axis=0), 0.0)
        for j in range(1, half):
            ssum = ssum + jnp.where(t < n - j, pltpu.roll(u, n - j, axis=0), 0.0)
        cnt = jnp.minimum(t + half, n) - jnp.maximum(t - half, 0)
        p = ssum / cnt.astype(F32) - u
        y = jnp.dot(p.astype(BF16), w_ref[g], preferred_element_type=F32)
        o_ref[:, cols] = (y * sc_ref[:, cols]).astype(o_ref.dtype)


def _pool_mix(u, w_pool, pool_scale):
    b, s, w = u.shape
    g, gd, _ = w_pool.shape
    blocks = _nbytes((s, w), F32) + _nbytes((g, gd, gd), BF16) + _nbytes((s, w), BF16)
    return pl.pallas_call(
        functools.partial(_pool_kernel, windows=POOL_WINDOWS[:g]),
        grid=(b,),
        in_specs=[
            pl.BlockSpec((None, s, w), lambda bi: (bi, 0, 0)),
            pl.BlockSpec((g, gd, gd), lambda bi: (0, 0, 0)),
            pl.BlockSpec((1, w), lambda bi: (0, 0)),
        ],
        out_specs=pl.BlockSpec((None, s, w), lambda bi: (bi, 0, 0)),
        out_shape=jax.ShapeDtypeStruct((b, s, w), BF16),
        compiler_params=_params(blocks, 6 * _nbytes((s, gd), F32), 1),
        name="pool_mix",
    )(u, w_pool.astype(BF16), pool_scale.reshape(1, w))


def _rope_table(n):
    half = QK_ROPE_DIM // 2
    nf = half // 2
    t = np.arange(n)
    inv = ROPE_BASE ** (-np.arange(nf, dtype=np.float64) / nf)
    ar = (t // GRID_W)[:, None] * inv[None, :]
    ac = (t % GRID_W)[:, None] * inv[None, :]
    cr, sr, cc, sc = np.cos(ar), np.sin(ar), np.cos(ac), np.sin(ac)
    return jnp.asarray(np.concatenate([cr, cr, cc, cc, -sr, sr, -sc, sc], axis=1), dtype=F32)


def _identity_rope_table(n):
    return jnp.concatenate([jnp.ones((n, QK_ROPE_DIM), F32), jnp.zeros((n, QK_ROPE_DIM), F32)], axis=1)


def _swap_rope_cols(w):
    nf = QK_ROPE_DIM // 4
    lead = w.shape[:-1]
    return jnp.flip(w.reshape(lead + (2, 2, nf)), axis=-2).reshape(lead + (QK_ROPE_DIM,))


def _even_mixer(h, hc, w_in, layer, w_four, rpb, with_ctx_out):
    fw = w_four.shape[0] * w_four.shape[1]
    nh = rpb.shape[0]
    u = _mm_plain(h, w_in, BF16, "even_in_fourier", layer, 0, fw, tm_target=2048)
    qkv = _mm_chunked(h, w_in, "even_in_qkv", layer, fw, tm_target=2048)
    qkvc = _mm_chunked(hc, w_in, "even_in_qkv_ctx", layer, fw)
    y_f = _four_channels(_dft_positions(u), w_four)
    y_a = _na_attention(qkv, qkvc, rpb)
    ctx_ops = None
    if with_ctx_out:
        b = h.shape[0]
        uc = _mm_plain(hc, w_in, BF16, "even_in_fourier_ctx", layer, 0, fw).reshape(b, -1, fw)
        yc_f = _four_channels(_dft_positions(uc), w_four).reshape(1, -1, fw)
        yc_a = _dense_attention(qkvc, [(qkvc, qkvc, nh, 2 * nh)], nh, 0, HEAD_DIM ** -0.5, "ctx_attention_even", b)
        ctx_ops = [yc_f, yc_a]
    return [y_f, y_a], ctx_ops


def _ep_odd_in(acc, e_refs, o_refs, *, splits):
    gq_ref, gkv_ref, tab_ref = e_refs
    u_ref, cq_ref, ckv_ref, kr_ref = o_refs
    o_q, o_kv, o_kr = splits

    def rms(v, g_ref):
        return (v * lax.rsqrt(jnp.mean(v * v, axis=-1, keepdims=True) + EPS) * g_ref[...]).astype(BF16)

    u_ref[...] = acc[:, 0:o_q]
    cq_ref[...] = rms(acc[:, o_q:o_kv], gq_ref)
    ckv_ref[...] = rms(acc[:, o_kv:o_kr], gkv_ref)
    r = _rope_pair(acc[:, o_kr:], tab_ref[...])
    lane = lax.broadcasted_iota(jnp.int32, r.shape, 1)
    kr_ref[...] = jnp.where(lane < QK_ROPE_DIM, r, 0.0).astype(kr_ref.dtype)


def _mm_odd_in(a, w, g_q, g_kv, tab, splits, name, tm_target=1024):
    b, s, k = a.shape
    n = w.shape[1]
    o_q, o_kv, o_kr = splits
    tm = _pick_tile(s, tm_target)

    def row(width):
        return pl.BlockSpec((None, tm, width), lambda bi, i, j: (bi, i, 0))

    def vec(width):
        return pl.BlockSpec((1, width), lambda bi, i, j: (0, 0))

    extras = [(g_q.reshape(1, -1), vec(o_kv - o_q)), (g_kv.reshape(1, -1), vec(o_kr - o_kv)),
              (tab, pl.BlockSpec((tm, n - o_kr), lambda bi, i, j: (i, 0)))]
    outs = [(jax.ShapeDtypeStruct((b, s, o_q), F32), row(o_q)),
            (jax.ShapeDtypeStruct((b, s, o_kv - o_q), BF16), row(o_kv - o_q)),
            (jax.ShapeDtypeStruct((b, s, o_kr - o_kv), BF16), row(o_kr - o_kv)),
            (jax.ShapeDtypeStruct((b, s, n - o_kr), BF16), row(n - o_kr))]
    return _mm_call([(a, _a_spec(tm, k))], [(_w3(w), _w_spec(k, n))], extras,
                    functools.partial(_ep_odd_in, splits=splits), outs, (b, s // tm, 1), tm, n, name)


def _odd_mixer(h, hc, w_in, w_pool, pool_scale, g_q, g_kv, w_uq, w_ukv, with_ctx_out):
    b, s, d = h.shape
    lc = hc.shape[1] // b
    pw = w_pool.shape[0] * w_pool.shape[1]
    q_rank, kv_rank = g_q.shape[0], g_kv.shape[0]
    nh = w_uq.shape[1] // (QK_NOPE_DIM + QK_ROPE_DIM)
    splits = (pw, pw + q_rank, pw + q_rank + kv_rank)
    w_in = w_in.astype(BF16)
    w_in2 = jnp.concatenate([w_in, _swap_rope_cols(w_in[:, splits[2]:])], axis=1)
    wq = w_uq.astype(BF16).reshape(q_rank, nh, QK_NOPE_DIM + QK_ROPE_DIM)
    wq_r = wq[..., QK_NOPE_DIM:]
    wq2 = jnp.concatenate([wq[..., :QK_NOPE_DIM], wq_r, _swap_rope_cols(wq_r)], axis=-1).reshape(q_rank, -1)
    w_ukv = w_ukv.astype(BF16)
    tab, tab_c = _rope_table(s), _identity_rope_table(b * lc)

    u, cq_n, ckv_n, k_rot = _mm_odd_in(h, w_in2, g_q, g_kv, tab, splits, "odd_in")
    uc, cqc_n, ckvc_n, k_rot_c = _mm_odd_in(hc, w_in2, g_q, g_kv, tab_c, splits, "odd_in_ctx")
    k_l, v_l = _mm_kv(ckv_n, w_ukv, k_rot, "odd_up_kv")
    k_c, v_c = _mm_kv(ckvc_n, w_ukv, k_rot_c, "odd_up_kv_ctx")
    q = _mm_qrope(cq_n, wq2, tab, "odd_up_q")
    scale = (QK_NOPE_DIM + QK_ROPE_DIM) ** -0.5
    attn = _dense_attention(q, [(k_l, v_l, 0, 0), (k_c, v_c, 0, 0)], nh, 0, scale, "mla_attention", b)
    pooled = _pool_mix(u, w_pool, pool_scale)
    ctx_ops = None
    if with_ctx_out:
        qc = _mm_qrope(cqc_n, wq2, tab_c, "odd_up_q_ctx")
        attn_c = _dense_attention(qc, [(k_c, v_c, 0, 0)], nh, 0, scale, "ctx_attention_odd", b)
        pooled_c = _pool_mix(uc.reshape(b, lc, pw), w_pool, pool_scale).reshape(1, b * lc, pw)
        ctx_ops = [pooled_c, attn_c]
    return [pooled, attn], ctx_ops


def _ffn(xs, norm_g, shift, scale, gate, wg, wu, wd, layer, tag):
    h2 = _norm_call(xs, norm_g, shift, scale)
    act = _swiglu_up(h2, wg, wu, layer, "ffn_up_" + tag)
    return _mm_residual([act], wd, xs, gate, "ffn_down_" + tag, layer)


def kernel(x, c, ctx, c_ctx, w_mod, b_mod, norm1_g, norm2_g, w_in_ab, w_four, na_rpb, w_out_ab, w_in_cd, w_pool,
           pool_scale, mla_gq, mla_gkv, w_uq, w_ukv, w_out_cd, w_ffn_gate, w_ffn_up, w_ffn_down, final_g):
    b, s, d = x.shape
    depth = w_mod.shape[0]
    nrow = -(-(b + 1) // 8) * 8
    cvec = jnp.concatenate([c, c_ctx[None, :], jnp.zeros((nrow - b - 1, d), F32)], axis=0)
    mods = _mod_call(cvec, w_mod, b_mod)
    w_in_ab, w_out_ab, w_out_cd = w_in_ab.astype(BF16), w_out_ab.astype(BF16), w_out_cd.astype(BF16)
    wg, wu, wd = w_ffn_gate, w_ffn_up, w_ffn_down.astype(BF16)
    xc = ctx.reshape(1, -1, d)
    for l in range(depth):
        last = l == depth - 1
        mod = mods[l, :b].reshape(b, 1, 6 * d)
        modc = mods[l, b].reshape(1, 1, 6 * d)
        sh1, sc1, g1, sh2, sc2, g2 = jnp.split(mod, 6, axis=-1)
        csh1, csc1, cg1, csh2, csc2, cg2 = jnp.split(modc, 6, axis=-1)
        h = _norm_call(x, norm1_g[l], sh1, sc1)
        hc = _norm_call(xc, norm1_g[l], csh1, csc1)
        i = l // 2
        if l % 2 == 0:
            ops, ctx_ops = _even_mixer(h, hc, w_in_ab, i, w_four[i], na_rpb[i], not last)
            w_out = w_out_ab
        else:
            ops, ctx_ops = _odd_mixer(h, hc, w_in_cd[i], w_pool[i], pool_scale[i], mla_gq[i], mla_gkv[i],
                                      w_uq[i], w_ukv[i], not last)
            w_out = w_out_cd
        x = _mm_residual(ops, w_out, x, g1, f"out_proj_{l}", i, tm_target=512, tn_target=d)
        x = _ffn(x, norm2_g[l], sh2, sc2, g2, wg, wu, wd, l, f"{l}")
        if not last:
            xc = _mm_residual(ctx_ops, w_out, xc, cg1, f"out_proj_ctx_{l}", i)
            xc = _ffn(xc, norm2_g[l], csh2, csc2, cg2, wg, wu, wd, l, f"ctx_{l}")
    return _norm_call(x, final_g, out_dtype=F32)
```

```python
import functools
import math

import numpy as np
import jax
import jax.numpy as jnp
from jax import lax
from jax.experimental import pallas as pl
from jax.experimental.pallas import tpu as pltpu

GRID_W = 64
HEAD_DIM = 128
EPS = 1e-6
NEG_INF = -1e30
NA_WIN_H = 8
NA_WIN_W = 16
POOL_WINDOWS = (2, 4, 8, 16)
QK_NOPE_DIM = 128
QK_ROPE_DIM = 64
V_HEAD_DIM = 128
ROPE_BASE = 10000.0
LOG2E = 1.4426950408889634

V7X_LANES = 128
V7X_VMEM_BYTES = 64 * 1024 * 1024
V7X_VMEM_CAP = 56 * 1024 * 1024
NA_ROWS_PER_TILE = 4

BF16 = jnp.bfloat16
F32 = jnp.float32


def _pick_tile(n, target, mult=16):
    if n <= target:
        return n
    for t in range(target, 0, -1):
        if n % t == 0 and t % mult == 0:
            return t
    return n


def _nbytes(shape, dtype):
    return int(np.prod(shape)) * jnp.dtype(dtype).itemsize


def _params(block_bytes, scratch_bytes=0, ndims=3):
    est = 2 * block_bytes + scratch_bytes
    limit = int(min(max(est * 5 // 4 + (4 << 20), 16 << 20), V7X_VMEM_CAP))
    return pltpu.CompilerParams(dimension_semantics=("arbitrary",) * ndims, vmem_limit_bytes=limit)


def _mod_kernel(c_ref, w_ref, b_ref, o_ref):
    cv = c_ref[...]
    s = cv * (1.0 / (1.0 + jnp.exp(-cv)))
    acc = jnp.dot(s.astype(BF16), w_ref[...].astype(BF16), preferred_element_type=F32)
    o_ref[...] = acc + b_ref[...]


def _mod_call(cvec, w_mod, b_mod):
    depth, d, n = w_mod.shape
    r = cvec.shape[0]
    tn = _pick_tile(n, 1024, V7X_LANES)
    blocks = _nbytes((r, d), F32) + _nbytes((d, tn), F32) + 2 * _nbytes((r, tn), F32)
    return pl.pallas_call(
        _mod_kernel,
        grid=(depth, n // tn),
        in_specs=[
            pl.BlockSpec((r, d), lambda l, j: (0, 0)),
            pl.BlockSpec((None, d, tn), lambda l, j: (l, 0, j)),
            pl.BlockSpec((None, 1, tn), lambda l, j: (l, 0, j)),
        ],
        out_specs=pl.BlockSpec((None, r, tn), lambda l, j: (l, 0, j)),
        out_shape=jax.ShapeDtypeStruct((depth, r, n), F32),
        compiler_params=_params(blocks, _nbytes((d, tn), BF16), 2),
        name="mod_vectors",
    )(cvec, w_mod, b_mod.reshape(depth, 1, n))


def _norm_mod_kernel(x_ref, g_ref, sh_ref, sc_ref, o_ref):
    xf = x_ref[...]
    col = g_ref[...] * (1.0 + sc_ref[...])
    y = xf * lax.rsqrt(jnp.mean(xf * xf, axis=-1, keepdims=True) + EPS)
    o_ref[...] = (y * col + sh_ref[...]).astype(o_ref.dtype)


def _norm_kernel(x_ref, g_ref, o_ref):
    xf = x_ref[...]
    y = xf * lax.rsqrt(jnp.mean(xf * xf, axis=-1, keepdims=True) + EPS) * g_ref[...]
    o_ref[...] = y.astype(o_ref.dtype)


def _norm_call(x, g, shift=None, scale=None, out_dtype=BF16):
    b, s, d = x.shape
    ts = _pick_tile(s, 1024 if b * s >= 8192 else 256)
    xspec = pl.BlockSpec((None, ts, d), lambda bi, i: (bi, i, 0))
    gspec = pl.BlockSpec((1, d), lambda bi, i: (0, 0))
    mspec = pl.BlockSpec((None, 1, d), lambda bi, i: (bi, 0, 0))
    blocks = _nbytes((ts, d), F32) + _nbytes((ts, d), out_dtype) + 3 * _nbytes((1, d), F32)
    common = dict(
        grid=(b, s // ts),
        out_specs=xspec,
        out_shape=jax.ShapeDtypeStruct((b, s, d), out_dtype),
        compiler_params=_params(blocks, 2 * _nbytes((ts, d), F32), 2),
    )
    if shift is None:
        return pl.pallas_call(_norm_kernel, in_specs=[xspec, gspec], name="rmsnorm", **common)(
            x, g.reshape(1, d))
    return pl.pallas_call(_norm_mod_kernel, in_specs=[xspec, gspec, mspec, mspec], name="rmsnorm_modulate",
                          **common)(x, g.reshape(1, d), shift, scale)


def _mm_kernel(*refs, n_ops, n_extra, epilogue):
    a_refs = refs[:n_ops]
    w_refs = refs[n_ops:2 * n_ops]
    e_refs = refs[2 * n_ops:2 * n_ops + n_extra]
    o_refs = refs[2 * n_ops + n_extra:]
    acc = None
    for a_ref, w_ref in zip(a_refs, w_refs):
        part = jnp.dot(a_ref[...], w_ref[...], preferred_element_type=F32)
        acc = part if acc is None else acc + part
    epilogue(acc, e_refs, o_refs)


def _mm_call(a_ops, w_ops, extras, epilogue, outs, grid, tm, tn, name, tmp_bytes=0):
    arrays = [a for a, _ in a_ops] + [w for w, _ in w_ops] + [e for e, _ in extras]
    in_specs = [s for _, s in a_ops] + [s for _, s in w_ops] + [s for _, s in extras]
    blocks = 0
    for arr, spec in a_ops + w_ops + extras:
        blocks += _nbytes([d for d in spec.block_shape if d is not None], arr.dtype)
    for sds, spec in outs:
        blocks += _nbytes([d for d in spec.block_shape if d is not None], sds.dtype)
    kern = functools.partial(_mm_kernel, n_ops=len(a_ops), n_extra=len(extras), epilogue=epilogue)
    res = pl.pallas_call(
        kern,
        grid=grid,
        in_specs=in_specs,
        out_specs=[s for _, s in outs],
        out_shape=[o for o, _ in outs],
        compiler_params=_params(blocks, 3 * _nbytes((tm, tn), F32) + tmp_bytes, len(grid)),
        name=name,
    )(*arrays)
    return res


def _a_spec(tm, k, col_blk=0):
    return pl.BlockSpec((None, tm, k), lambda b, i, j: (b, i, col_blk))


def _w3(w):
    return w if w.ndim == 3 else w[None]


def _w_spec(k, tn, layer=0, row_blk=0, col_off=0):
    return pl.BlockSpec((None, k, tn), lambda b, i, j: (layer, row_blk, j + col_off))


def _tile_spec(tm, tn):
    return pl.BlockSpec((None, tm, tn), lambda b, i, j: (b, i, j))


def _vec_spec(tn):
    return pl.BlockSpec((None, 1, tn), lambda b, i, j: (b, 0, j))


def _ep_store(acc, e_refs, o_refs):
    o_refs[0][...] = acc.astype(o_refs[0].dtype)


def _col_tile(n, col0, target, mult):
    return _pick_tile(math.gcd(n, col0) if col0 else n, target, mult)


def _mm_plain(a, w, out_dtype, name, layer=0, col0=0, n=None, tm_target=1024, tn_target=512):
    b, s, k = a.shape
    w = _w3(w)
    n = w.shape[2] - col0 if n is None else n
    tm = _pick_tile(s, tm_target)
    tn = _col_tile(n, col0, tn_target, V7X_LANES)
    return _mm_call([(a, _a_spec(tm, k))], [(w, _w_spec(k, tn, layer, 0, col0 // tn))], [], _ep_store,
                    [(jax.ShapeDtypeStruct((b, s, n), out_dtype), _tile_spec(tm, tn))],
                    (b, s // tm, n // tn), tm, tn, name)[0]


def _ep_chunks(acc, e_refs, o_refs, *, width):
    o_ref = o_refs[0]
    for c in range(acc.shape[1] // width):
        o_ref[c] = acc[:, c * width:(c + 1) * width].astype(o_ref.dtype)


def _mm_chunked(a, w, name, layer=0, col0=0, n=None, width=HEAD_DIM, tm_target=1024, tn_target=512):
    b, s, k = a.shape
    w = _w3(w)
    n = w.shape[2] - col0 if n is None else n
    tm = _pick_tile(s, tm_target)
    tn = _col_tile(n, col0, tn_target, width)
    cpt = tn // width
    ospec = pl.BlockSpec((None, cpt, tm, width), lambda bi, i, j: (bi, j, i, 0))
    return _mm_call([(a, _a_spec(tm, k))], [(w, _w_spec(k, tn, layer, 0, col0 // tn))], [],
                    functools.partial(_ep_chunks, width=width),
                    [(jax.ShapeDtypeStruct((b, n // width, s, width), BF16), ospec)],
                    (b, s // tm, n // tn), tm, tn, name)[0]


def _ep_residual(acc, e_refs, o_refs):
    x_ref, g_ref = e_refs
    o_refs[0][...] = x_ref[...] + g_ref[...] * acc


def _mm_residual(a_list, w, x, gate, name, layer=0, tm_target=1024, tn_target=512):
    b, s, n = x.shape
    w = _w3(w)
    tm = _pick_tile(s, tm_target)
    tn = _pick_tile(n, tn_target, V7X_LANES)
    kblk = functools.reduce(math.gcd, [a.shape[2] for a in a_list])
    a_ops, w_ops = [], []
    for a in a_list:
        for cb in range(a.shape[2] // kblk):
            a_ops.append((a, _a_spec(tm, kblk, cb)))
            w_ops.append((w, _w_spec(kblk, tn, layer, len(w_ops))))
    extras = [(x, _tile_spec(tm, tn)), (gate, _vec_spec(tn))]
    return _mm_call(a_ops, w_ops, extras, _ep_residual,
                    [(jax.ShapeDtypeStruct((b, s, n), F32), _tile_spec(tm, tn))],
                    (b, s // tm, n // tn), tm, tn, name)[0]


def _swiglu_kernel(a_ref, wg_ref, wu_ref, o_ref, wgb_ref, wub_ref):
    @pl.when((pl.program_id(1) == 0) & (pl.program_id(2) == 0))
    def _():
        wgb_ref[...] = wg_ref[...].astype(BF16)
        wub_ref[...] = wu_ref[...].astype(BF16)

    a = a_ref[...]
    g = jnp.dot(a, wgb_ref[...], preferred_element_type=F32)
    u = jnp.dot(a, wub_ref[...], preferred_element_type=F32)
    o_ref[...] = (g * (1.0 / (1.0 + jnp.exp(-g))) * u).astype(o_ref.dtype)


def _swiglu_up(a, wg, wu, layer, name, tm_target=1024, tn_target=512):
    b, s, k = a.shape
    n = wg.shape[2]
    tm = _pick_tile(s, tm_target)
    tn = _pick_tile(n, tn_target, V7X_LANES)
    blocks = _nbytes((tm, k), BF16) + 2 * _nbytes((k, tn), F32) + _nbytes((tm, tn), BF16)
    wspec = pl.BlockSpec((None, k, tn), lambda j, bi, i: (layer, 0, j))
    return pl.pallas_call(
        _swiglu_kernel,
        grid=(n // tn, b, s // tm),
        in_specs=[pl.BlockSpec((None, tm, k), lambda j, bi, i: (bi, i, 0)), wspec, wspec],
        out_specs=pl.BlockSpec((None, tm, tn), lambda j, bi, i: (bi, i, j)),
        out_shape=jax.ShapeDtypeStruct((b, s, n), BF16),
        scratch_shapes=[pltpu.VMEM((k, tn), BF16), pltpu.VMEM((k, tn), BF16)],
        compiler_params=_params(blocks, 2 * _nbytes((k, tn), BF16) + 4 * _nbytes((tm, tn), F32), 3),
        name=name,
    )(a, wg, wu)


def _rope_pair(hi, tab):
    z = hi * tab
    return z + pltpu.roll(z, QK_ROPE_DIM, axis=1)


def _ep_qrope(acc, e_refs, o_refs, *, heads):
    tab_ref, = e_refs
    o_ref = o_refs[0]
    tab = tab_ref[...]
    blk = QK_NOPE_DIM + 2 * QK_ROPE_DIM
    for h in range(heads):
        o_ref[h, :, 0:QK_NOPE_DIM] = acc[:, h * blk:h * blk + QK_NOPE_DIM].astype(o_ref.dtype)
        hi = acc[:, h * blk + QK_NOPE_DIM:(h + 1) * blk]
        o_ref[h, :, QK_NOPE_DIM:blk] = _rope_pair(hi, tab).astype(o_ref.dtype)


def _mm_qrope(a, w, tab, name, tm_target=1024, heads_per_tile=6):
    b, s, k = a.shape
    blk = QK_NOPE_DIM + 2 * QK_ROPE_DIM
    nh = w.shape[1] // blk
    hpt = math.gcd(nh, heads_per_tile)
    tn = hpt * blk
    tm = _pick_tile(s, tm_target)
    tspec = pl.BlockSpec((tm, 2 * QK_ROPE_DIM), lambda bi, i, j: (i, 0))
    ospec = pl.BlockSpec((None, hpt, tm, blk), lambda bi, i, j: (bi, j, i, 0))
    return _mm_call([(a, _a_spec(tm, k))], [(_w3(w), _w_spec(k, tn))], [(tab, tspec)],
                    functools.partial(_ep_qrope, heads=hpt),
                    [(jax.ShapeDtypeStruct((b, nh, s, blk), BF16), ospec)],
                    (b, s // tm, nh // hpt), tm, tn, name)[0]


def _ep_kv(acc, e_refs, o_refs, *, heads):
    kr_ref, = e_refs
    k_ref, v_ref = o_refs
    kr = kr_ref[...]
    blk = QK_NOPE_DIM + V_HEAD_DIM
    for h in range(heads):
        k_ref[h, :, 0:QK_NOPE_DIM] = acc[:, h * blk:h * blk + QK_NOPE_DIM].astype(k_ref.dtype)
        k_ref[h, :, QK_NOPE_DIM:QK_NOPE_DIM + kr.shape[1]] = kr
        v_ref[h] = acc[:, h * blk + QK_NOPE_DIM:(h + 1) * blk].astype(v_ref.dtype)


def _mm_kv(a, w, krot, name, tm_target=1024, heads_per_tile=6):
    b, s, k = a.shape
    blk = QK_NOPE_DIM + V_HEAD_DIM
    nh = w.shape[1] // blk
    hpt = math.gcd(nh, heads_per_tile)
    tn = hpt * blk
    tm = _pick_tile(s, tm_target)
    kd = QK_NOPE_DIM + krot.shape[2]
    krspec = pl.BlockSpec((None, tm, krot.shape[2]), lambda bi, i, j: (bi, i, 0))
    kspec = pl.BlockSpec((None, hpt, tm, kd), lambda bi, i, j: (bi, j, i, 0))
    vspec = pl.BlockSpec((None, hpt, tm, V_HEAD_DIM), lambda bi, i, j: (bi, j, i, 0))
    return _mm_call([(a, _a_spec(tm, k))], [(_w3(w), _w_spec(k, tn))], [(krot, krspec)],
                    functools.partial(_ep_kv, heads=hpt),
                    [(jax.ShapeDtypeStruct((b, nh, s, kd), BF16), kspec),
                     (jax.ShapeDtypeStruct((b, nh, s, V_HEAD_DIM), BF16), vspec)],
                    (b, s // tm, nh // hpt), tm, tn, name)


def _dft_matrix(n):
    m = jnp.arange(n, dtype=jnp.int32)
    side = 1
    while side * side < n:
        side *= 2
    if side * side != n:
        k = jnp.arange(n, dtype=jnp.int32)
        ang = (2.0 * math.pi / n) * ((k[:, None] * m[None, :]) % n).astype(F32)
        return jnp.concatenate([jnp.cos(ang), jnp.sin(ang)], axis=0).astype(BF16)[None]
    a = jnp.arange(side, dtype=jnp.int32)
    ang_x = (2.0 * math.pi / side) * ((a[:, None] * m[None, :]) % side).astype(F32)
    ang_y = (2.0 * math.pi / n) * ((a[:, None] * m[None, :]) % n).astype(F32)
    cx, sx = jnp.cos(ang_x), jnp.sin(ang_x)
    cy, sy = jnp.cos(ang_y)[None], jnp.sin(ang_y)[None]
    p = jnp.concatenate([cx, sx], axis=0)[:, None, :]
    q = jnp.concatenate([sx, -cx], axis=0)[:, None, :]
    return (p * cy - q * sy).astype(BF16).reshape(1, 2 * n, n)


def _dft_positions_dense(u):
    b, s, w = u.shape
    mat = _dft_matrix(s)
    tm = _pick_tile(2 * s, 1024)
    tn = _pick_tile(w, 512, V7X_LANES)
    aspec = pl.BlockSpec((None, tm, s), lambda bi, i, j: (0, i, 0))
    wspec = pl.BlockSpec((None, s, tn), lambda bi, i, j: (bi, 0, j))
    return _mm_call([(mat, aspec)], [(u, wspec)], [], _ep_store,
                    [(jax.ShapeDtypeStruct((b, 2 * s, w), BF16), _tile_spec(tm, tn))],
                    (b, 2 * s // tm, w // tn), tm, tn, "fourier_positions")[0]


DFT_ROWS_PER_STEP = 16


def _dft_stage1_kernel(x_ref, a_ref, tc_ref, ts_ref, o_ref, *, side, width, n2_per_step):
    g = jnp.dot(a_ref[...], x_ref[...], preferred_element_type=F32)
    gr, gs = g[0:side], g[side:2 * side]
    for n2 in range(n2_per_step):
        tcs = tc_ref[:, n2 * V7X_LANES:(n2 + 1) * V7X_LANES]
        tss = ts_ref[:, n2 * V7X_LANES:(n2 + 1) * V7X_LANES]
        for c0 in range(n2 * width, (n2 + 1) * width, V7X_LANES):
            cols = slice(c0, c0 + V7X_LANES)
            o_ref[0:side, cols] = (gr[:, cols] * tcs - gs[:, cols] * tss).astype(o_ref.dtype)
            o_ref[side:2 * side, cols] = (gr[:, cols] * tss + gs[:, cols] * tcs).astype(o_ref.dtype)


def _dft_stage2_kernel(h_ref, m_ref, o_ref, scr_ref, *, side):
    rows = h_ref.shape[1]
    ngrp = scr_ref.shape[0]
    for j in range(rows):
        hcat = jnp.concatenate([h_ref[0, j], h_ref[1, j]], axis=0)
        res = jnp.dot(m_ref[...], hcat, preferred_element_type=F32)
        for c in range(ngrp):
            cols = slice(c * V7X_LANES, (c + 1) * V7X_LANES)
            scr_ref[c, pl.ds(j, side, stride=rows), :] = res[0:side, cols]
            scr_ref[c, pl.ds(side * rows + j, side, stride=rows), :] = res[side:2 * side, cols]
    for c in range(ngrp):
        cols = slice(c * V7X_LANES, (c + 1) * V7X_LANES)
        for p in range(2):
            part = scr_ref[c, p * side * rows:(p + 1) * side * rows, :].reshape(side, rows, V7X_LANES)
            o_ref[p, :, :, cols] = part.astype(o_ref.dtype)


def _dft_positions(u):
    b, s, w = u.shape
    side = math.isqrt(s)
    if side * side != s or side % DFT_ROWS_PER_STEP or w % V7X_LANES:
        return _dft_positions_dense(u)
    idx = np.arange(side)
    ang = 2.0 * np.pi * ((idx[:, None] * idx[None, :]) % side) / side
    cs, sn = np.cos(ang), np.sin(ang)
    a1 = jnp.asarray(np.concatenate([cs, sn], axis=0), dtype=F32).astype(BF16)
    m2 = jnp.asarray(np.block([[cs, -sn], [sn, cs]]), dtype=F32).astype(BF16)
    tw = 2.0 * np.pi * (idx[:, None] * idx[None, :]) / s
    tcos = jnp.repeat(jnp.asarray(np.cos(tw), dtype=F32), V7X_LANES, axis=1)
    tsin = jnp.repeat(jnp.asarray(np.sin(tw), dtype=F32), V7X_LANES, axis=1)
    n2s = DFT_ROWS_PER_STEP
    tc = n2s * w
    blocks1 = _nbytes((side, tc), BF16) + _nbytes((2 * side, side), BF16) + 2 * _nbytes((side, n2s * V7X_LANES), F32) \
        + _nbytes((2 * side, tc), BF16)
    h = pl.pallas_call(
        functools.partial(_dft_stage1_kernel, side=side, width=w, n2_per_step=n2s),
        grid=(b, side // n2s),
        in_specs=[
            pl.BlockSpec((None, side, tc), lambda bi, j: (bi, 0, j)),
            pl.BlockSpec((2 * side, side), lambda bi, j: (0, 0)),
            pl.BlockSpec((side, n2s * V7X_LANES), lambda bi, j: (0, j)),
            pl.BlockSpec((side, n2s * V7X_LANES), lambda bi, j: (0, j)),
        ],
        out_specs=pl.BlockSpec((None, 2 * side, tc), lambda bi, j: (bi, 0, j)),
        out_shape=jax.ShapeDtypeStruct((b, 2 * side, side * w), BF16),
        compiler_params=_params(blocks1, 3 * _nbytes((2 * side, tc), F32), 2),
        name="fourier_positions_stage1",
    )(u.reshape(b, side, side * w), a1, tcos, tsin)
    rows = DFT_ROWS_PER_STEP
    blocks2 = 2 * _nbytes((2, rows, side, w), BF16) + _nbytes((2 * side, 2 * side), BF16)
    f2 = pl.pallas_call(
        functools.partial(_dft_stage2_kernel, side=side),
        grid=(b, side // rows),
        in_specs=[
            pl.BlockSpec((None, 2, rows, side, w), lambda bi, i: (bi, 0, i, 0, 0)),
            pl.BlockSpec((2 * side, 2 * side), lambda bi, i: (0, 0)),
        ],
        out_specs=pl.BlockSpec((None, 2, side, rows, w), lambda bi, i: (bi, 0, 0, i, 0)),
        out_shape=jax.ShapeDtypeStruct((b, 2, side, side, w), BF16),
        scratch_shapes=[pltpu.VMEM((w // V7X_LANES, 2 * side * rows, V7X_LANES), F32)],
        compiler_params=_params(blocks2, _nbytes((2 * side * rows, w), F32) + 4 * _nbytes((2 * side, w), F32), 2),
        name="fourier_positions_stage2",
    )(h.reshape(b, 2, side, side, w), m2)
    return f2.reshape(b, 2 * s, w)


def _four_channel_kernel(fr_ref, fi_ref, cs_ref, w_ref, o_ref, *, norm):
    gd = w_ref.shape[1]
    for g in range(w_ref.shape[0]):
        cols = slice(g * gd, (g + 1) * gd)
        f = jnp.dot(fr_ref[:, cols].astype(BF16), cs_ref[0:gd, :], preferred_element_type=F32)
        f = f + jnp.dot(fi_ref[:, cols].astype(BF16), cs_ref[gd:2 * gd, :], preferred_element_type=F32)
        f = (f * norm).astype(BF16)
        o_ref[:, cols] = jnp.dot(f, w_ref[g], preferred_element_type=F32).astype(o_ref.dtype)


def _four_channels(f2, w_four):
    b, s2, w = f2.shape
    s = s2 // 2
    g, gd, _ = w_four.shape
    idx = np.arange(gd)
    ang = 2.0 * np.pi * ((idx[:, None] * idx[None, :]) % gd) / gd
    cs = jnp.asarray(np.concatenate([np.cos(ang), -np.sin(ang)], axis=0), dtype=F32).astype(BF16)
    ts = _pick_tile(s, 1024)
    nblk = s // ts
    blocks = 2 * _nbytes((ts, w), f2.dtype) + _nbytes((ts, w), BF16) + _nbytes((2 * gd, gd), BF16) \
        + _nbytes((g, gd, gd), BF16)
    return pl.pallas_call(
        functools.partial(_four_channel_kernel, norm=1.0 / math.sqrt(s * gd)),
        grid=(b, nblk),
        in_specs=[
            pl.BlockSpec((None, ts, w), lambda bi, i: (bi, i, 0)),
            pl.BlockSpec((None, ts, w), lambda bi, i: (bi, nblk + i, 0)),
            pl.BlockSpec((2 * gd, gd), lambda bi, i: (0, 0)),
            pl.BlockSpec((g, gd, gd), lambda bi, i: (0, 0, 0)),
        ],
        out_specs=pl.BlockSpec((None, ts, w), lambda bi, i: (bi, i, 0)),
        out_shape=jax.ShapeDtypeStruct((b, s, w), BF16),
        compiler_params=_params(blocks, 6 * _nbytes((ts, gd), F32), 2),
        name="fourier_channels",
    )(f2, f2, cs, w_four.astype(BF16))


def _na_plan(rows):
    rpt = NA_ROWS_PER_TILE
    kh = min(NA_WIN_H, rows)
    wrows = min(kh + rpt - 1, rows)
    cases, ws_list, case_list = {}, [], []
    for t in range(rows // rpt):
        r0 = t * rpt
        ws = int(np.clip(r0 - kh // 2, 0, rows - wrows))
        plan = []
        for r in range(r0, r0 + rpt):
            rs = int(np.clip(r - kh // 2, 0, rows - kh))
            plan.append(tuple((ws + w) - r + NA_WIN_H - 1 if rs <= ws + w < rs + kh else None
                              for w in range(wrows)))
        plan = tuple(plan)
        cases.setdefault(plan, len(cases))
        ws_list.append(ws)
        case_list.append(cases[plan])
    return wrows, ws_list, case_list, list(cases)


def _na_block_table(rpb):
    qcol = np.arange(GRID_W)[:, None]
    kcol = np.arange(GRID_W)[None, :]
    cstart = np.clip(qcol - NA_WIN_W // 2, 0, GRID_W - NA_WIN_W)
    col_ok = (kcol >= cstart) & (kcol < cstart + NA_WIN_W)
    cidx = np.clip(kcol - qcol + NA_WIN_W - 1, 0, 2 * NA_WIN_W - 2)
    csel = jnp.asarray(np.eye(rpb.shape[2], dtype=np.float32)[cidx])
    blk = jnp.einsum("hrs,qks->hrqk", rpb, csel, precision=lax.Precision.HIGHEST)
    blk = jnp.where(jnp.asarray(col_ok), blk * LOG2E, NEG_INF)
    return jnp.concatenate([blk, blk], axis=-1)


def _ones_column(rows, width):
    lane = lax.broadcasted_iota(jnp.int32, (rows, width), 1)
    return jnp.where(lane == 0, 1.0, 0.0).astype(BF16)


def _na_kernel(q_ref, k_ref, v_ref, kc_ref, vc_ref, blk_ref, o_ref, bias_ref, vx_ref, vcx_ref,
               *, tiles, tq, tk, plans):
    d = v_ref.shape[1]

    @pl.when(pl.program_id(1) == 0)
    def _():
        for c, plan in enumerate(plans):
            for qr, row in enumerate(plan):
                for kr, ridx in enumerate(row):
                    half = slice((kr % 2) * GRID_W, (kr % 2 + 1) * GRID_W)
                    if ridx is None:
                        blk = jnp.full((GRID_W, GRID_W), NEG_INF, F32)
                    else:
                        blk = blk_ref[ridx][:, half]
                    bias_ref[c, qr * GRID_W:(qr + 1) * GRID_W, kr * GRID_W:(kr + 1) * GRID_W] = blk

    vx_ref[:, 0:d] = v_ref[...]
    vx_ref[:, d:2 * d] = _ones_column(v_ref.shape[0], d)
    vcx_ref[:, 0:d] = vc_ref[...]
    vcx_ref[:, d:2 * d] = _ones_column(vc_ref.shape[0], d)
    kc = kc_ref[...]
    dn = (((1,), (1,)), ((), ()))

    for t, (ws, case) in enumerate(tiles):
        q0, k0 = t * tq, ws * GRID_W
        q = q_ref[q0:q0 + tq, :]
        s_loc = lax.dot_general(q, k_ref[k0:k0 + tk, :], dn, preferred_element_type=F32) + bias_ref[case]
        s_ctx = lax.dot_general(q, kc, dn, preferred_element_type=F32)
        m = jnp.maximum(jnp.max(s_loc, axis=-1, keepdims=True), jnp.max(s_ctx, axis=-1, keepdims=True))
        p_loc = jnp.exp2(s_loc - m).astype(BF16)
        p_ctx = jnp.exp2(s_ctx - m).astype(BF16)
        acc = jnp.dot(p_loc, vx_ref[k0:k0 + tk, :], preferred_element_type=F32)
        acc = acc + jnp.dot(p_ctx, vcx_ref[...], preferred_element_type=F32)
        o_ref[q0:q0 + tq, :] = (acc[:, 0:d] / acc[:, d:d + 1]).astype(o_ref.dtype)


def _na_attention(qkv, qkvc, rpb):
    b, c3, s, d = qkv.shape
    nh = c3 // 3
    lc = qkvc.shape[2] // b
    rows = s // GRID_W
    wrows, ws_list, case_list, plans = _na_plan(rows)
    blk = _na_block_table(rpb)
    nrel = blk.shape[1]
    tq = NA_ROWS_PER_TILE * GRID_W
    tk = wrows * GRID_W
    blocks = 3 * _nbytes((s, d), BF16) + 2 * _nbytes((lc, d), BF16) + _nbytes((nrel, GRID_W, 2 * GRID_W), F32) \
        + _nbytes((s, d), BF16)
    scratch = _nbytes((len(plans), tq, tk), F32) + _nbytes((s + lc, 2 * d), BF16)
    return pl.pallas_call(
        functools.partial(_na_kernel, tiles=list(zip(ws_list, case_list)), tq=tq, tk=tk, plans=plans),
        grid=(nh, b),
        in_specs=[
            pl.BlockSpec((None, None, s, d), lambda h, bi: (bi, h, 0, 0)),
            pl.BlockSpec((None, None, s, d), lambda h, bi: (bi, nh + h, 0, 0)),
            pl.BlockSpec((None, None, s, d), lambda h, bi: (bi, 2 * nh + h, 0, 0)),
            pl.BlockSpec((None, None, lc, d), lambda h, bi: (0, nh + h, bi, 0)),
            pl.BlockSpec((None, None, lc, d), lambda h, bi: (0, 2 * nh + h, bi, 0)),
            pl.BlockSpec((None, nrel, GRID_W, 2 * GRID_W), lambda h, bi: (h, 0, 0, 0)),
        ],
        out_specs=pl.BlockSpec((None, s, d), lambda h, bi: (bi, 0, h)),
        out_shape=jax.ShapeDtypeStruct((b, s, nh * d), BF16),
        scratch_shapes=[pltpu.VMEM((len(plans), tq, tk), F32), pltpu.VMEM((s, 2 * d), BF16),
                        pltpu.VMEM((lc, 2 * d), BF16)],
        compiler_params=_params(blocks, scratch + 8 * _nbytes((tq, tk + lc), F32), 2),
        name="neighbourhood_attention",
    )(qkv, qkv, qkv, qkvc, qkvc, blk)


def _dense_attn_kernel(*refs, n_src, chunks, heads):
    q_ref = refs[0]
    k_refs = refs[1:1 + n_src]
    v_refs = refs[1 + n_src:1 + 2 * n_src]
    o_ref = refs[1 + 2 * n_src]
    vx_refs = refs[2 + 2 * n_src:]
    dv = v_refs[0].shape[2]

    @pl.when(pl.program_id(2) == 0)
    def _():
        for v_ref, vx_ref in zip(v_refs, vx_refs):
            for hh in range(heads):
                vx_ref[hh, :, 0:dv] = v_ref[hh]
                vx_ref[hh, :, dv:2 * dv] = _ones_column(v_ref.shape[1], dv)

    for hh in range(heads):
        q = q_ref[hh]
        m = acc = None
        for src, start, size in chunks:
            s = lax.dot_general(q, k_refs[src][hh, start:start + size, :], (((1,), (1,)), ((), ())),
                                preferred_element_type=F32)
            m_chunk = jnp.max(s, axis=-1, keepdims=True)
            m_new = m_chunk if m is None else jnp.maximum(m, m_chunk)
            p = jnp.exp2(s - m_new)
            pv = jnp.dot(p.astype(BF16), vx_refs[src][hh, start:start + size, :], preferred_element_type=F32)
            acc = pv if acc is None else acc * jnp.exp2(m - m_new) + pv
            m = m_new
        o_ref[:, hh * dv:(hh + 1) * dv] = (acc[:, 0:dv] / acc[:, dv:dv + 1]).astype(o_ref.dtype)


def _head_major_spec(arr, batch, rows, off, tiled, heads):
    per_batch = arr.shape[2] * arr.shape[0] // batch // rows
    hb = off // heads
    if arr.shape[0] == 1:
        return pl.BlockSpec((None, heads, rows, arr.shape[3]),
                            lambda bi, h, i: (0, hb + h, bi * per_batch + (i if tiled else 0), 0))
    return pl.BlockSpec((None, heads, rows, arr.shape[3]), lambda bi, h, i: (bi, hb + h, i if tiled else 0, 0))


def _dense_attention(q, kv_list, nh, q_off, name, batch, tq_target=1024, chunk_target=512, heads=2):
    dk = q.shape[3]
    sq = q.shape[0] * q.shape[2] // batch
    dv = kv_list[0][1].shape[3]
    offsets = [q_off] + [o for _, _, ko, vo in kv_list for o in (ko, vo)]
    heads = functools.reduce(math.gcd, [nh] + offsets, heads)
    tq = _pick_tile(sq, tq_target)
    nq = sq // tq
    chunks, arrays_k, arrays_v, scratch = [], [], [], []
    blocks = heads * (_nbytes((tq, dk), BF16) + _nbytes((tq, dv), BF16))
    tmp = 4 * _nbytes((tq, chunk_target), F32) + 2 * _nbytes((tq, 2 * dv), F32)
    kspecs, vspecs = [], []
    for src, (k, v, k_off, v_off) in enumerate(kv_list):
        sk = k.shape[0] * k.shape[2] // batch
        chunks += [(src, st, min(chunk_target, sk - st)) for st in range(0, sk, chunk_target)]
        kspecs.append(_head_major_spec(k, batch, sk, k_off, False, heads))
        vspecs.append(_head_major_spec(v, batch, sk, v_off, False, heads))
        arrays_k.append(k)
        arrays_v.append(v)
        scratch.append(pltpu.VMEM((heads, sk, 2 * dv), BF16))
        blocks += heads * (_nbytes((sk, dk), BF16) + _nbytes((sk, dv), BF16))
        tmp += heads * _nbytes((sk, 2 * dv), BF16)
    if q.shape[0] == 1:
        out_spec = pl.BlockSpec((None, tq, heads * dv), lambda bi, h, i: (0, bi * nq + i, h))
    else:
        out_spec = pl.BlockSpec((None, tq, heads * dv), lambda bi, h, i: (bi, i, h))
    return pl.pallas_call(
        functools.partial(_dense_attn_kernel, n_src=len(kv_list), chunks=chunks, heads=heads),
        grid=(batch, nh // heads, nq),
        in_specs=[_head_major_spec(q, batch, tq, q_off, True, heads)] + kspecs + vspecs,
        out_specs=out_spec,
        out_shape=jax.ShapeDtypeStruct((q.shape[0], q.shape[2], nh * dv), BF16),
        scratch_shapes=scratch,
        compiler_params=_params(blocks, tmp, 3),
        name=name,
    )(q, *arrays_k, *arrays_v)


def _pool_kernel(u_ref, w_ref, sc_ref, o_ref, *, windows):
    n, gd = u_ref.shape[0], w_ref.shape[1]
    t = lax.broadcasted_iota(jnp.int32, (n, gd), 0)
    for g, window in enumerate(windows):
        cols = slice(g * gd, (g + 1) * gd)
        u = u_ref[:, cols]
        half = window // 2
        ssum = u
        for j in range(1, half + 1):
            ssum = ssum + jnp.where(t >= j, pltpu.roll(u, j, axis=0), 0.0)
        for j in range(1, half):
            ssum = ssum + jnp.where(t < n - j, pltpu.roll(u, n - j, axis=0), 0.0)
        cnt = jnp.minimum(t + half, n) - jnp.maximum(t - half, 0)
        p = ssum / cnt.astype(F32) - u
        y = jnp.dot(p.astype(BF16), w_ref[g], preferred_element_type=F32)
        o_ref[:, cols] = (y * sc_ref[:, cols]).astype(o_ref.dtype)


def _pool_mix(u, w_pool, pool_scale):
    b, s, w = u.shape
    g, gd, _ = w_pool.shape
    blocks = _nbytes((s, w), F32) + _nbytes((g, gd, gd), BF16) + _nbytes((s, w), BF16)
    return pl.pallas_call(
        functools.partial(_pool_kernel, windows=POOL_WINDOWS[:g]),
        grid=(b,),
        in_specs=[
            pl.BlockSpec((None, s, w), lambda bi: (bi, 0, 0)),
            pl.BlockSpec((g, gd, gd), lambda bi: (0, 0, 0)),
            pl.BlockSpec((1, w), lambda bi: (0, 0)),
        ],
        out_specs=pl.BlockSpec((None, s, w), lambda bi: (bi, 0, 0)),
        out_shape=jax.ShapeDtypeStruct((b, s, w), BF16),
        compiler_params=_params(blocks, 6 * _nbytes((s, gd), F32), 1),
        name="pool_mix",
    )(u, w_pool.astype(BF16), pool_scale.reshape(1, w))


def _rope_table(n):
    half = QK_ROPE_DIM // 2
    nf = half // 2
    t = np.arange(n)
    inv = ROPE_BASE ** (-np.arange(nf, dtype=np.float64) / nf)
    ar = (t // GRID_W)[:, None] * inv[None, :]
    ac = (t % GRID_W)[:, None] * inv[None, :]
    cr, sr, cc, sc = np.cos(ar), np.sin(ar), np.cos(ac), np.sin(ac)
    return jnp.asarray(np.concatenate([cr, cr, cc, cc, -sr, sr, -sc, sc], axis=1), dtype=F32)


def _identity_rope_table(n):
    return jnp.concatenate([jnp.ones((n, QK_ROPE_DIM), F32), jnp.zeros((n, QK_ROPE_DIM), F32)], axis=1)


def _swap_rope_cols(w):
    nf = QK_ROPE_DIM // 4
    lead = w.shape[:-1]
    return jnp.flip(w.reshape(lead + (2, 2, nf)), axis=-2).reshape(lead + (QK_ROPE_DIM,))


def _even_mixer(h, hc, w_in, layer, w_four, rpb, with_ctx_out):
    fw = w_four.shape[0] * w_four.shape[1]
    nh = rpb.shape[0]
    u = _mm_plain(h, w_in, BF16, "even_in_fourier", layer, 0, fw, tm_target=2048)
    qkv = _mm_chunked(h, w_in, "even_in_qkv", layer, fw, tm_target=2048)
    qkvc = _mm_chunked(hc, w_in, "even_in_qkv_ctx", layer, fw)
    y_f = _four_channels(_dft_positions(u), w_four)
    y_a = _na_attention(qkv, qkvc, rpb)
    ctx_ops = None
    if with_ctx_out:
        b = h.shape[0]
        uc = _mm_plain(hc, w_in, BF16, "even_in_fourier_ctx", layer, 0, fw).reshape(b, -1, fw)
        yc_f = _four_channels(_dft_positions(uc), w_four).reshape(1, -1, fw)
        yc_a = _dense_attention(qkvc, [(qkvc, qkvc, nh, 2 * nh)], nh, 0, "ctx_attention_even", b)
        ctx_ops = [yc_f, yc_a]
    return [y_f, y_a], ctx_ops


def _ep_odd_in(acc, e_refs, o_refs, *, splits):
    gq_ref, gkv_ref, tab_ref = e_refs
    u_ref, cq_ref, ckv_ref, kr_ref = o_refs
    o_q, o_kv, o_kr = splits

    def rms(v, g_ref):
        return (v * lax.rsqrt(jnp.mean(v * v, axis=-1, keepdims=True) + EPS) * g_ref[...]).astype(BF16)

    u_ref[...] = acc[:, 0:o_q]
    cq_ref[...] = rms(acc[:, o_q:o_kv], gq_ref)
    ckv_ref[...] = rms(acc[:, o_kv:o_kr], gkv_ref)
    r = _rope_pair(acc[:, o_kr:], tab_ref[...])
    lane = lax.broadcasted_iota(jnp.int32, r.shape, 1)
    kr_ref[...] = jnp.where(lane < QK_ROPE_DIM, r, 0.0).astype(kr_ref.dtype)


def _mm_odd_in(a, w, g_q, g_kv, tab, splits, name, tm_target=1024):
    b, s, k = a.shape
    n = w.shape[1]
    o_q, o_kv, o_kr = splits
    tm = _pick_tile(s, tm_target)

    def row(width):
        return pl.BlockSpec((None, tm, width), lambda bi, i, j: (bi, i, 0))

    def vec(width):
        return pl.BlockSpec((1, width), lambda bi, i, j: (0, 0))

    extras = [(g_q.reshape(1, -1), vec(o_kv - o_q)), (g_kv.reshape(1, -1), vec(o_kr - o_kv)),
              (tab, pl.BlockSpec((tm, n - o_kr), lambda bi, i, j: (i, 0)))]
    outs = [(jax.ShapeDtypeStruct((b, s, o_q), F32), row(o_q)),
            (jax.ShapeDtypeStruct((b, s, o_kv - o_q), BF16), row(o_kv - o_q)),
            (jax.ShapeDtypeStruct((b, s, o_kr - o_kv), BF16), row(o_kr - o_kv)),
            (jax.ShapeDtypeStruct((b, s, n - o_kr), BF16), row(n - o_kr))]
    return _mm_call([(a, _a_spec(tm, k))], [(_w3(w), _w_spec(k, n))], extras,
                    functools.partial(_ep_odd_in, splits=splits), outs, (b, s // tm, 1), tm, n, name)


def _odd_mixer(h, hc, w_in, w_pool, pool_scale, g_q, g_kv, w_uq, w_ukv, with_ctx_out):
    b, s, d = h.shape
    lc = hc.shape[1] // b
    pw = w_pool.shape[0] * w_pool.shape[1]
    q_rank, kv_rank = g_q.shape[0], g_kv.shape[0]
    nh = w_uq.shape[1] // (QK_NOPE_DIM + QK_ROPE_DIM)
    splits = (pw, pw + q_rank, pw + q_rank + kv_rank)
    w_in = w_in.astype(BF16)
    w_in2 = jnp.concatenate([w_in, _swap_rope_cols(w_in[:, splits[2]:])], axis=1)
    q_scale = (QK_NOPE_DIM + QK_ROPE_DIM) ** -0.5 * LOG2E
    wq = (w_uq * q_scale).astype(BF16).reshape(q_rank, nh, QK_NOPE_DIM + QK_ROPE_DIM)
    wq_r = wq[..., QK_NOPE_DIM:]
    wq2 = jnp.concatenate([wq[..., :QK_NOPE_DIM], wq_r, _swap_rope_cols(wq_r)], axis=-1).reshape(q_rank, -1)
    w_ukv = w_ukv.astype(BF16)
    tab, tab_c = _rope_table(s), _identity_rope_table(b * lc)

    u, cq_n, ckv_n, k_rot = _mm_odd_in(h, w_in2, g_q, g_kv, tab, splits, "odd_in")
    uc, cqc_n, ckvc_n, k_rot_c = _mm_odd_in(hc, w_in2, g_q, g_kv, tab_c, splits, "odd_in_ctx")
    k_l, v_l = _mm_kv(ckv_n, w_ukv, k_rot, "odd_up_kv")
    k_c, v_c = _mm_kv(ckvc_n, w_ukv, k_rot_c, "odd_up_kv_ctx")
    q = _mm_qrope(cq_n, wq2, tab, "odd_up_q")
    attn = _dense_attention(q, [(k_l, v_l, 0, 0), (k_c, v_c, 0, 0)], nh, 0, "mla_attention", b)
    pooled = _pool_mix(u, w_pool, pool_scale)
    ctx_ops = None
    if with_ctx_out:
        qc = _mm_qrope(cqc_n, wq2, tab_c, "odd_up_q_ctx")
        attn_c = _dense_attention(qc, [(k_c, v_c, 0, 0)], nh, 0, "ctx_attention_odd", b)
        pooled_c = _pool_mix(uc.reshape(b, lc, pw), w_pool, pool_scale).reshape(1, b * lc, pw)
        ctx_ops = [pooled_c, attn_c]
    return [pooled, attn], ctx_ops


def _ffn(xs, norm_g, shift, scale, gate, wg, wu, wd, layer, tag):
    h2 = _norm_call(xs, norm_g, shift, scale)
    act = _swiglu_up(h2, wg, wu, layer, "ffn_up_" + tag)
    return _mm_residual([act], wd, xs, gate, "ffn_down_" + tag, layer)


def kernel(x, c, ctx, c_ctx, w_mod, b_mod, norm1_g, norm2_g, w_in_ab, w_four, na_rpb, w_out_ab, w_in_cd, w_pool,
           pool_scale, mla_gq, mla_gkv, w_uq, w_ukv, w_out_cd, w_ffn_gate, w_ffn_up, w_ffn_down, final_g):
    b, s, d = x.shape
    depth = w_mod.shape[0]
    nrow = -(-(b + 1) // 8) * 8
    cvec = jnp.concatenate([c, c_ctx[None, :], jnp.zeros((nrow - b - 1, d), F32)], axis=0)
    mods = _mod_call(cvec, w_mod, b_mod)
    fw, na_width = w_four.shape[1] * w_four.shape[2], na_rpb.shape[1] * HEAD_DIM
    col_scale = np.ones((w_in_ab.shape[2],), np.float32)
    col_scale[fw:fw + na_width] = HEAD_DIM ** -0.5 * LOG2E
    w_in_ab = (w_in_ab * jnp.asarray(col_scale)).astype(BF16)
    w_out_ab, w_out_cd = w_out_ab.astype(BF16), w_out_cd.astype(BF16)
    wg, wu, wd = w_ffn_gate, w_ffn_up, w_ffn_down.astype(BF16)
    xc = ctx.reshape(1, -1, d)
    for l in range(depth):
        last = l == depth - 1
        mod = mods[l, :b].reshape(b, 1, 6 * d)
        modc = mods[l, b].reshape(1, 1, 6 * d)
        sh1, sc1, g1, sh2, sc2, g2 = jnp.split(mod, 6, axis=-1)
        csh1, csc1, cg1, csh2, csc2, cg2 = jnp.split(modc, 6, axis=-1)
        h = _norm_call(x, norm1_g[l], sh1, sc1)
        hc = _norm_call(xc, norm1_g[l], csh1, csc1)
        i = l // 2
        if l % 2 == 0:
            ops, ctx_ops = _even_mixer(h, hc, w_in_ab, i, w_four[i], na_rpb[i], not last)
            w_out = w_out_ab
        else:
            ops, ctx_ops = _odd_mixer(h, hc, w_in_cd[i], w_pool[i], pool_scale[i], mla_gq[i], mla_gkv[i],
                                      w_uq[i], w_ukv[i], not last)
            w_out = w_out_cd
        x = _mm_residual(ops, w_out, x, g1, f"out_proj_{l}", i, tm_target=512, tn_target=d)
        x = _ffn(x, norm2_g[l], sh2, sc2, g2, wg, wu, wd, l, f"{l}")
        if not last:
            xc = _mm_residual(ctx_ops, w_out, xc, cg1, f"out_proj_ctx_{l}", i)
            xc = _ffn(xc, norm2_g[l], csh2, csc2, cg2, wg, wu, wd, l, f"ctx_{l}")
    return _norm_call(x, final_g, out_dtype=F32)
```

```python
import functools
import math

import numpy as np
import jax
import jax.numpy as jnp
from jax import lax
from jax.experimental import pallas as pl
from jax.experimental.pallas import tpu as pltpu

GRID_W = 64
HEAD_DIM = 128
EPS = 1e-6
NEG_INF = -1e30
NA_WIN_H = 8
NA_WIN_W = 16
POOL_WINDOWS = (2, 4, 8, 16)
QK_NOPE_DIM = 128
QK_ROPE_DIM = 64
V_HEAD_DIM = 128
ROPE_BASE = 10000.0
LOG2E = 1.4426950408889634

V7X_LANES = 128
V7X_VMEM_BYTES = 64 * 1024 * 1024
V7X_VMEM_CAP = 56 * 1024 * 1024
NORM_INPUT_BUFFERS = 3
NA_ROWS_PER_TILE = 4

BF16 = jnp.bfloat16
F32 = jnp.float32


def _pick_tile(n, target, mult=16):
    if n <= target:
        return n
    for t in range(target, 0, -1):
        if n % t == 0 and t % mult == 0:
            return t
    return n


def _nbytes(shape, dtype):
    return int(np.prod(shape)) * jnp.dtype(dtype).itemsize


def _params(block_bytes, scratch_bytes=0, ndims=3):
    est = 2 * block_bytes + scratch_bytes
    limit = int(min(max(est * 5 // 4 + (4 << 20), 16 << 20), V7X_VMEM_CAP))
    return pltpu.CompilerParams(dimension_semantics=("arbitrary",) * ndims, vmem_limit_bytes=limit)


def _mod_kernel(c_ref, w_ref, b_ref, o_ref):
    cv = c_ref[...]
    s = cv * (1.0 / (1.0 + jnp.exp(-cv)))
    acc = jnp.dot(s.astype(BF16), w_ref[...].astype(BF16), preferred_element_type=F32)
    o_ref[...] = acc + b_ref[...]


def _mod_call(cvec, w_mod, b_mod):
    depth, d, n = w_mod.shape
    r = cvec.shape[0]
    tn = _pick_tile(n, 1024, V7X_LANES)
    blocks = _nbytes((r, d), F32) + _nbytes((d, tn), F32) + 2 * _nbytes((r, tn), F32)
    return pl.pallas_call(
        _mod_kernel,
        grid=(depth, n // tn),
        in_specs=[
            pl.BlockSpec((r, d), lambda l, j: (0, 0)),
            pl.BlockSpec((None, d, tn), lambda l, j: (l, 0, j)),
            pl.BlockSpec((None, 1, tn), lambda l, j: (l, 0, j)),
        ],
        out_specs=pl.BlockSpec((None, r, tn), lambda l, j: (l, 0, j)),
        out_shape=jax.ShapeDtypeStruct((depth, r, n), F32),
        compiler_params=_params(blocks, _nbytes((d, tn), BF16), 2),
        name="mod_vectors",
    )(cvec, w_mod, b_mod.reshape(depth, 1, n))


NORM_OUTPUT_BUFFERS = 2


def _norm_stream_kernel(x_hbm, g_ref, sh_ref, sc_ref, o_hbm, xbuf, obuf, in_sem, out_sem, *, tiles_per_batch, ts,
                        total, modulate):
    def in_copy(step, slot):
        bi, row = step // tiles_per_batch, (step % tiles_per_batch) * ts
        return pltpu.make_async_copy(x_hbm.at[bi, pl.ds(row, ts), :], xbuf.at[slot], in_sem.at[slot])

    def out_copy(step, slot):
        bi, row = step // tiles_per_batch, (step % tiles_per_batch) * ts
        return pltpu.make_async_copy(obuf.at[slot], o_hbm.at[bi, pl.ds(row, ts), :], out_sem.at[slot])

    for step in range(min(NORM_INPUT_BUFFERS, total)):
        in_copy(step, step).start()

    def body(step, carry):
        slot, oslot = step % NORM_INPUT_BUFFERS, step % NORM_OUTPUT_BUFFERS
        in_copy(step, slot).wait()

        @pl.when(step >= NORM_OUTPUT_BUFFERS)
        def _():
            out_copy(step - NORM_OUTPUT_BUFFERS, oslot).wait()

        xf = xbuf[slot]
        y = xf * lax.rsqrt(jnp.mean(xf * xf, axis=-1, keepdims=True) + EPS)
        if modulate:
            bi = step // tiles_per_batch
            col = g_ref[...] * (1.0 + sc_ref[bi])
            obuf[oslot] = (y * col + sh_ref[bi]).astype(obuf.dtype)
        else:
            obuf[oslot] = (y * g_ref[...]).astype(obuf.dtype)
        out_copy(step, oslot).start()

        @pl.when(step + NORM_INPUT_BUFFERS < total)
        def _():
            in_copy(step + NORM_INPUT_BUFFERS, slot).start()

        return carry

    lax.fori_loop(0, total, body, 0)
    for step in range(max(total - NORM_OUTPUT_BUFFERS, 0), total):
        out_copy(step, step % NORM_OUTPUT_BUFFERS).wait()


def _norm_call(x, g, shift=None, scale=None, out_dtype=BF16):
    b, s, d = x.shape
    ts = _pick_tile(s, 1024 if b * s >= 8192 else 256)
    modulate = shift is not None
    if not modulate:
        shift = scale = jnp.zeros((b, 1, d), F32)
    whole = pl.BlockSpec(memory_space=pltpu.VMEM)
    ring = NORM_INPUT_BUFFERS * _nbytes((ts, d), F32) + NORM_OUTPUT_BUFFERS * _nbytes((ts, d), out_dtype)
    limit = int(min(ring + 3 * _nbytes((ts, d), F32) + (4 << 20), V7X_VMEM_CAP))
    return pl.pallas_call(
        functools.partial(_norm_stream_kernel, tiles_per_batch=s // ts, ts=ts, total=b * (s // ts),
                          modulate=modulate),
        in_specs=[pl.BlockSpec(memory_space=pl.ANY), whole, whole, whole],
        out_specs=pl.BlockSpec(memory_space=pl.ANY),
        out_shape=jax.ShapeDtypeStruct((b, s, d), out_dtype),
        scratch_shapes=[pltpu.VMEM((NORM_INPUT_BUFFERS, ts, d), F32), pltpu.VMEM((NORM_OUTPUT_BUFFERS, ts, d), out_dtype),
                        pltpu.SemaphoreType.DMA((NORM_INPUT_BUFFERS,)), pltpu.SemaphoreType.DMA((NORM_OUTPUT_BUFFERS,))],
        compiler_params=pltpu.CompilerParams(vmem_limit_bytes=limit),
        name="rmsnorm_modulate" if modulate else "rmsnorm",
    )(x, g.reshape(1, d), shift, scale)


def _mm_kernel(*refs, n_ops, n_extra, epilogue):
    a_refs = refs[:n_ops]
    w_refs = refs[n_ops:2 * n_ops]
    e_refs = refs[2 * n_ops:2 * n_ops + n_extra]
    o_refs = refs[2 * n_ops + n_extra:]
    acc = None
    for a_ref, w_ref in zip(a_refs, w_refs):
        part = jnp.dot(a_ref[...], w_ref[...], preferred_element_type=F32)
        acc = part if acc is None else acc + part
    epilogue(acc, e_refs, o_refs)


def _mm_call(a_ops, w_ops, extras, epilogue, outs, grid, tm, tn, name, tmp_bytes=0):
    arrays = [a for a, _ in a_ops] + [w for w, _ in w_ops] + [e for e, _ in extras]
    in_specs = [s for _, s in a_ops] + [s for _, s in w_ops] + [s for _, s in extras]
    blocks = 0
    for arr, spec in a_ops + w_ops + extras:
        blocks += _nbytes([d for d in spec.block_shape if d is not None], arr.dtype)
    for sds, spec in outs:
        blocks += _nbytes([d for d in spec.block_shape if d is not None], sds.dtype)
    kern = functools.partial(_mm_kernel, n_ops=len(a_ops), n_extra=len(extras), epilogue=epilogue)
    res = pl.pallas_call(
        kern,
        grid=grid,
        in_specs=in_specs,
        out_specs=[s for _, s in outs],
        out_shape=[o for o, _ in outs],
        compiler_params=_params(blocks, 3 * _nbytes((tm, tn), F32) + tmp_bytes, len(grid)),
        name=name,
    )(*arrays)
    return res


def _a_spec(tm, k, col_blk=0):
    return pl.BlockSpec((None, tm, k), lambda b, i, j: (b, i, col_blk))


def _w3(w):
    return w if w.ndim == 3 else w[None]


def _w_spec(k, tn, layer=0, row_blk=0, col_off=0):
    return pl.BlockSpec((None, k, tn), lambda b, i, j: (layer, row_blk, j + col_off))


def _tile_spec(tm, tn):
    return pl.BlockSpec((None, tm, tn), lambda b, i, j: (b, i, j))


def _vec_spec(tn):
    return pl.BlockSpec((None, 1, tn), lambda b, i, j: (b, 0, j))


def _ep_store(acc, e_refs, o_refs):
    o_refs[0][...] = acc.astype(o_refs[0].dtype)


def _col_tile(n, col0, target, mult):
    return _pick_tile(math.gcd(n, col0) if col0 else n, target, mult)


def _mm_plain(a, w, out_dtype, name, layer=0, col0=0, n=None, tm_target=1024, tn_target=512):
    b, s, k = a.shape
    w = _w3(w)
    n = w.shape[2] - col0 if n is None else n
    tm = _pick_tile(s, tm_target)
    tn = _col_tile(n, col0, tn_target, V7X_LANES)
    return _mm_call([(a, _a_spec(tm, k))], [(w, _w_spec(k, tn, layer, 0, col0 // tn))], [], _ep_store,
                    [(jax.ShapeDtypeStruct((b, s, n), out_dtype), _tile_spec(tm, tn))],
                    (b, s // tm, n // tn), tm, tn, name)[0]


def _ep_chunks(acc, e_refs, o_refs, *, width):
    o_ref = o_refs[0]
    for c in range(acc.shape[1] // width):
        o_ref[c] = acc[:, c * width:(c + 1) * width].astype(o_ref.dtype)


def _mm_chunked(a, w, name, layer=0, col0=0, n=None, width=HEAD_DIM, tm_target=1024, tn_target=512):
    b, s, k = a.shape
    w = _w3(w)
    n = w.shape[2] - col0 if n is None else n
    tm = _pick_tile(s, tm_target)
    tn = _col_tile(n, col0, tn_target, width)
    cpt = tn // width
    ospec = pl.BlockSpec((None, cpt, tm, width), lambda bi, i, j: (bi, j, i, 0))
    return _mm_call([(a, _a_spec(tm, k))], [(w, _w_spec(k, tn, layer, 0, col0 // tn))], [],
                    functools.partial(_ep_chunks, width=width),
                    [(jax.ShapeDtypeStruct((b, n // width, s, width), BF16), ospec)],
                    (b, s // tm, n // tn), tm, tn, name)[0]


def _ep_residual(acc, e_refs, o_refs):
    x_ref, g_ref = e_refs
    o_refs[0][...] = x_ref[...] + g_ref[...] * acc


def _mm_residual(a_list, w, x, gate, name, layer=0, tm_target=1024, tn_target=512):
    b, s, n = x.shape
    w = _w3(w)
    tm = _pick_tile(s, tm_target)
    tn = _pick_tile(n, tn_target, V7X_LANES)
    kblk = functools.reduce(math.gcd, [a.shape[2] for a in a_list])
    a_ops, w_ops = [], []
    for a in a_list:
        for cb in range(a.shape[2] // kblk):
            a_ops.append((a, _a_spec(tm, kblk, cb)))
            w_ops.append((w, _w_spec(kblk, tn, layer, len(w_ops))))
    extras = [(x, _tile_spec(tm, tn)), (gate, _vec_spec(tn))]
    return _mm_call(a_ops, w_ops, extras, _ep_residual,
                    [(jax.ShapeDtypeStruct((b, s, n), F32), _tile_spec(tm, tn))],
                    (b, s // tm, n // tn), tm, tn, name)[0]


def _swiglu_kernel(a_ref, wg_ref, wu_ref, o_ref, wgb_ref, wub_ref):
    @pl.when((pl.program_id(1) == 0) & (pl.program_id(2) == 0))
    def _():
        wgb_ref[...] = wg_ref[...].astype(BF16)
        wub_ref[...] = wu_ref[...].astype(BF16)

    a = a_ref[...]
    g = jnp.dot(a, wgb_ref[...], preferred_element_type=F32)
    u = jnp.dot(a, wub_ref[...], preferred_element_type=F32)
    o_ref[...] = (g * (1.0 / (1.0 + jnp.exp(-g))) * u).astype(o_ref.dtype)


def _swiglu_up(a, wg, wu, layer, name, tm_target=1024, tn_target=512):
    b, s, k = a.shape
    n = wg.shape[2]
    tm = _pick_tile(s, tm_target)
    tn = _pick_tile(n, tn_target, V7X_LANES)
    blocks = _nbytes((tm, k), BF16) + 2 * _nbytes((k, tn), F32) + _nbytes((tm, tn), BF16)
    wspec = pl.BlockSpec((None, k, tn), lambda j, bi, i: (layer, 0, j))
    return pl.pallas_call(
        _swiglu_kernel,
        grid=(n // tn, b, s // tm),
        in_specs=[pl.BlockSpec((None, tm, k), lambda j, bi, i: (bi, i, 0)), wspec, wspec],
        out_specs=pl.BlockSpec((None, tm, tn), lambda j, bi, i: (bi, i, j)),
        out_shape=jax.ShapeDtypeStruct((b, s, n), BF16),
        scratch_shapes=[pltpu.VMEM((k, tn), BF16), pltpu.VMEM((k, tn), BF16)],
        compiler_params=_params(blocks, 2 * _nbytes((k, tn), BF16) + 4 * _nbytes((tm, tn), F32), 3),
        name=name,
    )(a, wg, wu)


def _rope_pair(hi, tab):
    z = hi * tab
    return z + pltpu.roll(z, QK_ROPE_DIM, axis=1)


def _ep_qrope(acc, e_refs, o_refs, *, heads):
    tab_ref, = e_refs
    o_ref = o_refs[0]
    tab = tab_ref[...]
    blk = QK_NOPE_DIM + 2 * QK_ROPE_DIM
    for h in range(heads):
        o_ref[h, :, 0:QK_NOPE_DIM] = acc[:, h * blk:h * blk + QK_NOPE_DIM].astype(o_ref.dtype)
        hi = acc[:, h * blk + QK_NOPE_DIM:(h + 1) * blk]
        o_ref[h, :, QK_NOPE_DIM:blk] = _rope_pair(hi, tab).astype(o_ref.dtype)


def _mm_qrope(a, w, tab, name, tm_target=1024, heads_per_tile=6):
    b, s, k = a.shape
    blk = QK_NOPE_DIM + 2 * QK_ROPE_DIM
    nh = w.shape[1] // blk
    hpt = math.gcd(nh, heads_per_tile)
    tn = hpt * blk
    tm = _pick_tile(s, tm_target)
    tspec = pl.BlockSpec((tm, 2 * QK_ROPE_DIM), lambda bi, i, j: (i, 0))
    ospec = pl.BlockSpec((None, hpt, tm, blk), lambda bi, i, j: (bi, j, i, 0))
    return _mm_call([(a, _a_spec(tm, k))], [(_w3(w), _w_spec(k, tn))], [(tab, tspec)],
                    functools.partial(_ep_qrope, heads=hpt),
                    [(jax.ShapeDtypeStruct((b, nh, s, blk), BF16), ospec)],
                    (b, s // tm, nh // hpt), tm, tn, name)[0]


def _ep_kv(acc, e_refs, o_refs, *, heads):
    kr_ref, = e_refs
    k_ref, v_ref = o_refs
    kr = kr_ref[...]
    blk = QK_NOPE_DIM + V_HEAD_DIM
    for h in range(heads):
        k_ref[h, :, 0:QK_NOPE_DIM] = acc[:, h * blk:h * blk + QK_NOPE_DIM].astype(k_ref.dtype)
        k_ref[h, :, QK_NOPE_DIM:QK_NOPE_DIM + kr.shape[1]] = kr
        v_ref[h] = acc[:, h * blk + QK_NOPE_DIM:(h + 1) * blk].astype(v_ref.dtype)


def _mm_kv(a, w, krot, name, tm_target=1024, heads_per_tile=6):
    b, s, k = a.shape
    blk = QK_NOPE_DIM + V_HEAD_DIM
    nh = w.shape[1] // blk
    hpt = math.gcd(nh, heads_per_tile)
    tn = hpt * blk
    tm = _pick_tile(s, tm_target)
    kd = QK_NOPE_DIM + krot.shape[2]
    krspec = pl.BlockSpec((None, tm, krot.shape[2]), lambda bi, i, j: (bi, i, 0))
    kspec = pl.BlockSpec((None, hpt, tm, kd), lambda bi, i, j: (bi, j, i, 0))
    vspec = pl.BlockSpec((None, hpt, tm, V_HEAD_DIM), lambda bi, i, j: (bi, j, i, 0))
    return _mm_call([(a, _a_spec(tm, k))], [(_w3(w), _w_spec(k, tn))], [(krot, krspec)],
                    functools.partial(_ep_kv, heads=hpt),
                    [(jax.ShapeDtypeStruct((b, nh, s, kd), BF16), kspec),
                     (jax.ShapeDtypeStruct((b, nh, s, V_HEAD_DIM), BF16), vspec)],
                    (b, s // tm, nh // hpt), tm, tn, name)


def _dft_matrix(n):
    m = jnp.arange(n, dtype=jnp.int32)
    side = 1
    while side * side < n:
        side *= 2
    if side * side != n:
        k = jnp.arange(n, dtype=jnp.int32)
        ang = (2.0 * math.pi / n) * ((k[:, None] * m[None, :]) % n).astype(F32)
        return jnp.concatenate([jnp.cos(ang), jnp.sin(ang)], axis=0).astype(BF16)[None]
    a = jnp.arange(side, dtype=jnp.int32)
    ang_x = (2.0 * math.pi / side) * ((a[:, None] * m[None, :]) % side).astype(F32)
    ang_y = (2.0 * math.pi / n) * ((a[:, None] * m[None, :]) % n).astype(F32)
    cx, sx = jnp.cos(ang_x), jnp.sin(ang_x)
    cy, sy = jnp.cos(ang_y)[None], jnp.sin(ang_y)[None]
    p = jnp.concatenate([cx, sx], axis=0)[:, None, :]
    q = jnp.concatenate([sx, -cx], axis=0)[:, None, :]
    return (p * cy - q * sy).astype(BF16).reshape(1, 2 * n, n)


def _dft_positions_dense(u):
    b, s, w = u.shape
    mat = _dft_matrix(s)
    tm = _pick_tile(2 * s, 1024)
    tn = _pick_tile(w, 512, V7X_LANES)
    aspec = pl.BlockSpec((None, tm, s), lambda bi, i, j: (0, i, 0))
    wspec = pl.BlockSpec((None, s, tn), lambda bi, i, j: (bi, 0, j))
    return _mm_call([(mat, aspec)], [(u, wspec)], [], _ep_store,
                    [(jax.ShapeDtypeStruct((b, 2 * s, w), BF16), _tile_spec(tm, tn))],
                    (b, 2 * s // tm, w // tn), tm, tn, "fourier_positions")[0]


DFT_ROWS_PER_STEP = 16


def _dft_stage1_kernel(x_ref, a_ref, tc_ref, ts_ref, o_ref, *, side, width, n2_per_step):
    g = jnp.dot(a_ref[...], x_ref[...], preferred_element_type=F32)
    gr, gs = g[0:side], g[side:2 * side]
    for n2 in range(n2_per_step):
        tcs = tc_ref[:, n2 * V7X_LANES:(n2 + 1) * V7X_LANES]
        tss = ts_ref[:, n2 * V7X_LANES:(n2 + 1) * V7X_LANES]
        for c0 in range(n2 * width, (n2 + 1) * width, V7X_LANES):
            cols = slice(c0, c0 + V7X_LANES)
            o_ref[0:side, cols] = (gr[:, cols] * tcs - gs[:, cols] * tss).astype(o_ref.dtype)
            o_ref[side:2 * side, cols] = (gr[:, cols] * tss + gs[:, cols] * tcs).astype(o_ref.dtype)


def _dft_stage2_kernel(h_ref, m_ref, o_ref, scr_ref, *, side):
    rows = h_ref.shape[1]
    ngrp = scr_ref.shape[0]
    for j in range(rows):
        hcat = jnp.concatenate([h_ref[0, j], h_ref[1, j]], axis=0)
        res = jnp.dot(m_ref[...], hcat, preferred_element_type=F32)
        for c in range(ngrp):
            cols = slice(c * V7X_LANES, (c + 1) * V7X_LANES)
            scr_ref[c, pl.ds(j, side, stride=rows), :] = res[0:side, cols]
            scr_ref[c, pl.ds(side * rows + j, side, stride=rows), :] = res[side:2 * side, cols]
    for c in range(ngrp):
        cols = slice(c * V7X_LANES, (c + 1) * V7X_LANES)
        for p in range(2):
            part = scr_ref[c, p * side * rows:(p + 1) * side * rows, :].reshape(side, rows, V7X_LANES)
            o_ref[p, :, :, cols] = part.astype(o_ref.dtype)


def _dft_positions(u):
    b, s, w = u.shape
    side = math.isqrt(s)
    if side * side != s or side % DFT_ROWS_PER_STEP or w % V7X_LANES:
        return _dft_positions_dense(u)
    idx = np.arange(side)
    ang = 2.0 * np.pi * ((idx[:, None] * idx[None, :]) % side) / side
    cs, sn = np.cos(ang), np.sin(ang)
    a1 = jnp.asarray(np.concatenate([cs, sn], axis=0), dtype=F32).astype(BF16)
    m2 = jnp.asarray(np.block([[cs, -sn], [sn, cs]]), dtype=F32).astype(BF16)
    tw = 2.0 * np.pi * (idx[:, None] * idx[None, :]) / s
    tcos = jnp.repeat(jnp.asarray(np.cos(tw), dtype=F32), V7X_LANES, axis=1)
    tsin = jnp.repeat(jnp.asarray(np.sin(tw), dtype=F32), V7X_LANES, axis=1)
    n2s = DFT_ROWS_PER_STEP
    tc = n2s * w
    blocks1 = _nbytes((side, tc), BF16) + _nbytes((2 * side, side), BF16) + 2 * _nbytes((side, n2s * V7X_LANES), F32) \
        + _nbytes((2 * side, tc), BF16)
    h = pl.pallas_call(
        functools.partial(_dft_stage1_kernel, side=side, width=w, n2_per_step=n2s),
        grid=(b, side // n2s),
        in_specs=[
            pl.BlockSpec((None, side, tc), lambda bi, j: (bi, 0, j)),
            pl.BlockSpec((2 * side, side), lambda bi, j: (0, 0)),
            pl.BlockSpec((side, n2s * V7X_LANES), lambda bi, j: (0, j)),
            pl.BlockSpec((side, n2s * V7X_LANES), lambda bi, j: (0, j)),
        ],
        out_specs=pl.BlockSpec((None, 2 * side, tc), lambda bi, j: (bi, 0, j)),
        out_shape=jax.ShapeDtypeStruct((b, 2 * side, side * w), BF16),
        compiler_params=_params(blocks1, 3 * _nbytes((2 * side, tc), F32), 2),
        name="fourier_positions_stage1",
    )(u.reshape(b, side, side * w), a1, tcos, tsin)
    rows = DFT_ROWS_PER_STEP
    blocks2 = 2 * _nbytes((2, rows, side, w), BF16) + _nbytes((2 * side, 2 * side), BF16)
    f2 = pl.pallas_call(
        functools.partial(_dft_stage2_kernel, side=side),
        grid=(b, side // rows),
        in_specs=[
            pl.BlockSpec((None, 2, rows, side, w), lambda bi, i: (bi, 0, i, 0, 0)),
            pl.BlockSpec((2 * side, 2 * side), lambda bi, i: (0, 0)),
        ],
        out_specs=pl.BlockSpec((None, 2, side, rows, w), lambda bi, i: (bi, 0, 0, i, 0)),
        out_shape=jax.ShapeDtypeStruct((b, 2, side, side, w), BF16),
        scratch_shapes=[pltpu.VMEM((w // V7X_LANES, 2 * side * rows, V7X_LANES), F32)],
        compiler_params=_params(blocks2, _nbytes((2 * side * rows, w), F32) + 4 * _nbytes((2 * side, w), F32), 2),
        name="fourier_positions_stage2",
    )(h.reshape(b, 2, side, side, w), m2)
    return f2.reshape(b, 2 * s, w)


def _four_channel_kernel(fr_ref, fi_ref, cs_ref, w_ref, o_ref, *, norm):
    gd = w_ref.shape[1]
    for g in range(w_ref.shape[0]):
        cols = slice(g * gd, (g + 1) * gd)
        f = jnp.dot(fr_ref[:, cols].astype(BF16), cs_ref[0:gd, :], preferred_element_type=F32)
        f = f + jnp.dot(fi_ref[:, cols].astype(BF16), cs_ref[gd:2 * gd, :], preferred_element_type=F32)
        f = (f * norm).astype(BF16)
        o_ref[:, cols] = jnp.dot(f, w_ref[g], preferred_element_type=F32).astype(o_ref.dtype)


def _four_channels(f2, w_four):
    b, s2, w = f2.shape
    s = s2 // 2
    g, gd, _ = w_four.shape
    idx = np.arange(gd)
    ang = 2.0 * np.pi * ((idx[:, None] * idx[None, :]) % gd) / gd
    cs = jnp.asarray(np.concatenate([np.cos(ang), -np.sin(ang)], axis=0), dtype=F32).astype(BF16)
    ts = _pick_tile(s, 1024)
    nblk = s // ts
    blocks = 2 * _nbytes((ts, w), f2.dtype) + _nbytes((ts, w), BF16) + _nbytes((2 * gd, gd), BF16) \
        + _nbytes((g, gd, gd), BF16)
    return pl.pallas_call(
        functools.partial(_four_channel_kernel, norm=1.0 / math.sqrt(s * gd)),
        grid=(b, nblk),
        in_specs=[
            pl.BlockSpec((None, ts, w), lambda bi, i: (bi, i, 0)),
            pl.BlockSpec((None, ts, w), lambda bi, i: (bi, nblk + i, 0)),
            pl.BlockSpec((2 * gd, gd), lambda bi, i: (0, 0)),
            pl.BlockSpec((g, gd, gd), lambda bi, i: (0, 0, 0)),
        ],
        out_specs=pl.BlockSpec((None, ts, w), lambda bi, i: (bi, i, 0)),
        out_shape=jax.ShapeDtypeStruct((b, s, w), BF16),
        compiler_params=_params(blocks, 6 * _nbytes((ts, gd), F32), 2),
        name="fourier_channels",
    )(f2, f2, cs, w_four.astype(BF16))


def _na_plan(rows):
    rpt = NA_ROWS_PER_TILE
    kh = min(NA_WIN_H, rows)
    wrows = min(kh + rpt - 1, rows)
    cases, ws_list, case_list = {}, [], []
    for t in range(rows // rpt):
        r0 = t * rpt
        ws = int(np.clip(r0 - kh // 2, 0, rows - wrows))
        plan = []
        for r in range(r0, r0 + rpt):
            rs = int(np.clip(r - kh // 2, 0, rows - kh))
            plan.append(tuple((ws + w) - r + NA_WIN_H - 1 if rs <= ws + w < rs + kh else None
                              for w in range(wrows)))
        plan = tuple(plan)
        cases.setdefault(plan, len(cases))
        ws_list.append(ws)
        case_list.append(cases[plan])
    return wrows, ws_list, case_list, list(cases)


def _na_block_table(rpb):
    qcol = np.arange(GRID_W)[:, None]
    kcol = np.arange(GRID_W)[None, :]
    cstart = np.clip(qcol - NA_WIN_W // 2, 0, GRID_W - NA_WIN_W)
    col_ok = (kcol >= cstart) & (kcol < cstart + NA_WIN_W)
    cidx = np.clip(kcol - qcol + NA_WIN_W - 1, 0, 2 * NA_WIN_W - 2)
    csel = jnp.asarray(np.eye(rpb.shape[2], dtype=np.float32)[cidx])
    blk = jnp.einsum("hrs,qks->hrqk", rpb, csel, precision=lax.Precision.HIGHEST)
    blk = jnp.where(jnp.asarray(col_ok), blk * LOG2E, NEG_INF)
    return jnp.concatenate([blk, blk], axis=-1)


def _ones_column(rows, width):
    lane = lax.broadcasted_iota(jnp.int32, (rows, width), 1)
    return jnp.where(lane == 0, 1.0, 0.0).astype(BF16)


def _na_kernel(q_ref, k_ref, v_ref, kc_ref, vc_ref, blk_ref, o_ref, bias_ref, vx_ref, vcx_ref,
               *, tiles, tq, tk, plans):
    d = v_ref.shape[1]

    @pl.when(pl.program_id(1) == 0)
    def _():
        for c, plan in enumerate(plans):
            for qr, row in enumerate(plan):
                for kr, ridx in enumerate(row):
                    half = slice((kr % 2) * GRID_W, (kr % 2 + 1) * GRID_W)
                    if ridx is None:
                        blk = jnp.full((GRID_W, GRID_W), NEG_INF, F32)
                    else:
                        blk = blk_ref[ridx][:, half]
                    bias_ref[c, qr * GRID_W:(qr + 1) * GRID_W, kr * GRID_W:(kr + 1) * GRID_W] = blk

    vx_ref[:, 0:d] = v_ref[...]
    vx_ref[:, d:2 * d] = _ones_column(v_ref.shape[0], d)
    vcx_ref[:, 0:d] = vc_ref[...]
    vcx_ref[:, d:2 * d] = _ones_column(vc_ref.shape[0], d)
    kc = kc_ref[...]
    dn = (((1,), (1,)), ((), ()))

    for t, (ws, case) in enumerate(tiles):
        q0, k0 = t * tq, ws * GRID_W
        q = q_ref[q0:q0 + tq, :]
        s_loc = lax.dot_general(q, k_ref[k0:k0 + tk, :], dn, preferred_element_type=F32) + bias_ref[case]
        s_ctx = lax.dot_general(q, kc, dn, preferred_element_type=F32)
        m = jnp.maximum(jnp.max(s_loc, axis=-1, keepdims=True), jnp.max(s_ctx, axis=-1, keepdims=True))
        p_loc = jnp.exp2(s_loc - m).astype(BF16)
        p_ctx = jnp.exp2(s_ctx - m).astype(BF16)
        acc = jnp.dot(p_loc, vx_ref[k0:k0 + tk, :], preferred_element_type=F32)
        acc = acc + jnp.dot(p_ctx, vcx_ref[...], preferred_element_type=F32)
        o_ref[q0:q0 + tq, :] = (acc[:, 0:d] / acc[:, d:d + 1]).astype(o_ref.dtype)


def _na_attention(qkv, qkvc, rpb):
    b, c3, s, d = qkv.shape
    nh = c3 // 3
    lc = qkvc.shape[2] // b
    rows = s // GRID_W
    wrows, ws_list, case_list, plans = _na_plan(rows)
    blk = _na_block_table(rpb)
    nrel = blk.shape[1]
    tq = NA_ROWS_PER_TILE * GRID_W
    tk = wrows * GRID_W
    blocks = 3 * _nbytes((s, d), BF16) + 2 * _nbytes((lc, d), BF16) + _nbytes((nrel, GRID_W, 2 * GRID_W), F32) \
        + _nbytes((s, d), BF16)
    scratch = _nbytes((len(plans), tq, tk), F32) + _nbytes((s + lc, 2 * d), BF16)
    return pl.pallas_call(
        functools.partial(_na_kernel, tiles=list(zip(ws_list, case_list)), tq=tq, tk=tk, plans=plans),
        grid=(nh, b),
        in_specs=[
            pl.BlockSpec((None, None, s, d), lambda h, bi: (bi, h, 0, 0)),
            pl.BlockSpec((None, None, s, d), lambda h, bi: (bi, nh + h, 0, 0)),
            pl.BlockSpec((None, None, s, d), lambda h, bi: (bi, 2 * nh + h, 0, 0)),
            pl.BlockSpec((None, None, lc, d), lambda h, bi: (0, nh + h, bi, 0)),
            pl.BlockSpec((None, None, lc, d), lambda h, bi: (0, 2 * nh + h, bi, 0)),
            pl.BlockSpec((None, nrel, GRID_W, 2 * GRID_W), lambda h, bi: (h, 0, 0, 0)),
        ],
        out_specs=pl.BlockSpec((None, s, d), lambda h, bi: (bi, 0, h)),
        out_shape=jax.ShapeDtypeStruct((b, s, nh * d), BF16),
        scratch_shapes=[pltpu.VMEM((len(plans), tq, tk), F32), pltpu.VMEM((s, 2 * d), BF16),
                        pltpu.VMEM((lc, 2 * d), BF16)],
        compiler_params=_params(blocks, scratch + 8 * _nbytes((tq, tk + lc), F32), 2),
        name="neighbourhood_attention",
    )(qkv, qkv, qkv, qkvc, qkvc, blk)


def _dense_attn_kernel(*refs, n_src, chunks, heads):
    q_ref = refs[0]
    k_refs = refs[1:1 + n_src]
    v_refs = refs[1 + n_src:1 + 2 * n_src]
    o_ref = refs[1 + 2 * n_src]
    vx_refs = refs[2 + 2 * n_src:]
    dv = v_refs[0].shape[2]

    @pl.when(pl.program_id(2) == 0)
    def _():
        for v_ref, vx_ref in zip(v_refs, vx_refs):
            for hh in range(heads):
                vx_ref[hh, :, 0:dv] = v_ref[hh]
                vx_ref[hh, :, dv:2 * dv] = _ones_column(v_ref.shape[1], dv)

    for hh in range(heads):
        q = q_ref[hh]
        m = acc = None
        for src, start, size in chunks:
            s = lax.dot_general(q, k_refs[src][hh, start:start + size, :], (((1,), (1,)), ((), ())),
                                preferred_element_type=F32)
            m_chunk = jnp.max(s, axis=-1, keepdims=True)
            m_new = m_chunk if m is None else jnp.maximum(m, m_chunk)
            p = jnp.exp2(s - m_new)
            pv = jnp.dot(p.astype(BF16), vx_refs[src][hh, start:start + size, :], preferred_element_type=F32)
            acc = pv if acc is None else acc * jnp.exp2(m - m_new) + pv
            m = m_new
        o_ref[:, hh * dv:(hh + 1) * dv] = (acc[:, 0:dv] / acc[:, dv:dv + 1]).astype(o_ref.dtype)


def _head_major_spec(arr, batch, rows, off, tiled, heads):
    per_batch = arr.shape[2] * arr.shape[0] // batch // rows
    hb = off // heads
    if arr.shape[0] == 1:
        return pl.BlockSpec((None, heads, rows, arr.shape[3]),
                            lambda bi, h, i: (0, hb + h, bi * per_batch + (i if tiled else 0), 0))
    return pl.BlockSpec((None, heads, rows, arr.shape[3]), lambda bi, h, i: (bi, hb + h, i if tiled else 0, 0))


def _dense_attention(q, kv_list, nh, q_off, name, batch, tq_target=1024, chunk_target=512, heads=2):
    dk = q.shape[3]
    sq = q.shape[0] * q.shape[2] // batch
    dv = kv_list[0][1].shape[3]
    offsets = [q_off] + [o for _, _, ko, vo in kv_list for o in (ko, vo)]
    heads = functools.reduce(math.gcd, [nh] + offsets, heads)
    tq = _pick_tile(sq, tq_target)
    nq = sq // tq
    chunks, arrays_k, arrays_v, scratch = [], [], [], []
    blocks = heads * (_nbytes((tq, dk), BF16) + _nbytes((tq, dv), BF16))
    tmp = 4 * _nbytes((tq, chunk_target), F32) + 2 * _nbytes((tq, 2 * dv), F32)
    kspecs, vspecs = [], []
    for src, (k, v, k_off, v_off) in enumerate(kv_list):
        sk = k.shape[0] * k.shape[2] // batch
        chunks += [(src, st, min(chunk_target, sk - st)) for st in range(0, sk, chunk_target)]
        kspecs.append(_head_major_spec(k, batch, sk, k_off, False, heads))
        vspecs.append(_head_major_spec(v, batch, sk, v_off, False, heads))
        arrays_k.append(k)
        arrays_v.append(v)
        scratch.append(pltpu.VMEM((heads, sk, 2 * dv), BF16))
        blocks += heads * (_nbytes((sk, dk), BF16) + _nbytes((sk, dv), BF16))
        tmp += heads * _nbytes((sk, 2 * dv), BF16)
    if q.shape[0] == 1:
        out_spec = pl.BlockSpec((None, tq, heads * dv), lambda bi, h, i: (0, bi * nq + i, h))
    else:
        out_spec = pl.BlockSpec((None, tq, heads * dv), lambda bi, h, i: (bi, i, h))
    return pl.pallas_call(
        functools.partial(_dense_attn_kernel, n_src=len(kv_list), chunks=chunks, heads=heads),
        grid=(batch, nh // heads, nq),
        in_specs=[_head_major_spec(q, batch, tq, q_off, True, heads)] + kspecs + vspecs,
        out_specs=out_spec,
        out_shape=jax.ShapeDtypeStruct((q.shape[0], q.shape[2], nh * dv), BF16),
        scratch_shapes=scratch,
        compiler_params=_params(blocks, tmp, 3),
        name=name,
    )(q, *arrays_k, *arrays_v)


def _pool_kernel(u_ref, w_ref, sc_ref, o_ref, *, windows):
    n, gd = u_ref.shape[0], w_ref.shape[1]
    t = lax.broadcasted_iota(jnp.int32, (n, gd), 0)
    for g, window in enumerate(windows):
        cols = slice(g * gd, (g + 1) * gd)
        u = u_ref[:, cols]
        half = window // 2
        ssum = u
        for j in range(1, half + 1):
            ssum = ssum + jnp.where(t >= j, pltpu.roll(u, j, axis=0), 0.0)
        for j in range(1, half):
            ssum = ssum + jnp.where(t < n - j, pltpu.roll(u, n - j, axis=0), 0.0)
        cnt = jnp.minimum(t + half, n) - jnp.maximum(t - half, 0)
        p = ssum / cnt.astype(F32) - u
        y = jnp.dot(p.astype(BF16), w_ref[g], preferred_element_type=F32)
        o_ref[:, cols] = (y * sc_ref[:, cols]).astype(o_ref.dtype)


def _pool_mix(u, w_pool, pool_scale):
    b, s, w = u.shape
    g, gd, _ = w_pool.shape
    blocks = _nbytes((s, w), F32) + _nbytes((g, gd, gd), BF16) + _nbytes((s, w), BF16)
    return pl.pallas_call(
        functools.partial(_pool_kernel, windows=POOL_WINDOWS[:g]),
        grid=(b,),
        in_specs=[
            pl.BlockSpec((None, s, w), lambda bi: (bi, 0, 0)),
            pl.BlockSpec((g, gd, gd), lambda bi: (0, 0, 0)),
            pl.BlockSpec((1, w), lambda bi: (0, 0)),
        ],
        out_specs=pl.BlockSpec((None, s, w), lambda bi: (bi, 0, 0)),
        out_shape=jax.ShapeDtypeStruct((b, s, w), BF16),
        compiler_params=_params(blocks, 6 * _nbytes((s, gd), F32), 1),
        name="pool_mix",
    )(u, w_pool.astype(BF16), pool_scale.reshape(1, w))


def _rope_table(n):
    half = QK_ROPE_DIM // 2
    nf = half // 2
    t = np.arange(n)
    inv = ROPE_BASE ** (-np.arange(nf, dtype=np.float64) / nf)
    ar = (t // GRID_W)[:, None] * inv[None, :]
    ac = (t % GRID_W)[:, None] * inv[None, :]
    cr, sr, cc, sc = np.cos(ar), np.sin(ar), np.cos(ac), np.sin(ac)
    return jnp.asarray(np.concatenate([cr, cr, cc, cc, -sr, sr, -sc, sc], axis=1), dtype=F32)


def _identity_rope_table(n):
    return jnp.concatenate([jnp.ones((n, QK_ROPE_DIM), F32), jnp.zeros((n, QK_ROPE_DIM), F32)], axis=1)


def _swap_rope_cols(w):
    nf = QK_ROPE_DIM // 4
    lead = w.shape[:-1]
    return jnp.flip(w.reshape(lead + (2, 2, nf)), axis=-2).reshape(lead + (QK_ROPE_DIM,))


def _even_mixer(h, hc, w_in, layer, w_four, rpb, with_ctx_out):
    fw = w_four.shape[0] * w_four.shape[1]
    nh = rpb.shape[0]
    u = _mm_plain(h, w_in, BF16, "even_in_fourier", layer, 0, fw, tm_target=2048)
    qkv = _mm_chunked(h, w_in, "even_in_qkv", layer, fw, tm_target=2048)
    qkvc = _mm_chunked(hc, w_in, "even_in_qkv_ctx", layer, fw)
    y_f = _four_channels(_dft_positions(u), w_four)
    y_a = _na_attention(qkv, qkvc, rpb)
    ctx_ops = None
    if with_ctx_out:
        b = h.shape[0]
        uc = _mm_plain(hc, w_in, BF16, "even_in_fourier_ctx", layer, 0, fw).reshape(b, -1, fw)
        yc_f = _four_channels(_dft_positions(uc), w_four).reshape(1, -1, fw)
        yc_a = _dense_attention(qkvc, [(qkvc, qkvc, nh, 2 * nh)], nh, 0, "ctx_attention_even", b)
        ctx_ops = [yc_f, yc_a]
    return [y_f, y_a], ctx_ops


def _ep_odd_in(acc, e_refs, o_refs, *, splits):
    gq_ref, gkv_ref, tab_ref = e_refs
    u_ref, cq_ref, ckv_ref, kr_ref = o_refs
    o_q, o_kv, o_kr = splits

    def rms(v, g_ref):
        return (v * lax.rsqrt(jnp.mean(v * v, axis=-1, keepdims=True) + EPS) * g_ref[...]).astype(BF16)

    u_ref[...] = acc[:, 0:o_q]
    cq_ref[...] = rms(acc[:, o_q:o_kv], gq_ref)
    ckv_ref[...] = rms(acc[:, o_kv:o_kr], gkv_ref)
    r = _rope_pair(acc[:, o_kr:], tab_ref[...])
    lane = lax.broadcasted_iota(jnp.int32, r.shape, 1)
    kr_ref[...] = jnp.where(lane < QK_ROPE_DIM, r, 0.0).astype(kr_ref.dtype)


def _mm_odd_in(a, w, g_q, g_kv, tab, splits, name, tm_target=1024):
    b, s, k = a.shape
    n = w.shape[1]
    o_q, o_kv, o_kr = splits
    tm = _pick_tile(s, tm_target)

    def row(width):
        return pl.BlockSpec((None, tm, width), lambda bi, i, j: (bi, i, 0))

    def vec(width):
        return pl.BlockSpec((1, width), lambda bi, i, j: (0, 0))

    extras = [(g_q.reshape(1, -1), vec(o_kv - o_q)), (g_kv.reshape(1, -1), vec(o_kr - o_kv)),
              (tab, pl.BlockSpec((tm, n - o_kr), lambda bi, i, j: (i, 0)))]
    outs = [(jax.ShapeDtypeStruct((b, s, o_q), F32), row(o_q)),
            (jax.ShapeDtypeStruct((b, s, o_kv - o_q), BF16), row(o_kv - o_q)),
            (jax.ShapeDtypeStruct((b, s, o_kr - o_kv), BF16), row(o_kr - o_kv)),
            (jax.ShapeDtypeStruct((b, s, n - o_kr), BF16), row(n - o_kr))]
    return _mm_call([(a, _a_spec(tm, k))], [(_w3(w), _w_spec(k, n))], extras,
                    functools.partial(_ep_odd_in, splits=splits), outs, (b, s // tm, 1), tm, n, name)


def _odd_mixer(h, hc, w_in, w_pool, pool_scale, g_q, g_kv, w_uq, w_ukv, with_ctx_out):
    b, s, d = h.shape
    lc = hc.shape[1] // b
    pw = w_pool.shape[0] * w_pool.shape[1]
    q_rank, kv_rank = g_q.shape[0], g_kv.shape[0]
    nh = w_uq.shape[1] // (QK_NOPE_DIM + QK_ROPE_DIM)
    splits = (pw, pw + q_rank, pw + q_rank + kv_rank)
    w_in = w_in.astype(BF16)
    w_in2 = jnp.concatenate([w_in, _swap_rope_cols(w_in[:, splits[2]:])], axis=1)
    q_scale = (QK_NOPE_DIM + QK_ROPE_DIM) ** -0.5 * LOG2E
    wq = (w_uq * q_scale).astype(BF16).reshape(q_rank, nh, QK_NOPE_DIM + QK_ROPE_DIM)
    wq_r = wq[..., QK_NOPE_DIM:]
    wq2 = jnp.concatenate([wq[..., :QK_NOPE_DIM], wq_r, _swap_rope_cols(wq_r)], axis=-1).reshape(q_rank, -1)
    w_ukv = w_ukv.astype(BF16)
    tab, tab_c = _rope_table(s), _identity_rope_table(b * lc)

    u, cq_n, ckv_n, k_rot = _mm_odd_in(h, w_in2, g_q, g_kv, tab, splits, "odd_in")
    uc, cqc_n, ckvc_n, k_rot_c = _mm_odd_in(hc, w_in2, g_q, g_kv, tab_c, splits, "odd_in_ctx")
    k_l, v_l = _mm_kv(ckv_n, w_ukv, k_rot, "odd_up_kv")
    k_c, v_c = _mm_kv(ckvc_n, w_ukv, k_rot_c, "odd_up_kv_ctx")
    q = _mm_qrope(cq_n, wq2, tab, "odd_up_q")
    attn = _dense_attention(q, [(k_l, v_l, 0, 0), (k_c, v_c, 0, 0)], nh, 0, "mla_attention", b)
    pooled = _pool_mix(u, w_pool, pool_scale)
    ctx_ops = None
    if with_ctx_out:
        qc = _mm_qrope(cqc_n, wq2, tab_c, "odd_up_q_ctx")
        attn_c = _dense_attention(qc, [(k_c, v_c, 0, 0)], nh, 0, "ctx_attention_odd", b)
        pooled_c = _pool_mix(uc.reshape(b, lc, pw), w_pool, pool_scale).reshape(1, b * lc, pw)
        ctx_ops = [pooled_c, attn_c]
    return [pooled, attn], ctx_ops


def _ffn(xs, norm_g, shift, scale, gate, wg, wu, wd, layer, tag):
    h2 = _norm_call(xs, norm_g, shift, scale)
    act = _swiglu_up(h2, wg, wu, layer, "ffn_up_" + tag)
    return _mm_residual([act], wd, xs, gate, "ffn_down_" + tag, layer)


def kernel(x, c, ctx, c_ctx, w_mod, b_mod, norm1_g, norm2_g, w_in_ab, w_four, na_rpb, w_out_ab, w_in_cd, w_pool,
           pool_scale, mla_gq, mla_gkv, w_uq, w_ukv, w_out_cd, w_ffn_gate, w_ffn_up, w_ffn_down, final_g):
    b, s, d = x.shape
    depth = w_mod.shape[0]
    nrow = -(-(b + 1) // 8) * 8
    cvec = jnp.concatenate([c, c_ctx[None, :], jnp.zeros((nrow - b - 1, d), F32)], axis=0)
    mods = _mod_call(cvec, w_mod, b_mod)
    fw, na_width = w_four.shape[1] * w_four.shape[2], na_rpb.shape[1] * HEAD_DIM
    col_scale = np.ones((w_in_ab.shape[2],), np.float32)
    col_scale[fw:fw + na_width] = HEAD_DIM ** -0.5 * LOG2E
    w_in_ab = (w_in_ab * jnp.asarray(col_scale)).astype(BF16)
    w_out_ab, w_out_cd = w_out_ab.astype(BF16), w_out_cd.astype(BF16)
    wg, wu, wd = w_ffn_gate, w_ffn_up, w_ffn_down.astype(BF16)
    xc = ctx.reshape(1, -1, d)
    for l in range(depth):
        last = l == depth - 1
        mod = mods[l, :b].reshape(b, 1, 6 * d)
        modc = mods[l, b].reshape(1, 1, 6 * d)
        sh1, sc1, g1, sh2, sc2, g2 = jnp.split(mod, 6, axis=-1)
        csh1, csc1, cg1, csh2, csc2, cg2 = jnp.split(modc, 6, axis=-1)
        h = _norm_call(x, norm1_g[l], sh1, sc1)
        hc = _norm_call(xc, norm1_g[l], csh1, csc1)
        i = l // 2
        if l % 2 == 0:
            ops, ctx_ops = _even_mixer(h, hc, w_in_ab, i, w_four[i], na_rpb[i], not last)
            w_out = w_out_ab
        else:
            ops, ctx_ops = _odd_mixer(h, hc, w_in_cd[i], w_pool[i], pool_scale[i], mla_gq[i], mla_gkv[i],
                                      w_uq[i], w_ukv[i], not last)
            w_out = w_out_cd
        x = _mm_residual(ops, w_out, x, g1, f"out_proj_{l}", i, tm_target=512, tn_target=d)
        x = _ffn(x, norm2_g[l], sh2, sc2, g2, wg, wu, wd, l, f"{l}")
        if not last:
            xc = _mm_residual(ctx_ops, w_out, xc, cg1, f"out_proj_ctx_{l}", i)
            xc = _ffn(xc, norm2_g[l], csh2, csc2, cg2, wg, wu, wd, l, f"ctx_{l}")
    return _norm_call(x, final_g, out_dtype=F32)
```

```python
import functools
import math

import numpy as np
import jax
import jax.numpy as jnp
from jax import lax
from jax.experimental import pallas as pl
from jax.experimental.pallas import tpu as pltpu

GRID_W = 64
HEAD_DIM = 128
EPS = 1e-6
NEG_INF = -1e30
NA_WIN_H = 8
NA_WIN_W = 16
POOL_WINDOWS = (2, 4, 8, 16)
QK_NOPE_DIM = 128
QK_ROPE_DIM = 64
V_HEAD_DIM = 128
ROPE_BASE = 10000.0
LOG2E = 1.4426950408889634

V7X_LANES = 128
V7X_VMEM_BYTES = 64 * 1024 * 1024
V7X_VMEM_CAP = 56 * 1024 * 1024
NORM_INPUT_BUFFERS = 3
NA_ROWS_PER_TILE = 4

BF16 = jnp.bfloat16
F32 = jnp.float32


def _pick_tile(n, target, mult=16):
    if n <= target:
        return n
    for t in range(target, 0, -1):
        if n % t == 0 and t % mult == 0:
            return t
    return n


def _nbytes(shape, dtype):
    return int(np.prod(shape)) * jnp.dtype(dtype).itemsize


def _params(block_bytes, scratch_bytes=0, ndims=3):
    est = 2 * block_bytes + scratch_bytes
    limit = int(min(max(est * 5 // 4 + (4 << 20), 16 << 20), V7X_VMEM_CAP))
    return pltpu.CompilerParams(dimension_semantics=("arbitrary",) * ndims, vmem_limit_bytes=limit)


def _mod_kernel(c_ref, w_ref, b_ref, o_ref):
    cv = c_ref[...]
    s = cv * (1.0 / (1.0 + jnp.exp(-cv)))
    acc = jnp.dot(s.astype(BF16), w_ref[...].astype(BF16), preferred_element_type=F32)
    o_ref[...] = acc + b_ref[...]


def _mod_call(cvec, w_mod, b_mod):
    depth, d, n = w_mod.shape
    r = cvec.shape[0]
    tn = _pick_tile(n, 1024, V7X_LANES)
    blocks = _nbytes((r, d), F32) + _nbytes((d, tn), F32) + 2 * _nbytes((r, tn), F32)
    return pl.pallas_call(
        _mod_kernel,
        grid=(depth, n // tn),
        in_specs=[
            pl.BlockSpec((r, d), lambda l, j: (0, 0)),
            pl.BlockSpec((None, d, tn), lambda l, j: (l, 0, j)),
            pl.BlockSpec((None, 1, tn), lambda l, j: (l, 0, j)),
        ],
        out_specs=pl.BlockSpec((None, r, tn), lambda l, j: (l, 0, j)),
        out_shape=jax.ShapeDtypeStruct((depth, r, n), F32),
        compiler_params=_params(blocks, _nbytes((d, tn), BF16), 2),
        name="mod_vectors",
    )(cvec, w_mod, b_mod.reshape(depth, 1, n))


NORM_OUTPUT_BUFFERS = 2


def _norm_stream_kernel(x_hbm, g_ref, sh_ref, sc_ref, o_hbm, xbuf, obuf, in_sem, out_sem, *, tiles_per_batch, ts,
                        total, modulate):
    def in_copy(step, slot):
        bi, row = step // tiles_per_batch, (step % tiles_per_batch) * ts
        return pltpu.make_async_copy(x_hbm.at[bi, pl.ds(row, ts), :], xbuf.at[slot], in_sem.at[slot])

    def out_copy(step, slot):
        bi, row = step // tiles_per_batch, (step % tiles_per_batch) * ts
        return pltpu.make_async_copy(obuf.at[slot], o_hbm.at[bi, pl.ds(row, ts), :], out_sem.at[slot])

    for step in range(min(NORM_INPUT_BUFFERS, total)):
        in_copy(step, step).start()

    def body(step, carry):
        slot, oslot = step % NORM_INPUT_BUFFERS, step % NORM_OUTPUT_BUFFERS
        in_copy(step, slot).wait()

        @pl.when(step >= NORM_OUTPUT_BUFFERS)
        def _():
            out_copy(step - NORM_OUTPUT_BUFFERS, oslot).wait()

        xf = xbuf[slot]
        y = xf * lax.rsqrt(jnp.mean(xf * xf, axis=-1, keepdims=True) + EPS)
        if modulate:
            bi = step // tiles_per_batch
            col = g_ref[...] * (1.0 + sc_ref[bi])
            obuf[oslot] = (y * col + sh_ref[bi]).astype(obuf.dtype)
        else:
            obuf[oslot] = (y * g_ref[...]).astype(obuf.dtype)
        out_copy(step, oslot).start(priority=1)

        @pl.when(step + NORM_INPUT_BUFFERS < total)
        def _():
            in_copy(step + NORM_INPUT_BUFFERS, slot).start()

        return carry

    lax.fori_loop(0, total, body, 0)
    for step in range(max(total - NORM_OUTPUT_BUFFERS, 0), total):
        out_copy(step, step % NORM_OUTPUT_BUFFERS).wait()


def _norm_call(x, g, shift=None, scale=None, out_dtype=BF16):
    b, s, d = x.shape
    ts = _pick_tile(s, 1024 if b * s >= 8192 else 256)
    modulate = shift is not None
    if not modulate:
        shift = scale = jnp.zeros((b, 1, d), F32)
    whole = pl.BlockSpec(memory_space=pltpu.VMEM)
    ring = NORM_INPUT_BUFFERS * _nbytes((ts, d), F32) + NORM_OUTPUT_BUFFERS * _nbytes((ts, d), out_dtype)
    limit = int(min(ring + 3 * _nbytes((ts, d), F32) + (4 << 20), V7X_VMEM_CAP))
    return pl.pallas_call(
        functools.partial(_norm_stream_kernel, tiles_per_batch=s // ts, ts=ts, total=b * (s // ts),
                          modulate=modulate),
        in_specs=[pl.BlockSpec(memory_space=pl.ANY), whole, whole, whole],
        out_specs=pl.BlockSpec(memory_space=pl.ANY),
        out_shape=jax.ShapeDtypeStruct((b, s, d), out_dtype),
        scratch_shapes=[pltpu.VMEM((NORM_INPUT_BUFFERS, ts, d), F32), pltpu.VMEM((NORM_OUTPUT_BUFFERS, ts, d), out_dtype),
                        pltpu.SemaphoreType.DMA((NORM_INPUT_BUFFERS,)), pltpu.SemaphoreType.DMA((NORM_OUTPUT_BUFFERS,))],
        compiler_params=pltpu.CompilerParams(vmem_limit_bytes=limit),
        name="rmsnorm_modulate" if modulate else "rmsnorm",
    )(x, g.reshape(1, d), shift, scale)


def _mm_kernel(*refs, n_ops, n_extra, epilogue):
    a_refs = refs[:n_ops]
    w_refs = refs[n_ops:2 * n_ops]
    e_refs = refs[2 * n_ops:2 * n_ops + n_extra]
    o_refs = refs[2 * n_ops + n_extra:]
    acc = None
    for a_ref, w_ref in zip(a_refs, w_refs):
        part = jnp.dot(a_ref[...], w_ref[...], preferred_element_type=F32)
        acc = part if acc is None else acc + part
    epilogue(acc, e_refs, o_refs)


def _mm_call(a_ops, w_ops, extras, epilogue, outs, grid, tm, tn, name, tmp_bytes=0):
    arrays = [a for a, _ in a_ops] + [w for w, _ in w_ops] + [e for e, _ in extras]
    in_specs = [s for _, s in a_ops] + [s for _, s in w_ops] + [s for _, s in extras]
    blocks = 0
    for arr, spec in a_ops + w_ops + extras:
        blocks += _nbytes([d for d in spec.block_shape if d is not None], arr.dtype)
    for sds, spec in outs:
        blocks += _nbytes([d for d in spec.block_shape if d is not None], sds.dtype)
    kern = functools.partial(_mm_kernel, n_ops=len(a_ops), n_extra=len(extras), epilogue=epilogue)
    res = pl.pallas_call(
        kern,
        grid=grid,
        in_specs=in_specs,
        out_specs=[s for _, s in outs],
        out_shape=[o for o, _ in outs],
        compiler_params=_params(blocks, 3 * _nbytes((tm, tn), F32) + tmp_bytes, len(grid)),
        name=name,
    )(*arrays)
    return res


def _a_spec(tm, k, col_blk=0):
    return pl.BlockSpec((None, tm, k), lambda b, i, j: (b, i, col_blk))


def _w3(w):
    return w if w.ndim == 3 else w[None]


def _w_spec(k, tn, layer=0, row_blk=0, col_off=0):
    return pl.BlockSpec((None, k, tn), lambda b, i, j: (layer, row_blk, j + col_off))


def _tile_spec(tm, tn):
    return pl.BlockSpec((None, tm, tn), lambda b, i, j: (b, i, j))


def _vec_spec(tn):
    return pl.BlockSpec((None, 1, tn), lambda b, i, j: (b, 0, j))


def _ep_store(acc, e_refs, o_refs):
    o_refs[0][...] = acc.astype(o_refs[0].dtype)


def _col_tile(n, col0, target, mult):
    return _pick_tile(math.gcd(n, col0) if col0 else n, target, mult)


def _mm_plain(a, w, out_dtype, name, layer=0, col0=0, n=None, tm_target=1024, tn_target=512):
    b, s, k = a.shape
    w = _w3(w)
    n = w.shape[2] - col0 if n is None else n
    tm = _pick_tile(s, tm_target)
    tn = _col_tile(n, col0, tn_target, V7X_LANES)
    return _mm_call([(a, _a_spec(tm, k))], [(w, _w_spec(k, tn, layer, 0, col0 // tn))], [], _ep_store,
                    [(jax.ShapeDtypeStruct((b, s, n), out_dtype), _tile_spec(tm, tn))],
                    (b, s // tm, n // tn), tm, tn, name)[0]


def _ep_chunks(acc, e_refs, o_refs, *, width):
    o_ref = o_refs[0]
    for c in range(acc.shape[1] // width):
        o_ref[c] = acc[:, c * width:(c + 1) * width].astype(o_ref.dtype)


def _mm_chunked(a, w, name, layer=0, col0=0, n=None, width=HEAD_DIM, tm_target=1024, tn_target=512):
    b, s, k = a.shape
    w = _w3(w)
    n = w.shape[2] - col0 if n is None else n
    tm = _pick_tile(s, tm_target)
    tn = _col_tile(n, col0, tn_target, width)
    cpt = tn // width
    ospec = pl.BlockSpec((None, cpt, tm, width), lambda bi, i, j: (bi, j, i, 0))
    return _mm_call([(a, _a_spec(tm, k))], [(w, _w_spec(k, tn, layer, 0, col0 // tn))], [],
                    functools.partial(_ep_chunks, width=width),
                    [(jax.ShapeDtypeStruct((b, n // width, s, width), BF16), ospec)],
                    (b, s // tm, n // tn), tm, tn, name)[0]


def _ep_residual(acc, e_refs, o_refs):
    x_ref, g_ref = e_refs
    o_refs[0][...] = x_ref[...] + g_ref[...] * acc


def _mm_residual(a_list, w, x, gate, name, layer=0, tm_target=1024, tn_target=512):
    b, s, n = x.shape
    w = _w3(w)
    tm = _pick_tile(s, tm_target)
    tn = _pick_tile(n, tn_target, V7X_LANES)
    kblk = functools.reduce(math.gcd, [a.shape[2] for a in a_list])
    a_ops, w_ops = [], []
    for a in a_list:
        for cb in range(a.shape[2] // kblk):
            a_ops.append((a, _a_spec(tm, kblk, cb)))
            w_ops.append((w, _w_spec(kblk, tn, layer, len(w_ops))))
    extras = [(x, _tile_spec(tm, tn)), (gate, _vec_spec(tn))]
    return _mm_call(a_ops, w_ops, extras, _ep_residual,
                    [(jax.ShapeDtypeStruct((b, s, n), F32), _tile_spec(tm, tn))],
                    (b, s // tm, n // tn), tm, tn, name)[0]


def _swiglu_kernel(a_ref, wg_ref, wu_ref, o_ref, wgb_ref, wub_ref):
    @pl.when((pl.program_id(1) == 0) & (pl.program_id(2) == 0))
    def _():
        wgb_ref[...] = wg_ref[...].astype(BF16)
        wub_ref[...] = wu_ref[...].astype(BF16)

    a = a_ref[...]
    g = jnp.dot(a, wgb_ref[...], preferred_element_type=F32)
    u = jnp.dot(a, wub_ref[...], preferred_element_type=F32)
    o_ref[...] = (g * (1.0 / (1.0 + jnp.exp(-g))) * u).astype(o_ref.dtype)


def _swiglu_up(a, wg, wu, layer, name, tm_target=1024, tn_target=512):
    b, s, k = a.shape
    n = wg.shape[2]
    tm = _pick_tile(s, tm_target)
    tn = _pick_tile(n, tn_target, V7X_LANES)
    blocks = _nbytes((tm, k), BF16) + 2 * _nbytes((k, tn), F32) + _nbytes((tm, tn), BF16)
    wspec = pl.BlockSpec((None, k, tn), lambda j, bi, i: (layer, 0, j))
    return pl.pallas_call(
        _swiglu_kernel,
        grid=(n // tn, b, s // tm),
        in_specs=[pl.BlockSpec((None, tm, k), lambda j, bi, i: (bi, i, 0)), wspec, wspec],
        out_specs=pl.BlockSpec((None, tm, tn), lambda j, bi, i: (bi, i, j)),
        out_shape=jax.ShapeDtypeStruct((b, s, n), BF16),
        scratch_shapes=[pltpu.VMEM((k, tn), BF16), pltpu.VMEM((k, tn), BF16)],
        compiler_params=_params(blocks, 2 * _nbytes((k, tn), BF16) + 4 * _nbytes((tm, tn), F32), 3),
        name=name,
    )(a, wg, wu)


def _rope_pair(hi, tab):
    z = hi * tab
    return z + pltpu.roll(z, QK_ROPE_DIM, axis=1)


def _ep_qrope(acc, e_refs, o_refs, *, heads):
    tab_ref, = e_refs
    o_ref = o_refs[0]
    tab = tab_ref[...]
    blk = QK_NOPE_DIM + 2 * QK_ROPE_DIM
    for h in range(heads):
        o_ref[h, :, 0:QK_NOPE_DIM] = acc[:, h * blk:h * blk + QK_NOPE_DIM].astype(o_ref.dtype)
        hi = acc[:, h * blk + QK_NOPE_DIM:(h + 1) * blk]
        o_ref[h, :, QK_NOPE_DIM:blk] = _rope_pair(hi, tab).astype(o_ref.dtype)


def _mm_qrope(a, w, tab, name, tm_target=1024, heads_per_tile=6):
    b, s, k = a.shape
    blk = QK_NOPE_DIM + 2 * QK_ROPE_DIM
    nh = w.shape[1] // blk
    hpt = math.gcd(nh, heads_per_tile)
    tn = hpt * blk
    tm = _pick_tile(s, tm_target)
    tspec = pl.BlockSpec((tm, 2 * QK_ROPE_DIM), lambda bi, i, j: (i, 0))
    ospec = pl.BlockSpec((None, hpt, tm, blk), lambda bi, i, j: (bi, j, i, 0))
    return _mm_call([(a, _a_spec(tm, k))], [(_w3(w), _w_spec(k, tn))], [(tab, tspec)],
                    functools.partial(_ep_qrope, heads=hpt),
                    [(jax.ShapeDtypeStruct((b, nh, s, blk), BF16), ospec)],
                    (b, s // tm, nh // hpt), tm, tn, name)[0]


def _ep_kv(acc, e_refs, o_refs, *, heads):
    kr_ref, = e_refs
    k_ref, v_ref = o_refs
    kr = kr_ref[...]
    blk = QK_NOPE_DIM + V_HEAD_DIM
    for h in range(heads):
        k_ref[h, :, 0:QK_NOPE_DIM] = acc[:, h * blk:h * blk + QK_NOPE_DIM].astype(k_ref.dtype)
        k_ref[h, :, QK_NOPE_DIM:QK_NOPE_DIM + kr.shape[1]] = kr
        v_ref[h] = acc[:, h * blk + QK_NOPE_DIM:(h + 1) * blk].astype(v_ref.dtype)


def _mm_kv(a, w, krot, name, tm_target=1024, heads_per_tile=6):
    b, s, k = a.shape
    blk = QK_NOPE_DIM + V_HEAD_DIM
    nh = w.shape[1] // blk
    hpt = math.gcd(nh, heads_per_tile)
    tn = hpt * blk
    tm = _pick_tile(s, tm_target)
    kd = QK_NOPE_DIM + krot.shape[2]
    krspec = pl.BlockSpec((None, tm, krot.shape[2]), lambda bi, i, j: (bi, i, 0))
    kspec = pl.BlockSpec((None, hpt, tm, kd), lambda bi, i, j: (bi, j, i, 0))
    vspec = pl.BlockSpec((None, hpt, tm, V_HEAD_DIM), lambda bi, i, j: (bi, j, i, 0))
    return _mm_call([(a, _a_spec(tm, k))], [(_w3(w), _w_spec(k, tn))], [(krot, krspec)],
                    functools.partial(_ep_kv, heads=hpt),
                    [(jax.ShapeDtypeStruct((b, nh, s, kd), BF16), kspec),
                     (jax.ShapeDtypeStruct((b, nh, s, V_HEAD_DIM), BF16), vspec)],
                    (b, s // tm, nh // hpt), tm, tn, name)


def _dft_matrix(n):
    m = jnp.arange(n, dtype=jnp.int32)
    side = 1
    while side * side < n:
        side *= 2
    if side * side != n:
        k = jnp.arange(n, dtype=jnp.int32)
        ang = (2.0 * math.pi / n) * ((k[:, None] * m[None, :]) % n).astype(F32)
        return jnp.concatenate([jnp.cos(ang), jnp.sin(ang)], axis=0).astype(BF16)[None]
    a = jnp.arange(side, dtype=jnp.int32)
    ang_x = (2.0 * math.pi / side) * ((a[:, None] * m[None, :]) % side).astype(F32)
    ang_y = (2.0 * math.pi / n) * ((a[:, None] * m[None, :]) % n).astype(F32)
    cx, sx = jnp.cos(ang_x), jnp.sin(ang_x)
    cy, sy = jnp.cos(ang_y)[None], jnp.sin(ang_y)[None]
    p = jnp.concatenate([cx, sx], axis=0)[:, None, :]
    q = jnp.concatenate([sx, -cx], axis=0)[:, None, :]
    return (p * cy - q * sy).astype(BF16).reshape(1, 2 * n, n)


def _dft_positions_dense(u):
    b, s, w = u.shape
    mat = _dft_matrix(s)
    tm = _pick_tile(2 * s, 1024)
    tn = _pick_tile(w, 512, V7X_LANES)
    aspec = pl.BlockSpec((None, tm, s), lambda bi, i, j: (0, i, 0))
    wspec = pl.BlockSpec((None, s, tn), lambda bi, i, j: (bi, 0, j))
    return _mm_call([(mat, aspec)], [(u, wspec)], [], _ep_store,
                    [(jax.ShapeDtypeStruct((b, 2 * s, w), BF16), _tile_spec(tm, tn))],
                    (b, 2 * s // tm, w // tn), tm, tn, "fourier_positions")[0]


DFT_ROWS_PER_STEP = 16


def _dft_stage1_kernel(x_ref, a_ref, tc_ref, ts_ref, o_ref, *, side, width, n2_per_step):
    g = jnp.dot(a_ref[...], x_ref[...], preferred_element_type=F32)
    gr, gs = g[0:side], g[side:2 * side]
    for n2 in range(n2_per_step):
        tcs = tc_ref[:, n2 * V7X_LANES:(n2 + 1) * V7X_LANES]
        tss = ts_ref[:, n2 * V7X_LANES:(n2 + 1) * V7X_LANES]
        for c0 in range(n2 * width, (n2 + 1) * width, V7X_LANES):
            cols = slice(c0, c0 + V7X_LANES)
            o_ref[0:side, cols] = (gr[:, cols] * tcs - gs[:, cols] * tss).astype(o_ref.dtype)
            o_ref[side:2 * side, cols] = (gr[:, cols] * tss + gs[:, cols] * tcs).astype(o_ref.dtype)


def _dft_stage2_kernel(h_ref, m_ref, o_ref, scr_ref, *, side):
    rows = h_ref.shape[1]
    ngrp = scr_ref.shape[0]
    for j in range(rows):
        hcat = jnp.concatenate([h_ref[0, j], h_ref[1, j]], axis=0)
        res = jnp.dot(m_ref[...], hcat, preferred_element_type=F32)
        for c in range(ngrp):
            cols = slice(c * V7X_LANES, (c + 1) * V7X_LANES)
            scr_ref[c, pl.ds(j, side, stride=rows), :] = res[0:side, cols]
            scr_ref[c, pl.ds(side * rows + j, side, stride=rows), :] = res[side:2 * side, cols]
    for c in range(ngrp):
        cols = slice(c * V7X_LANES, (c + 1) * V7X_LANES)
        for p in range(2):
            part = scr_ref[c, p * side * rows:(p + 1) * side * rows, :].reshape(side, rows, V7X_LANES)
            o_ref[p, :, :, cols] = part.astype(o_ref.dtype)


def _dft_positions(u):
    b, s, w = u.shape
    side = math.isqrt(s)
    if side * side != s or side % DFT_ROWS_PER_STEP or w % V7X_LANES:
        return _dft_positions_dense(u)
    idx = np.arange(side)
    ang = 2.0 * np.pi * ((idx[:, None] * idx[None, :]) % side) / side
    cs, sn = np.cos(ang), np.sin(ang)
    a1 = jnp.asarray(np.concatenate([cs, sn], axis=0), dtype=F32).astype(BF16)
    m2 = jnp.asarray(np.block([[cs, -sn], [sn, cs]]), dtype=F32).astype(BF16)
    tw = 2.0 * np.pi * (idx[:, None] * idx[None, :]) / s
    tcos = jnp.repeat(jnp.asarray(np.cos(tw), dtype=F32), V7X_LANES, axis=1)
    tsin = jnp.repeat(jnp.asarray(np.sin(tw), dtype=F32), V7X_LANES, axis=1)
    n2s = DFT_ROWS_PER_STEP
    tc = n2s * w
    blocks1 = _nbytes((side, tc), BF16) + _nbytes((2 * side, side), BF16) + 2 * _nbytes((side, n2s * V7X_LANES), F32) \
        + _nbytes((2 * side, tc), BF16)
    h = pl.pallas_call(
        functools.partial(_dft_stage1_kernel, side=side, width=w, n2_per_step=n2s),
        grid=(b, side // n2s),
        in_specs=[
            pl.BlockSpec((None, side, tc), lambda bi, j: (bi, 0, j)),
            pl.BlockSpec((2 * side, side), lambda bi, j: (0, 0)),
            pl.BlockSpec((side, n2s * V7X_LANES), lambda bi, j: (0, j)),
            pl.BlockSpec((side, n2s * V7X_LANES), lambda bi, j: (0, j)),
        ],
        out_specs=pl.BlockSpec((None, 2 * side, tc), lambda bi, j: (bi, 0, j)),
        out_shape=jax.ShapeDtypeStruct((b, 2 * side, side * w), BF16),
        compiler_params=_params(blocks1, 3 * _nbytes((2 * side, tc), F32), 2),
        name="fourier_positions_stage1",
    )(u.reshape(b, side, side * w), a1, tcos, tsin)
    rows = DFT_ROWS_PER_STEP
    blocks2 = 2 * _nbytes((2, rows, side, w), BF16) + _nbytes((2 * side, 2 * side), BF16)
    f2 = pl.pallas_call(
        functools.partial(_dft_stage2_kernel, side=side),
        grid=(b, side // rows),
        in_specs=[
            pl.BlockSpec((None, 2, rows, side, w), lambda bi, i: (bi, 0, i, 0, 0)),
            pl.BlockSpec((2 * side, 2 * side), lambda bi, i: (0, 0)),
        ],
        out_specs=pl.BlockSpec((None, 2, side, rows, w), lambda bi, i: (bi, 0, 0, i, 0)),
        out_shape=jax.ShapeDtypeStruct((b, 2, side, side, w), BF16),
        scratch_shapes=[pltpu.VMEM((w // V7X_LANES, 2 * side * rows, V7X_LANES), F32)],
        compiler_params=_params(blocks2, _nbytes((2 * side * rows, w), F32) + 4 * _nbytes((2 * side, w), F32), 2),
        name="fourier_positions_stage2",
    )(h.reshape(b, 2, side, side, w), m2)
    return f2.reshape(b, 2 * s, w)


def _four_channel_kernel(fr_ref, fi_ref, cs_ref, w_ref, o_ref, *, norm):
    gd = w_ref.shape[1]
    for g in range(w_ref.shape[0]):
        cols = slice(g * gd, (g + 1) * gd)
        f = jnp.dot(fr_ref[:, cols].astype(BF16), cs_ref[0:gd, :], preferred_element_type=F32)
        f = f + jnp.dot(fi_ref[:, cols].astype(BF16), cs_ref[gd:2 * gd, :], preferred_element_type=F32)
        f = (f * norm).astype(BF16)
        o_ref[:, cols] = jnp.dot(f, w_ref[g], preferred_element_type=F32).astype(o_ref.dtype)


def _four_channels(f2, w_four):
    b, s2, w = f2.shape
    s = s2 // 2
    g, gd, _ = w_four.shape
    idx = np.arange(gd)
    ang = 2.0 * np.pi * ((idx[:, None] * idx[None, :]) % gd) / gd
    cs = jnp.asarray(np.concatenate([np.cos(ang), -np.sin(ang)], axis=0), dtype=F32).astype(BF16)
    ts = _pick_tile(s, 1024)
    nblk = s // ts
    blocks = 2 * _nbytes((ts, w), f2.dtype) + _nbytes((ts, w), BF16) + _nbytes((2 * gd, gd), BF16) \
        + _nbytes((g, gd, gd), BF16)
    return pl.pallas_call(
        functools.partial(_four_channel_kernel, norm=1.0 / math.sqrt(s * gd)),
        grid=(b, nblk),
        in_specs=[
            pl.BlockSpec((None, ts, w), lambda bi, i: (bi, i, 0)),
            pl.BlockSpec((None, ts, w), lambda bi, i: (bi, nblk + i, 0)),
            pl.BlockSpec((2 * gd, gd), lambda bi, i: (0, 0)),
            pl.BlockSpec((g, gd, gd), lambda bi, i: (0, 0, 0)),
        ],
        out_specs=pl.BlockSpec((None, ts, w), lambda bi, i: (bi, i, 0)),
        out_shape=jax.ShapeDtypeStruct((b, s, w), BF16),
        compiler_params=_params(blocks, 6 * _nbytes((ts, gd), F32), 2),
        name="fourier_channels",
    )(f2, f2, cs, w_four.astype(BF16))


def _na_plan(rows):
    rpt = NA_ROWS_PER_TILE
    kh = min(NA_WIN_H, rows)
    wrows = min(kh + rpt - 1, rows)
    cases, ws_list, case_list = {}, [], []
    for t in range(rows // rpt):
        r0 = t * rpt
        ws = int(np.clip(r0 - kh // 2, 0, rows - wrows))
        plan = []
        for r in range(r0, r0 + rpt):
            rs = int(np.clip(r - kh // 2, 0, rows - kh))
            plan.append(tuple((ws + w) - r + NA_WIN_H - 1 if rs <= ws + w < rs + kh else None
                              for w in range(wrows)))
        plan = tuple(plan)
        cases.setdefault(plan, len(cases))
        ws_list.append(ws)
        case_list.append(cases[plan])
    return wrows, ws_list, case_list, list(cases)


def _na_block_table(rpb):
    qcol = np.arange(GRID_W)[:, None]
    kcol = np.arange(GRID_W)[None, :]
    cstart = np.clip(qcol - NA_WIN_W // 2, 0, GRID_W - NA_WIN_W)
    col_ok = (kcol >= cstart) & (kcol < cstart + NA_WIN_W)
    cidx = np.clip(kcol - qcol + NA_WIN_W - 1, 0, 2 * NA_WIN_W - 2)
    csel = jnp.asarray(np.eye(rpb.shape[2], dtype=np.float32)[cidx])
    blk = jnp.einsum("hrs,qks->hrqk", rpb, csel, precision=lax.Precision.HIGHEST)
    blk = jnp.where(jnp.asarray(col_ok), blk * LOG2E, NEG_INF)
    return jnp.concatenate([blk, blk], axis=-1)


def _ones_column(rows, width):
    lane = lax.broadcasted_iota(jnp.int32, (rows, width), 1)
    return jnp.where(lane == 0, 1.0, 0.0).astype(BF16)


def _na_kernel(q_ref, k_ref, v_ref, kc_ref, vc_ref, blk_ref, o_ref, bias_ref, vx_ref, vcx_ref,
               *, tiles, tq, tk, plans):
    d = v_ref.shape[1]

    @pl.when(pl.program_id(1) == 0)
    def _():
        for c, plan in enumerate(plans):
            for qr, row in enumerate(plan):
                for kr, ridx in enumerate(row):
                    half = slice((kr % 2) * GRID_W, (kr % 2 + 1) * GRID_W)
                    if ridx is None:
                        blk = jnp.full((GRID_W, GRID_W), NEG_INF, F32)
                    else:
                        blk = blk_ref[ridx][:, half]
                    bias_ref[c, qr * GRID_W:(qr + 1) * GRID_W, kr * GRID_W:(kr + 1) * GRID_W] = blk

    vx_ref[:, 0:d] = v_ref[...]
    vx_ref[:, d:2 * d] = _ones_column(v_ref.shape[0], d)
    vcx_ref[:, 0:d] = vc_ref[...]
    vcx_ref[:, d:2 * d] = _ones_column(vc_ref.shape[0], d)
    kc = kc_ref[...]
    dn = (((1,), (1,)), ((), ()))

    for t, (ws, case) in enumerate(tiles):
        q0, k0 = t * tq, ws * GRID_W
        q = q_ref[q0:q0 + tq, :]
        s_loc = lax.dot_general(q, k_ref[k0:k0 + tk, :], dn, preferred_element_type=F32) + bias_ref[case]
        s_ctx = lax.dot_general(q, kc, dn, preferred_element_type=F32)
        m = jnp.maximum(jnp.max(s_loc, axis=-1, keepdims=True), jnp.max(s_ctx, axis=-1, keepdims=True))
        p_loc = jnp.exp2(s_loc - m).astype(BF16)
        p_ctx = jnp.exp2(s_ctx - m).astype(BF16)
        acc = jnp.dot(p_loc, vx_ref[k0:k0 + tk, :], preferred_element_type=F32)
        acc = acc + jnp.dot(p_ctx, vcx_ref[...], preferred_element_type=F32)
        o_ref[q0:q0 + tq, :] = (acc[:, 0:d] / acc[:, d:d + 1]).astype(o_ref.dtype)


def _na_attention(qkv, qkvc, rpb):
    b, c3, s, d = qkv.shape
    nh = c3 // 3
    lc = qkvc.shape[2] // b
    rows = s // GRID_W
    wrows, ws_list, case_list, plans = _na_plan(rows)
    blk = _na_block_table(rpb)
    nrel = blk.shape[1]
    tq = NA_ROWS_PER_TILE * GRID_W
    tk = wrows * GRID_W
    blocks = 3 * _nbytes((s, d), BF16) + 2 * _nbytes((lc, d), BF16) + _nbytes((nrel, GRID_W, 2 * GRID_W), F32) \
        + _nbytes((s, d), BF16)
    scratch = _nbytes((len(plans), tq, tk), F32) + _nbytes((s + lc, 2 * d), BF16)
    return pl.pallas_call(
        functools.partial(_na_kernel, tiles=list(zip(ws_list, case_list)), tq=tq, tk=tk, plans=plans),
        grid=(nh, b),
        in_specs=[
            pl.BlockSpec((None, None, s, d), lambda h, bi: (bi, h, 0, 0)),
            pl.BlockSpec((None, None, s, d), lambda h, bi: (bi, nh + h, 0, 0)),
            pl.BlockSpec((None, None, s, d), lambda h, bi: (bi, 2 * nh + h, 0, 0)),
            pl.BlockSpec((None, None, lc, d), lambda h, bi: (0, nh + h, bi, 0)),
            pl.BlockSpec((None, None, lc, d), lambda h, bi: (0, 2 * nh + h, bi, 0)),
            pl.BlockSpec((None, nrel, GRID_W, 2 * GRID_W), lambda h, bi: (h, 0, 0, 0)),
        ],
        out_specs=pl.BlockSpec((None, s, d), lambda h, bi: (bi, 0, h)),
        out_shape=jax.ShapeDtypeStruct((b, s, nh * d), BF16),
        scratch_shapes=[pltpu.VMEM((len(plans), tq, tk), F32), pltpu.VMEM((s, 2 * d), BF16),
                        pltpu.VMEM((lc, 2 * d), BF16)],
        compiler_params=_params(blocks, scratch + 8 * _nbytes((tq, tk + lc), F32), 2),
        name="neighbourhood_attention",
    )(qkv, qkv, qkv, qkvc, qkvc, blk)


def _dense_attn_kernel(*refs, n_src, chunks, heads):
    q_ref = refs[0]
    k_refs = refs[1:1 + n_src]
    v_refs = refs[1 + n_src:1 + 2 * n_src]
    o_ref = refs[1 + 2 * n_src]
    vx_refs = refs[2 + 2 * n_src:]
    dv = v_refs[0].shape[2]

    @pl.when(pl.program_id(2) == 0)
    def _():
        for v_ref, vx_ref in zip(v_refs, vx_refs):
            for hh in range(heads):
                vx_ref[hh, :, 0:dv] = v_ref[hh]
                vx_ref[hh, :, dv:2 * dv] = _ones_column(v_ref.shape[1], dv)

    for hh in range(heads):
        q = q_ref[hh]
        m = acc = None
        for src, start, size in chunks:
            s = lax.dot_general(q, k_refs[src][hh, start:start + size, :], (((1,), (1,)), ((), ())),
                                preferred_element_type=F32)
            m_chunk = jnp.max(s, axis=-1, keepdims=True)
            m_new = m_chunk if m is None else jnp.maximum(m, m_chunk)
            p = jnp.exp2(s - m_new)
            pv = jnp.dot(p.astype(BF16), vx_refs[src][hh, start:start + size, :], preferred_element_type=F32)
            acc = pv if acc is None else acc * jnp.exp2(m - m_new) + pv
            m = m_new
        o_ref[:, hh * dv:(hh + 1) * dv] = (acc[:, 0:dv] / acc[:, dv:dv + 1]).astype(o_ref.dtype)


def _head_major_spec(arr, batch, rows, off, tiled, heads):
    per_batch = arr.shape[2] * arr.shape[0] // batch // rows
    hb = off // heads
    if arr.shape[0] == 1:
        return pl.BlockSpec((None, heads, rows, arr.shape[3]),
                            lambda bi, h, i: (0, hb + h, bi * per_batch + (i if tiled else 0), 0))
    return pl.BlockSpec((None, heads, rows, arr.shape[3]), lambda bi, h, i: (bi, hb + h, i if tiled else 0, 0))


def _dense_attention(q, kv_list, nh, q_off, name, batch, tq_target=1024, chunk_target=512, heads=2):
    dk = q.shape[3]
    sq = q.shape[0] * q.shape[2] // batch
    dv = kv_list[0][1].shape[3]
    offsets = [q_off] + [o for _, _, ko, vo in kv_list for o in (ko, vo)]
    heads = functools.reduce(math.gcd, [nh] + offsets, heads)
    tq = _pick_tile(sq, tq_target)
    nq = sq // tq
    chunks, arrays_k, arrays_v, scratch = [], [], [], []
    blocks = heads * (_nbytes((tq, dk), BF16) + _nbytes((tq, dv), BF16))
    tmp = 4 * _nbytes((tq, chunk_target), F32) + 2 * _nbytes((tq, 2 * dv), F32)
    kspecs, vspecs = [], []
    for src, (k, v, k_off, v_off) in enumerate(kv_list):
        sk = k.shape[0] * k.shape[2] // batch
        chunks += [(src, st, min(chunk_target, sk - st)) for st in range(0, sk, chunk_target)]
        kspecs.append(_head_major_spec(k, batch, sk, k_off, False, heads))
        vspecs.append(_head_major_spec(v, batch, sk, v_off, False, heads))
        arrays_k.append(k)
        arrays_v.append(v)
        scratch.append(pltpu.VMEM((heads, sk, 2 * dv), BF16))
        blocks += heads * (_nbytes((sk, dk), BF16) + _nbytes((sk, dv), BF16))
        tmp += heads * _nbytes((sk, 2 * dv), BF16)
    if q.shape[0] == 1:
        out_spec = pl.BlockSpec((None, tq, heads * dv), lambda bi, h, i: (0, bi * nq + i, h))
    else:
        out_spec = pl.BlockSpec((None, tq, heads * dv), lambda bi, h, i: (bi, i, h))
    return pl.pallas_call(
        functools.partial(_dense_attn_kernel, n_src=len(kv_list), chunks=chunks, heads=heads),
        grid=(batch, nh // heads, nq),
        in_specs=[_head_major_spec(q, batch, tq, q_off, True, heads)] + kspecs + vspecs,
        out_specs=out_spec,
        out_shape=jax.ShapeDtypeStruct((q.shape[0], q.shape[2], nh * dv), BF16),
        scratch_shapes=scratch,
        compiler_params=_params(blocks, tmp, 3),
        name=name,
    )(q, *arrays_k, *arrays_v)


def _pool_kernel(u_ref, w_ref, sc_ref, o_ref, *, windows):
    n, gd = u_ref.shape[0], w_ref.shape[1]
    t = lax.broadcasted_iota(jnp.int32, (n, gd), 0)
    for g, window in enumerate(windows):
        cols = slice(g * gd, (g + 1) * gd)
        u = u_ref[:, cols]
        half = window // 2
        ssum = u
        for j in range(1, half + 1):
            ssum = ssum + jnp.where(t >= j, pltpu.roll(u, j, axis=0), 0.0)
        for j in range(1, half):
            ssum = ssum + jnp.where(t < n - j, pltpu.roll(u, n - j, axis=0), 0.0)
        cnt = jnp.minimum(t + half, n) - jnp.maximum(t - half, 0)
        p = ssum / cnt.astype(F32) - u
        y = jnp.dot(p.astype(BF16), w_ref[g], preferred_element_type=F32)
        o_ref[:, cols] = (y * sc_ref[:, cols]).astype(o_ref.dtype)


def _pool_mix(u, w_pool, pool_scale):
    b, s, w = u.shape
    g, gd, _ = w_pool.shape
    blocks = _nbytes((s, w), F32) + _nbytes((g, gd, gd), BF16) + _nbytes((s, w), BF16)
    return pl.pallas_call(
        functools.partial(_pool_kernel, windows=POOL_WINDOWS[:g]),
        grid=(b,),
        in_specs=[
            pl.BlockSpec((None, s, w), lambda bi: (bi, 0, 0)),
            pl.BlockSpec((g, gd, gd), lambda bi: (0, 0, 0)),
            pl.BlockSpec((1, w), lambda bi: (0, 0)),
        ],
        out_specs=pl.BlockSpec((None, s, w), lambda bi: (bi, 0, 0)),
        out_shape=jax.ShapeDtypeStruct((b, s, w), BF16),
        compiler_params=_params(blocks, 6 * _nbytes((s, gd), F32), 1),
        name="pool_mix",
    )(u, w_pool.astype(BF16), pool_scale.reshape(1, w))


def _rope_table(n):
    half = QK_ROPE_DIM // 2
    nf = half // 2
    t = np.arange(n)
    inv = ROPE_BASE ** (-np.arange(nf, dtype=np.float64) / nf)
    ar = (t // GRID_W)[:, None] * inv[None, :]
    ac = (t % GRID_W)[:, None] * inv[None, :]
    cr, sr, cc, sc = np.cos(ar), np.sin(ar), np.cos(ac), np.sin(ac)
    return jnp.asarray(np.concatenate([cr, cr, cc, cc, -sr, sr, -sc, sc], axis=1), dtype=F32)


def _identity_rope_table(n):
    return jnp.concatenate([jnp.ones((n, QK_ROPE_DIM), F32), jnp.zeros((n, QK_ROPE_DIM), F32)], axis=1)


def _swap_rope_cols(w):
    nf = QK_ROPE_DIM // 4
    lead = w.shape[:-1]
    return jnp.flip(w.reshape(lead + (2, 2, nf)), axis=-2).reshape(lead + (QK_ROPE_DIM,))


def _even_mixer(h, hc, w_in, layer, w_four, rpb, with_ctx_out):
    fw = w_four.shape[0] * w_four.shape[1]
    nh = rpb.shape[0]
    u = _mm_plain(h, w_in, BF16, "even_in_fourier", layer, 0, fw, tm_target=2048)
    qkv = _mm_chunked(h, w_in, "even_in_qkv", layer, fw, tm_target=2048)
    qkvc = _mm_chunked(hc, w_in, "even_in_qkv_ctx", layer, fw)
    y_f = _four_channels(_dft_positions(u), w_four)
    y_a = _na_attention(qkv, qkvc, rpb)
    ctx_ops = None
    if with_ctx_out:
        b = h.shape[0]
        uc = _mm_plain(hc, w_in, BF16, "even_in_fourier_ctx", layer, 0, fw).reshape(b, -1, fw)
        yc_f = _four_channels(_dft_positions(uc), w_four).reshape(1, -1, fw)
        yc_a = _dense_attention(qkvc, [(qkvc, qkvc, nh, 2 * nh)], nh, 0, "ctx_attention_even", b)
        ctx_ops = [yc_f, yc_a]
    return [y_f, y_a], ctx_ops


def _ep_odd_in(acc, e_refs, o_refs, *, splits):
    gq_ref, gkv_ref, tab_ref = e_refs
    u_ref, cq_ref, ckv_ref, kr_ref = o_refs
    o_q, o_kv, o_kr = splits

    def rms(v, g_ref):
        return (v * lax.rsqrt(jnp.mean(v * v, axis=-1, keepdims=True) + EPS) * g_ref[...]).astype(BF16)

    u_ref[...] = acc[:, 0:o_q]
    cq_ref[...] = rms(acc[:, o_q:o_kv], gq_ref)
    ckv_ref[...] = rms(acc[:, o_kv:o_kr], gkv_ref)
    r = _rope_pair(acc[:, o_kr:], tab_ref[...])
    lane = lax.broadcasted_iota(jnp.int32, r.shape, 1)
    kr_ref[...] = jnp.where(lane < QK_ROPE_DIM, r, 0.0).astype(kr_ref.dtype)


def _mm_odd_in(a, w, g_q, g_kv, tab, splits, name, tm_target=1024):
    b, s, k = a.shape
    n = w.shape[1]
    o_q, o_kv, o_kr = splits
    tm = _pick_tile(s, tm_target)

    def row(width):
        return pl.BlockSpec((None, tm, width), lambda bi, i, j: (bi, i, 0))

    def vec(width):
        return pl.BlockSpec((1, width), lambda bi, i, j: (0, 0))

    extras = [(g_q.reshape(1, -1), vec(o_kv - o_q)), (g_kv.reshape(1, -1), vec(o_kr - o_kv)),
              (tab, pl.BlockSpec((tm, n - o_kr), lambda bi, i, j: (i, 0)))]
    outs = [(jax.ShapeDtypeStruct((b, s, o_q), F32), row(o_q)),
            (jax.ShapeDtypeStruct((b, s, o_kv - o_q), BF16), row(o_kv - o_q)),
            (jax.ShapeDtypeStruct((b, s, o_kr - o_kv), BF16), row(o_kr - o_kv)),
            (jax.ShapeDtypeStruct((b, s, n - o_kr), BF16), row(n - o_kr))]
    return _mm_call([(a, _a_spec(tm, k))], [(_w3(w), _w_spec(k, n))], extras,
                    functools.partial(_ep_odd_in, splits=splits), outs, (b, s // tm, 1), tm, n, name)


def _odd_mixer(h, hc, w_in, w_pool, pool_scale, g_q, g_kv, w_uq, w_ukv, with_ctx_out):
    b, s, d = h.shape
    lc = hc.shape[1] // b
    pw = w_pool.shape[0] * w_pool.shape[1]
    q_rank, kv_rank = g_q.shape[0], g_kv.shape[0]
    nh = w_uq.shape[1] // (QK_NOPE_DIM + QK_ROPE_DIM)
    splits = (pw, pw + q_rank, pw + q_rank + kv_rank)
    w_in = w_in.astype(BF16)
    w_in2 = jnp.concatenate([w_in, _swap_rope_cols(w_in[:, splits[2]:])], axis=1)
    q_scale = (QK_NOPE_DIM + QK_ROPE_DIM) ** -0.5 * LOG2E
    wq = (w_uq * q_scale).astype(BF16).reshape(q_rank, nh, QK_NOPE_DIM + QK_ROPE_DIM)
    wq_r = wq[..., QK_NOPE_DIM:]
    wq2 = jnp.concatenate([wq[..., :QK_NOPE_DIM], wq_r, _swap_rope_cols(wq_r)], axis=-1).reshape(q_rank, -1)
    w_ukv = w_ukv.astype(BF16)
    tab, tab_c = _rope_table(s), _identity_rope_table(b * lc)

    u, cq_n, ckv_n, k_rot = _mm_odd_in(h, w_in2, g_q, g_kv, tab, splits, "odd_in")
    uc, cqc_n, ckvc_n, k_rot_c = _mm_odd_in(hc, w_in2, g_q, g_kv, tab_c, splits, "odd_in_ctx")
    k_l, v_l = _mm_kv(ckv_n, w_ukv, k_rot, "odd_up_kv")
    k_c, v_c = _mm_kv(ckvc_n, w_ukv, k_rot_c, "odd_up_kv_ctx")
    q = _mm_qrope(cq_n, wq2, tab, "odd_up_q")
    attn = _dense_attention(q, [(k_l, v_l, 0, 0), (k_c, v_c, 0, 0)], nh, 0, "mla_attention", b)
    pooled = _pool_mix(u, w_pool, pool_scale)
    ctx_ops = None
    if with_ctx_out:
        qc = _mm_qrope(cqc_n, wq2, tab_c, "odd_up_q_ctx")
        attn_c = _dense_attention(qc, [(k_c, v_c, 0, 0)], nh, 0, "ctx_attention_odd", b)
        pooled_c = _pool_mix(uc.reshape(b, lc, pw), w_pool, pool_scale).reshape(1, b * lc, pw)
        ctx_ops = [pooled_c, attn_c]
    return [pooled, attn], ctx_ops


def _ffn(xs, norm_g, shift, scale, gate, wg, wu, wd, layer, tag):
    h2 = _norm_call(xs, norm_g, shift, scale)
    act = _swiglu_up(h2, wg, wu, layer, "ffn_up_" + tag)
    return _mm_residual([act], wd, xs, gate, "ffn_down_" + tag, layer)


def kernel(x, c, ctx, c_ctx, w_mod, b_mod, norm1_g, norm2_g, w_in_ab, w_four, na_rpb, w_out_ab, w_in_cd, w_pool,
           pool_scale, mla_gq, mla_gkv, w_uq, w_ukv, w_out_cd, w_ffn_gate, w_ffn_up, w_ffn_down, final_g):
    b, s, d = x.shape
    depth = w_mod.shape[0]
    nrow = -(-(b + 1) // 8) * 8
    cvec = jnp.concatenate([c, c_ctx[None, :], jnp.zeros((nrow - b - 1, d), F32)], axis=0)
    mods = _mod_call(cvec, w_mod, b_mod)
    fw, na_width = w_four.shape[1] * w_four.shape[2], na_rpb.shape[1] * HEAD_DIM
    col_scale = np.ones((w_in_ab.shape[2],), np.float32)
    col_scale[fw:fw + na_width] = HEAD_DIM ** -0.5 * LOG2E
    w_in_ab = (w_in_ab * jnp.asarray(col_scale)).astype(BF16)
    w_out_ab, w_out_cd = w_out_ab.astype(BF16), w_out_cd.astype(BF16)
    wg, wu, wd = w_ffn_gate, w_ffn_up, w_ffn_down.astype(BF16)
    xc = ctx.reshape(1, -1, d)
    for l in range(depth):
        last = l == depth - 1
        mod = mods[l, :b].reshape(b, 1, 6 * d)
        modc = mods[l, b].reshape(1, 1, 6 * d)
        sh1, sc1, g1, sh2, sc2, g2 = jnp.split(mod, 6, axis=-1)
        csh1, csc1, cg1, csh2, csc2, cg2 = jnp.split(modc, 6, axis=-1)
        h = _norm_call(x, norm1_g[l], sh1, sc1)
        hc = _norm_call(xc, norm1_g[l], csh1, csc1)
        i = l // 2
        if l % 2 == 0:
            ops, ctx_ops = _even_mixer(h, hc, w_in_ab, i, w_four[i], na_rpb[i], not last)
            w_out = w_out_ab
        else:
            ops, ctx_ops = _odd_mixer(h, hc, w_in_cd[i], w_pool[i], pool_scale[i], mla_gq[i], mla_gkv[i],
                                      w_uq[i], w_ukv[i], not last)
            w_out = w_out_cd
        x = _mm_residual(ops, w_out, x, g1, f"out_proj_{l}", i, tm_target=512, tn_target=d)
        x = _ffn(x, norm2_g[l], sh2, sc2, g2, wg, wu, wd, l, f"{l}")
        if not last:
            xc = _mm_residual(ctx_ops, w_out, xc, cg1, f"out_proj_ctx_{l}", i)
            xc = _ffn(xc, norm2_g[l], csh2, csc2, cg2, wg, wu, wd, l, f"ctx_{l}")
    return _norm_call(x, final_g, out_dtype=F32)
```
